```python
import math
import jax, jax.numpy as jnp
from jax import lax
import numpy as np

D_MODEL = 2048
BATCH = 4
SEQ = 8192
DEPTH = 2

SC_W = 512
SC_CONV = 3
SSM_W = 512
SSM_GROUP = 16
SSM_GROUPS = SSM_W // SSM_GROUP
SSM_STATE = 64
N_Q_HEADS = 16
N_KV_HEADS = 4
HEAD_DIM = 64
ATT_W = N_Q_HEADS * HEAD_DIM
KV_W = N_KV_HEADS * HEAD_DIM
ROT_DIM = HEAD_DIM // 4
ROPE_THETA = 500000.0
N_IDX_HEADS = 8
IDX_DIM = 64
TOPK_MAX = 256
BLOCK_Q = 128
N_BRANCH = 3
D_FF = 5632
FFN_CONV = 3
NORM_EPS = 1e-6

SPLIT_SIZES = (SC_W, SC_W, SC_W, SSM_W, ATT_W, KV_W, KV_W,
               N_IDX_HEADS * IDX_DIM, IDX_DIM, N_IDX_HEADS,
               D_MODEL, D_MODEL, D_MODEL)
N_IN = sum(SPLIT_SIZES)

kernel_name = "hybrid_conv_s5_dsa_block"


def rms_norm(x, g):
    xf = x.astype(jnp.float32)
    y = xf * lax.rsqrt(jnp.mean(xf * xf, axis=-1, keepdims=True) + NORM_EPS)
    return (y * g.astype(jnp.float32)).astype(x.dtype)


def causal_dwconv(x, w):
    width = w.shape[0]
    seq = x.shape[1]
    w = w.astype(x.dtype)
    xp = jnp.pad(x, ((0, 0), (width - 1, 0), (0, 0)))
    out = w[width - 1] * x
    for j in range(width - 1):
        out = out + w[j] * xp[:, j:j + seq]
    return out


def rope_tables(positions):
    inv_freq = ROPE_THETA ** (-jnp.arange(0, ROT_DIM, 2, dtype=jnp.float32) / ROT_DIM)
    ang = positions.astype(jnp.float32)[..., None] * inv_freq
    return jnp.cos(ang)[:, :, None, :], jnp.sin(ang)[:, :, None, :]


def partial_rope(x, cos, sin):
    half = ROT_DIM // 2
    x1 = x[..., :half].astype(jnp.float32)
    x2 = x[..., half:ROT_DIM].astype(jnp.float32)
    rot = jnp.concatenate([x1 * cos - x2 * sin, x2 * cos + x1 * sin], axis=-1).astype(x.dtype)
    return jnp.concatenate([rot, x[..., ROT_DIM:]], axis=-1)


def _ssm_combine(left, right):
    a_l, b_l = left
    a_r, b_r = right
    return a_l * a_r, a_r * b_l + b_r


def s5_mixer(u, a_re, a_im, log_dt, b_re, b_im, c_re, c_im, d_skip, w_glu):
    bsz, seq, _ = u.shape
    uf = u.astype(jnp.float32).reshape(bsz, seq, SSM_GROUPS, SSM_GROUP)
    a = lax.complex(a_re.astype(jnp.float32), a_im.astype(jnp.float32))
    dt = jnp.exp(log_dt.astype(jnp.float32))[:, None]
    a_bar = jnp.exp(dt * a)
    b_bar = ((a_bar - 1.0) / a)[..., None] * lax.complex(
        b_re.astype(jnp.float32), b_im.astype(jnp.float32))
    bu = jnp.einsum('bsgp,gnp->bsgn', uf.astype(jnp.complex64), b_bar)
    a_seq = jnp.broadcast_to(a_bar, (1, seq) + a_bar.shape)
    _, state = lax.associative_scan(_ssm_combine, (a_seq, bu), axis=1)
    c = lax.complex(c_re.astype(jnp.float32), c_im.astype(jnp.float32))
    y = jnp.real(jnp.einsum('bsgn,gpn->bsgp', state, c)) \
        + d_skip.astype(jnp.float32).reshape(SSM_GROUPS, SSM_GROUP) * uf
    z = jax.nn.gelu(y.reshape(bsz, seq, SSM_W))
    z = z * jax.nn.sigmoid(z @ w_glu.astype(jnp.float32))
    return z.astype(u.dtype)


def dsa_attention(q, k, v, qi, ki, wi):
    bsz, seq = q.shape[0], q.shape[1]
    k_sel = min(TOPK_MAX, seq // 4)
    n_blk = seq // BLOCK_Q
    rep = N_Q_HEADS // N_KV_HEADS

    def to_blocks(t):
        return jnp.swapaxes(t.reshape((bsz, n_blk, BLOCK_Q) + t.shape[2:]), 0, 1)

    qb = to_blocks(q.reshape(bsz, seq, N_KV_HEADS, rep, HEAD_DIM))
    qib = to_blocks(qi)
    wib = to_blocks(wi)
    t0s = jnp.arange(n_blk, dtype=jnp.int32) * BLOCK_Q
    key_pos = jnp.arange(seq, dtype=jnp.int32)
    bidx = jnp.arange(bsz)[:, None, None]

    def one_block(args):
        q_b, qi_b, wi_b, t0 = args
        t = t0 + jnp.arange(BLOCK_Q, dtype=jnp.int32)
        dots = jnp.einsum('bqhd,bsd->bqhs', qi_b, ki).astype(jnp.float32) * (IDX_DIM ** -0.5)
        iscore = jnp.einsum('bqhs,bqh->bqs', jax.nn.relu(dots),
                            wi_b.astype(jnp.float32) * (N_IDX_HEADS ** -0.5))
        causal = key_pos[None, :] <= t[:, None]
        iscore = jnp.where(causal[None], iscore, -jnp.inf)
        _, idx = lax.top_k(iscore, k_sel)
        valid = idx <= t[None, :, None]
        kg = k[bidx, idx]
        vg = v[bidx, idx]
        logits = jnp.einsum('bqgrd,bqkgd->bqgrk', q_b, kg).astype(jnp.float32) * (HEAD_DIM ** -0.5)
        logits = jnp.where(valid[:, :, None, None, :], logits, -jnp.inf)
        p = jax.nn.softmax(logits, axis=-1).astype(vg.dtype)
        return jnp.einsum('bqgrk,bqkgd->bqgrd', p, vg)

    ob = lax.map(one_block, (qb, qib, wib, t0s))
    return jnp.swapaxes(ob, 0, 1).reshape(bsz, seq, ATT_W)


def hybrid_mixer(h, cos, sin, w_in, sc_conv, ssm_a_re, ssm_a_im, ssm_log_dt, ssm_b_re, ssm_b_im,
                 ssm_c_re, ssm_c_im, ssm_d, ssm_glu, w_branch_a, w_branch_b, w_branch_c, w_out):
    bsz, seq, _ = h.shape
    offsets = []
    acc = 0
    for s in SPLIT_SIZES[:-1]:
        acc += s
        offsets.append(acc)
    (sc_x, sc_b, sc_c, ssm_u, q, k, v, qi, ki, wi,
     g_a, g_b, g_c) = jnp.split(h @ w_in, offsets, axis=-1)

    y_a = sc_b * causal_dwconv(sc_c * sc_x, sc_conv)
    y_b = s5_mixer(ssm_u, ssm_a_re, ssm_a_im, ssm_log_dt, ssm_b_re, ssm_b_im,
                   ssm_c_re, ssm_c_im, ssm_d, ssm_glu)
    q = partial_rope(q.reshape(bsz, seq, N_Q_HEADS, HEAD_DIM), cos, sin)
    k = partial_rope(k.reshape(bsz, seq, N_KV_HEADS, HEAD_DIM), cos, sin)
    v = v.reshape(bsz, seq, N_KV_HEADS, HEAD_DIM)
    qi = partial_rope(qi.reshape(bsz, seq, N_IDX_HEADS, IDX_DIM), cos, sin)
    ki = partial_rope(ki[:, :, None, :], cos, sin)[:, :, 0, :]
    y_c = dsa_attention(q, k, v, qi, ki, wi)

    merged = (jax.nn.sigmoid(g_a) * (y_a @ w_branch_a)
              + jax.nn.sigmoid(g_b) * (y_b @ w_branch_b)
              + jax.nn.sigmoid(g_c) * (y_c @ w_branch_c))
    return merged @ w_out


def conv_glu_ffn(h, w_up, conv_w, w_down):
    up = causal_dwconv(h @ w_up, conv_w)
    g, u = jnp.split(up, 2, axis=-1)
    return (jax.nn.silu(g) * u) @ w_down


def setup_inputs(seed: int = 0) -> dict:
    key = jax.random.key(seed)
    ks = jax.random.split(key, 24)
    f32 = jnp.float32

    def nrm(k, shape, scale):
        return jax.random.normal(k, shape, f32) * scale

    x = jax.random.normal(ks[0], (BATCH, SEQ, D_MODEL), f32)
    positions = jnp.broadcast_to(jnp.arange(SEQ, dtype=jnp.int32), (BATCH, SEQ))
    norm_mix = 1.0 + nrm(ks[1], (DEPTH, D_MODEL), 0.02)
    w_in = nrm(ks[2], (DEPTH, D_MODEL, N_IN), D_MODEL ** -0.5)
    sc_conv = nrm(ks[3], (DEPTH, SC_CONV, SC_W), SC_CONV ** -0.5)
    ssm_a_re = -0.5 + nrm(ks[4], (DEPTH, SSM_GROUPS, SSM_STATE), 0.01)
    ssm_a_im = math.pi * jnp.arange(SSM_STATE, dtype=f32) + nrm(ks[5], (DEPTH, SSM_GROUPS, SSM_STATE), 0.01)
    ssm_log_dt = jax.random.uniform(ks[6], (DEPTH, SSM_GROUPS), f32, math.log(1e-3), math.log(1e-1))
    b_scale = (2.0 * SSM_GROUP) ** -0.5
    c_scale = (2.0 * SSM_STATE) ** -0.5
    ssm_b_re = nrm(ks[7], (DEPTH, SSM_GROUPS, SSM_STATE, SSM_GROUP), b_scale)
    ssm_b_im = nrm(ks[8], (DEPTH, SSM_GROUPS, SSM_STATE, SSM_GROUP), b_scale)
    ssm_c_re = nrm(ks[9], (DEPTH, SSM_GROUPS, SSM_GROUP, SSM_STATE), c_scale)
    ssm_c_im = nrm(ks[10], (DEPTH, SSM_GROUPS, SSM_GROUP, SSM_STATE), c_scale)
    ssm_d = nrm(ks[11], (DEPTH, SSM_W), 1.0)
    ssm_glu = nrm(ks[12], (DEPTH, SSM_W, SSM_W), SSM_W ** -0.5)
    w_branch_a = nrm(ks[13], (DEPTH, SC_W, D_MODEL), SC_W ** -0.5)
    w_branch_b = nrm(ks[14], (DEPTH, SSM_W, D_MODEL), SSM_W ** -0.5)
    w_branch_c = nrm(ks[15], (DEPTH, ATT_W, D_MODEL), ATT_W ** -0.5)
    w_out = nrm(ks[16], (DEPTH, D_MODEL, D_MODEL), D_MODEL ** -0.5)
    norm_ffn = 1.0 + nrm(ks[17], (DEPTH, D_MODEL), 0.02)
    w_up = nrm(ks[18], (DEPTH, D_MODEL, 2 * D_FF), D_MODEL ** -0.5)
    ffn_conv = nrm(ks[19], (DEPTH, FFN_CONV, 2 * D_FF), FFN_CONV ** -0.5)
    w_down = nrm(ks[20], (DEPTH, D_FF, D_MODEL), D_FF ** -0.5)
    norm_final = 1.0 + nrm(ks[21], (D_MODEL,), 0.02)
    return {"x": x, "positions": positions, "norm_mix": norm_mix, "w_in": w_in,
            "sc_conv": sc_conv, "ssm_a_re": ssm_a_re, "ssm_a_im": ssm_a_im,
            "ssm_log_dt": ssm_log_dt, "ssm_b_re": ssm_b_re, "ssm_b_im": ssm_b_im,
            "ssm_c_re": ssm_c_re, "ssm_c_im": ssm_c_im, "ssm_d": ssm_d, "ssm_glu": ssm_glu,
            "w_branch_a": w_branch_a, "w_branch_b": w_branch_b, "w_branch_c": w_branch_c,
            "w_out": w_out, "norm_ffn": norm_ffn, "w_up": w_up, "ffn_conv": ffn_conv,
            "w_down": w_down, "norm_final": norm_final}


def reference(x, positions, norm_mix, w_in, sc_conv, ssm_a_re, ssm_a_im, ssm_log_dt,
              ssm_b_re, ssm_b_im, ssm_c_re, ssm_c_im, ssm_d, ssm_glu, w_branch_a,
              w_branch_b, w_branch_c, w_out, norm_ffn, w_up, ffn_conv, w_down, norm_final):
    cos, sin = rope_tables(positions)
    for l in range(DEPTH):
        h = rms_norm(x, norm_mix[l])
        x = x + hybrid_mixer(h, cos, sin, w_in[l], sc_conv[l], ssm_a_re[l], ssm_a_im[l],
                             ssm_log_dt[l], ssm_b_re[l], ssm_b_im[l], ssm_c_re[l], ssm_c_im[l],
                             ssm_d[l], ssm_glu[l], w_branch_a[l], w_branch_b[l], w_branch_c[l],
                             w_out[l])
        h = rms_norm(x, norm_ffn[l])
        x = x + conv_glu_ffn(h, w_up[l], ffn_conv[l], w_down[l])
    return rms_norm(x, norm_final)
```

```python
import functools
import math

import jax
import jax.numpy as jnp
from jax import lax
from jax.experimental import pallas as pl
from jax.experimental.pallas import tpu as pltpu

F32 = jnp.float32
BF16 = jnp.bfloat16
I32 = jnp.int32

SC_W = 512
SC_CONV = 3
SSM_W = 512
SSM_GROUP = 16
SSM_GROUPS = SSM_W // SSM_GROUP
SSM_STATE = 64
SSM_NS = SSM_GROUPS * SSM_STATE
N_Q_HEADS = 16
N_KV_HEADS = 4
HEAD_DIM = 64
ATT_W = N_Q_HEADS * HEAD_DIM
KV_W = N_KV_HEADS * HEAD_DIM
REP = N_Q_HEADS // N_KV_HEADS
ROT_DIM = HEAD_DIM // 4
ROT_HALF = ROT_DIM // 2
ROPE_THETA = 500000.0
N_IDX_HEADS = 8
IDX_DIM = 64
TOPK_MAX = 256
D_FF = 5632
FFN_CONV = 3
NORM_EPS = 1e-6

LANES = 128
SUBLANES = 8
VMEM_LIMIT = 56 * 1024 * 1024
NEG_BIG = -1e30
INT_MIN = -(2 ** 31)

_ORIG_GATE_OFF = 3 * SC_W + SSM_W + ATT_W + 2 * KV_W + N_IDX_HEADS * IDX_DIM + IDX_DIM + N_IDX_HEADS
OFF_SCX = 3 * 2048
OFF_SCB = OFF_SCX + SC_W
OFF_SCC = OFF_SCB + SC_W
OFF_SSM = OFF_SCC + SC_W
OFF_Q = OFF_SSM + SSM_W
OFF_K = OFF_Q + ATT_W
OFF_V = OFF_K + KV_W
OFF_QI = OFF_V + KV_W
OFF_KI = OFF_QI + N_IDX_HEADS * IDX_DIM
NP_COLS = OFF_KI + LANES


def _cparams(sem, vmem=VMEM_LIMIT):
    return pltpu.CompilerParams(dimension_semantics=sem, vmem_limit_bytes=vmem)


def _pick(n, prefs):
    for p in prefs:
        if n % p == 0:
            return p
    return n


def _inproj_kernel(x_ref, g_ref, w_ref, o_ref, h_ref):
    @pl.when(pl.program_id(1) == 0)
    def _():
        x = x_ref[...]
        ms = jnp.mean(x * x, axis=-1, keepdims=True)
        h_ref[...] = (x * lax.rsqrt(ms + NORM_EPS) * g_ref[...]).astype(BF16)

    o_ref[...] = jnp.dot(h_ref[...], w_ref[...], preferred_element_type=F32).astype(o_ref.dtype)


def _inproj(x2, gain, w):
    t, d = x2.shape
    n = w.shape[1]
    tm = _pick(t, (1024, 512, 256, 128))
    tn = _pick(n, (1152, 384, 128))
    return pl.pallas_call(
        _inproj_kernel,
        grid=(t // tm, n // tn),
        in_specs=[pl.BlockSpec((tm, d), lambda i, j: (i, 0)),
                  pl.BlockSpec((1, d), lambda i, j: (0, 0)),
                  pl.BlockSpec((d, tn), lambda i, j: (0, j))],
        out_specs=pl.BlockSpec((tm, tn), lambda i, j: (i, j)),
        out_shape=jax.ShapeDtypeStruct((t, n), BF16),
        scratch_shapes=[pltpu.VMEM((tm, d), BF16)],
        compiler_params=_cparams(("parallel", "arbitrary")),
        name="inproj",
    )(x2, gain, w)


def _rope_table_kernel(pos_ref, invf_ref, a_ref, bm_ref, bp_ref):
    ang = pos_ref[...].astype(F32) * invf_ref[...]
    c = jnp.cos(ang)
    s = jnp.sin(ang)
    r = lax.broadcasted_iota(I32, ang.shape, 1) % HEAD_DIM
    lo = r < ROT_HALF
    hi = jnp.logical_and(r >= ROT_HALF, r < ROT_DIM)
    a_ref[...] = jnp.where(r < ROT_DIM, c, 1.0)
    bm_ref[...] = jnp.where(lo, -s, 0.0)
    bp_ref[...] = jnp.where(hi, s, 0.0)


def _rope_tables(positions):
    t = positions.size
    posb = jnp.broadcast_to(positions.reshape(t, 1), (t, LANES))
    inv_freq = ROPE_THETA ** (-jnp.arange(0, ROT_DIM, 2, dtype=F32) / ROT_DIM)
    lane = jnp.arange(LANES) % ROT_HALF
    invf = inv_freq[lane].reshape(1, LANES)
    tm = _pick(t, (1024, 512, 256, 128))
    spec = pl.BlockSpec((tm, LANES), lambda i: (i, 0))
    return pl.pallas_call(
        _rope_table_kernel,
        grid=(t // tm,),
        in_specs=[spec, pl.BlockSpec((1, LANES), lambda i: (0, 0))],
        out_specs=[spec, spec, spec],
        out_shape=[jax.ShapeDtypeStruct((t, LANES), F32)] * 3,
        compiler_params=_cparams(("parallel",)),
        name="rope_tables",
    )(posb, invf)


def _rope_kernel(q_ref, k_ref, v_ref, qi_ref, kiw_ref, a_ref, bm_ref, bp_ref,
                 qo_ref, ko_ref, vo_ref, qio_ref, kio_ref, wo_ref):
    a = a_ref[...]
    bm = bm_ref[...]
    bp = bp_ref[...]

    def rope(x):
        w = x.shape[1]
        reps = w // LANES
        xf = x.astype(F32)
        up = pltpu.roll(xf, w - ROT_HALF, 1)
        dn = pltpu.roll(xf, ROT_HALF, 1)
        return (xf * jnp.tile(a, (1, reps)) + up * jnp.tile(bm, (1, reps))
                + dn * jnp.tile(bp, (1, reps)))

    qo_ref[...] = (rope(q_ref[...]) * (HEAD_DIM ** -0.5)).astype(BF16)
    kr = rope(k_ref[...]).astype(BF16)
    v = v_ref[...]
    for g in range(N_KV_HEADS):
        ko_ref[0, g] = kr[:, g * HEAD_DIM:(g + 1) * HEAD_DIM]
        vo_ref[0, g] = v[:, g * HEAD_DIM:(g + 1) * HEAD_DIM]
    qio_ref[...] = (rope(qi_ref[...]) * (IDX_DIM ** -0.5)).astype(BF16)
    kiw = kiw_ref[...]
    kio_ref[...] = rope(kiw).astype(BF16)
    wo_ref[...] = kiw.astype(F32) * (N_IDX_HEADS ** -0.5)


def _rope(p2, tabs, bsz, seq):
    t = p2.shape[0]
    tm = _pick(seq, (512, 256, 128))
    nsb = seq // tm
    a, bm, bp = tabs

    def col(width, off):
        return pl.BlockSpec((tm, width), lambda i, o=off // width: (i, o))

    tab = pl.BlockSpec((tm, LANES), lambda i: (i, 0))
    hm = pl.BlockSpec((1, N_KV_HEADS, tm, HEAD_DIM), lambda i: (i // nsb, 0, i % nsb, 0))
    row = lambda w: pl.BlockSpec((tm, w), lambda i: (i, 0))
    return pl.pallas_call(
        _rope_kernel,
        grid=(t // tm,),
        in_specs=[col(ATT_W, OFF_Q), col(KV_W, OFF_K), col(KV_W, OFF_V),
                  col(N_IDX_HEADS * IDX_DIM, OFF_QI), col(LANES, OFF_KI), tab, tab, tab],
        out_specs=[row(ATT_W), hm, hm, row(N_IDX_HEADS * IDX_DIM), row(LANES), row(LANES)],
        out_shape=[jax.ShapeDtypeStruct((t, ATT_W), BF16),
                   jax.ShapeDtypeStruct((bsz, N_KV_HEADS, seq, HEAD_DIM), BF16),
                   jax.ShapeDtypeStruct((bsz, N_KV_HEADS, seq, HEAD_DIM), BF16),
                   jax.ShapeDtypeStruct((t, N_IDX_HEADS * IDX_DIM), BF16),
                   jax.ShapeDtypeStruct((t, LANES), BF16),
                   jax.ShapeDtypeStruct((t, LANES), F32)],
        compiler_params=_cparams(("parallel",)),
        name="rope",
    )(p2, p2, p2, p2, p2, a, bm, bp)


def _ssm_kernel(u_ref, bbd_ref, cbd_ref, a_ref, d_ref, wg_ref, o_ref, bu_ref, st_ref, *, chunk):
    @pl.when(pl.program_id(1) == 0)
    def _():
        st_ref[...] = jnp.zeros_like(st_ref)

    u = u_ref[0]
    bu_ref[...] = jnp.dot(u, bbd_ref[...], preferred_element_type=F32)
    ar = a_ref[0:1, :]
    ai = a_ref[1:2, :]

    def step(t, carry):
        xr, xi = carry
        br = bu_ref[pl.ds(t, 1), 0:SSM_NS]
        bi = bu_ref[pl.ds(t, 1), SSM_NS:2 * SSM_NS]
        nr = ar * xr - ai * xi + br
        ni = ar * xi + ai * xr + bi
        bu_ref[pl.ds(t, 1), 0:SSM_NS] = nr
        bu_ref[pl.ds(t, 1), SSM_NS:2 * SSM_NS] = ni
        return nr, ni

    xr, xi = lax.fori_loop(0, chunk, step, (st_ref[0:1, :], st_ref[1:2, :]), unroll=8)
    st_ref[0:1, :] = xr
    st_ref[1:2, :] = xi

    y = jnp.dot(bu_ref[...].astype(BF16), cbd_ref[...], preferred_element_type=F32)
    y = y + d_ref[...] * u.astype(F32)
    z = jax.nn.gelu(y)
    gate = jnp.dot(z.astype(BF16), wg_ref[...], preferred_element_type=F32)
    o_ref[0] = (z * jax.nn.sigmoid(gate)).astype(o_ref.dtype)


def _ssm(p3, bbd, cbd, a_bar, d_skip, w_glu):
    bsz, seq, _ = p3.shape
    chunk = _pick(seq, (256, 128))
    const = lambda shape: pl.BlockSpec(shape, lambda b, c: (0,) * len(shape))
    return pl.pallas_call(
        functools.partial(_ssm_kernel, chunk=chunk),
        grid=(bsz, seq // chunk),
        in_specs=[pl.BlockSpec((1, chunk, SSM_W), lambda b, c: (b, c, OFF_SSM // SSM_W)),
                  const((SSM_W, 2 * SSM_NS)), const((2 * SSM_NS, SSM_W)),
                  const((2, SSM_NS)), const((1, SSM_W)), const((SSM_W, SSM_W))],
        out_specs=pl.BlockSpec((1, chunk, SSM_W), lambda b, c: (b, c, 0)),
        out_shape=jax.ShapeDtypeStruct((bsz, seq, SSM_W), BF16),
        scratch_shapes=[pltpu.VMEM((chunk, 2 * SSM_NS), F32), pltpu.VMEM((2, SSM_NS), F32)],
        compiler_params=_cparams(("parallel", "arbitrary")),
        name="ssm",
    )(p3, bbd, cbd, a_bar, d_skip, w_glu)


def _dsa_kernel(qi_ref, ws_ref, ki_ref, q_ref, k_ref, v_ref, o_ref,
                key_ref, wb_ref, j_ref, m_ref, l_ref, acc_ref, *, tq, seq, ksel, rg):
    qb = pl.program_id(1)
    t0 = qb * tq
    nc = qb + 1
    cw = tq
    nt = (((1,), (1,)), ((), ()))

    def lane_cols(c, j):
        return c * cw + j * LANES + lax.broadcasted_iota(I32, (rg, LANES), 1)

    @pl.when(pl.program_id(2) == 0)
    def _select():
        qi = qi_ref[...]
        qs = jnp.concatenate([qi[:, h * IDX_DIM:(h + 1) * IDX_DIM] for h in range(N_IDX_HEADS)], axis=0)
        ws = ws_ref[...]
        for h in range(N_IDX_HEADS):
            wb_ref[h * tq:(h + 1) * tq, :] = jnp.broadcast_to(
                ws[:, IDX_DIM + h:IDX_DIM + h + 1], (tq, LANES))
        row = t0 + lax.broadcasted_iota(I32, (tq, cw), 0)

        def idx_body(c, _):
            start = pl.multiple_of(c * cw, cw)
            kib = ki_ref[pl.ds(start, cw), :][:, :IDX_DIM]
            d = lax.dot_general(qs, kib, nt, preferred_element_type=F32)
            r = jnp.maximum(d, 0.0) * jnp.tile(wb_ref[...], (1, cw // LANES))
            isc = r[0:tq]
            for h in range(1, N_IDX_HEADS):
                isc = isc + r[h * tq:(h + 1) * tq]
            col = start + lax.broadcasted_iota(I32, (tq, cw), 1)
            isc = jnp.where(isc == 0.0, 0.0, isc)
            isc = jnp.where(col <= row, isc, -jnp.inf)
            bits = pltpu.bitcast(isc, I32)
            key_ref[:, pl.ds(start, cw)] = bits ^ ((bits >> 31) & 0x7FFFFFFF)
            return 0

        lax.fori_loop(0, nc, idx_body, 0)

        for g in range(tq // rg):
            rows = pl.ds(g * rg, rg)

            def count(pred):
                def body(c, cnt):
                    start = pl.multiple_of(c * cw, cw)
                    blk = key_ref[rows, pl.ds(start, cw)]
                    for j in range(cw // LANES):
                        cnt = cnt + pred(blk[:, j * LANES:(j + 1) * LANES], c, j)
                    return cnt
                cnt = lax.fori_loop(0, nc, body, jnp.zeros((rg, LANES), I32))
                return jnp.sum(cnt, axis=1, keepdims=True)

            def count_ge(cand):
                cb = jnp.broadcast_to(cand, (rg, LANES))
                return count(lambda x, c, j: jnp.where(x >= cb, 1, 0))

            thr = jnp.where(count_ge(jnp.zeros((rg, 1), I32)) >= ksel, 0, INT_MIN).astype(I32)

            def bit_body(i, thr):
                cand = thr + jnp.left_shift(jnp.int32(1), 30 - i)
                return jnp.where(count_ge(cand) >= ksel, cand, thr)

            thr = lax.fori_loop(0, 31, bit_body, thr)
            n_ge = count_ge(thr)
            need = ksel - count_ge(thr + 1)
            j_ref[rows, :] = jnp.full((rg, LANES), seq, I32)

            @pl.when(jnp.max(n_ge) > ksel)
            def _ties():
                tb = jnp.broadcast_to(thr, (rg, LANES))

                def count_tie_below(cand):
                    cb = jnp.broadcast_to(cand, (rg, LANES))
                    return count(lambda x, c, j: jnp.where(
                        x == tb, jnp.where(lane_cols(c, j) < cb, 1, 0), 0))

                def jbit_body(i, jj):
                    cand = jj + jnp.left_shift(jnp.int32(1), (seq - 1).bit_length() - 1 - i)
                    return jnp.where(count_tie_below(cand) < need, cand, jj)

                jj = lax.fori_loop(0, (seq - 1).bit_length(), jbit_body, jnp.zeros((rg, 1), I32))
                j_ref[rows, :] = jnp.broadcast_to(jj, (rg, LANES))

            tb = jnp.broadcast_to(thr, (rg, LANES))
            jb = j_ref[rows, :]
            rowg = t0 + g * rg + lax.broadcasted_iota(I32, (rg, LANES), 0)

            def bias_body(c, _):
                start = pl.multiple_of(c * cw, cw)
                blk = key_ref[rows, pl.ds(start, cw)]
                outs = []
                for j in range(cw // LANES):
                    x = blk[:, j * LANES:(j + 1) * LANES]
                    col = lane_cols(c, j)
                    b = jnp.where(x > tb, 0.0, jnp.where(x == tb, jnp.where(col <= jb, 0.0, NEG_BIG), NEG_BIG))
                    outs.append(jnp.where(col <= rowg, b, NEG_BIG))
                key_ref[rows, pl.ds(start, cw)] = pltpu.bitcast(jnp.concatenate(outs, axis=1).astype(F32), I32)
                return 0

            lax.fori_loop(0, nc, bias_body, 0)

    q = q_ref[...]
    qs = jnp.concatenate([q[:, r * HEAD_DIM:(r + 1) * HEAD_DIM] for r in range(REP)], axis=0)
    m_ref[...] = jnp.full(m_ref.shape, NEG_BIG, F32)
    l_ref[...] = jnp.zeros(l_ref.shape, F32)
    acc_ref[...] = jnp.zeros(acc_ref.shape, F32)

    def att_body(c, _):
        start = pl.multiple_of(c * cw, cw)
        kb = k_ref[0, 0, pl.ds(start, cw), :]
        vb = v_ref[0, 0, pl.ds(start, cw), :]
        s = lax.dot_general(qs, kb, nt, preferred_element_type=F32)
        bias = pltpu.bitcast(key_ref[:, pl.ds(start, cw)], F32)
        s = s + jnp.tile(bias, (REP, 1))
        m_prev = m_ref[...]
        m_new = jnp.maximum(m_prev, jnp.max(s, axis=1, keepdims=True))
        alpha = jnp.exp(m_prev - m_new)
        p = jnp.exp(s - jnp.tile(m_new, (1, cw // LANES)))
        l_ref[...] = alpha * l_ref[...] + jnp.sum(p, axis=1, keepdims=True)
        acc_ref[...] = alpha[:, :HEAD_DIM] * acc_ref[...] + jnp.dot(
            p.astype(BF16), vb, preferred_element_type=F32)
        m_ref[...] = m_new
        return 0

    lax.fori_loop(0, nc, att_body, 0)
    out = acc_ref[...] / l_ref[...][:, :HEAD_DIM]
    o_ref[...] = jnp.concatenate([out[r * tq:(r + 1) * tq] for r in range(REP)], axis=1).astype(o_ref.dtype)


def _dsa(qi_r, w_s, ki_r, q_r, k_hm, v_hm, bsz, seq):
    t = q_r.shape[0]
    tq = 256
    rg = 128
    assert seq % tq == 0
    ksel = min(TOPK_MAX, seq // 4)
    assert ksel <= tq
    nqb = seq // tq
    rowblk = lambda w: pl.BlockSpec((tq, w), lambda b, i, g: (b * nqb + i, 0))
    kv = pl.BlockSpec((1, 1, seq, HEAD_DIM), lambda b, i, g: (b, g, 0, 0))
    return pl.pallas_call(
        functools.partial(_dsa_kernel, tq=tq, seq=seq, ksel=ksel, rg=rg),
        grid=(bsz, nqb, N_KV_HEADS),
        in_specs=[rowblk(N_IDX_HEADS * IDX_DIM), rowblk(LANES),
                  pl.BlockSpec((seq, LANES), lambda b, i, g: (b, 0)),
                  pl.BlockSpec((tq, REP * HEAD_DIM), lambda b, i, g: (b * nqb + i, g)),
                  kv, kv],
        out_specs=pl.BlockSpec((tq, REP * HEAD_DIM), lambda b, i, g: (b * nqb + i, g)),
        out_shape=jax.ShapeDtypeStruct((t, ATT_W), BF16),
        scratch_shapes=[pltpu.VMEM((tq, seq), I32),
                        pltpu.VMEM((N_IDX_HEADS * tq, LANES), F32),
                        pltpu.VMEM((tq, LANES), I32),
                        pltpu.VMEM((REP * tq, LANES), F32),
                        pltpu.VMEM((REP * tq, LANES), F32),
                        pltpu.VMEM((REP * tq, HEAD_DIM), F32)],
        compiler_params=_cparams(("parallel", "arbitrary", "arbitrary")),
        name="dsa",
    )(qi_r, w_s, ki_r, q_r, k_hm, v_hm)


def _causal_conv3(cur, halo, w):
    tm = cur.shape[0]
    ext = jnp.concatenate([halo, cur], axis=0)
    return (w[2:3, :] * cur + w[1:2, :] * ext[SUBLANES - 1:SUBLANES - 1 + tm]
            + w[0:1, :] * ext[SUBLANES - 2:SUBLANES - 2 + tm])


def _merge_kernel(x_ref, ga_ref, gb_ref, gc_ref, scx_ref, scb_ref, scc_ref, hx_ref, hc_ref,
                  yb_ref, yc_ref, cw_ref, wa_ref, wb_ref, wc_ref, wo_ref, o_ref, *, tm, seq):
    first = (pl.program_id(0) * tm) % seq == 0
    cx = scc_ref[...].astype(F32) * scx_ref[...].astype(F32)
    halo = hc_ref[...].astype(F32) * hx_ref[...].astype(F32)
    halo = jnp.where(first, 0.0, halo)
    ya = scb_ref[...].astype(F32) * _causal_conv3(cx, halo, cw_ref[...])
    dot = lambda a, w: jnp.dot(a, w[...], preferred_element_type=F32)
    m = jax.nn.sigmoid(ga_ref[...].astype(F32)) * dot(ya.astype(BF16), wa_ref)
    m = m + jax.nn.sigmoid(gb_ref[...].astype(F32)) * dot(yb_ref[...], wb_ref)
    m = m + jax.nn.sigmoid(gc_ref[...].astype(F32)) * dot(yc_ref[...], wc_ref)
    o_ref[...] = x_ref[...] + dot(m.astype(BF16), wo_ref)


def _merge(x2, p2, y_b, y_c, conv_w, w_a, w_b, w_c, w_o, seq):
    t, d = x2.shape
    tm = _pick(seq, (256, 128))
    hb = tm // SUBLANES

    def col(width, off):
        return pl.BlockSpec((tm, width), lambda i, o=off // width: (i, o))

    def halo(off):
        return pl.BlockSpec((SUBLANES, SC_W), lambda i, o=off // SC_W: (jnp.maximum(i * hb - 1, 0), o))

    def const(shape):
        return pl.BlockSpec(shape, lambda i: (0, 0), pipeline_mode=pl.Buffered(1))

    row = lambda w: pl.BlockSpec((tm, w), lambda i: (i, 0))
    return pl.pallas_call(
        functools.partial(_merge_kernel, tm=tm, seq=seq),
        grid=(t // tm,),
        in_specs=[row(d), col(d, 0), col(d, d), col(d, 2 * d),
                  col(SC_W, OFF_SCX), col(SC_W, OFF_SCB), col(SC_W, OFF_SCC),
                  halo(OFF_SCX), halo(OFF_SCC), row(SSM_W), row(ATT_W),
                  const((SC_CONV, SC_W)), const((SC_W, d)), const((SSM_W, d)),
                  const((ATT_W, d)), const((d, d))],
        out_specs=row(d),
        out_shape=jax.ShapeDtypeStruct((t, d), F32),
        compiler_params=_cparams(("parallel",)),
        name="merge",
    )(x2, p2, p2, p2, p2, p2, p2, p2, p2, y_b, y_c, conv_w, w_a, w_b, w_c, w_o)


def _ffn_kernel(x_ref, hx_ref, g_ref, wg_ref, wu_ref, cg_ref, cu_ref, wd_ref, o_ref,
                h_ref, acc_ref, *, tm, seq):
    j = pl.program_id(1)

    @pl.when(j == 0)
    def _():
        first = (pl.program_id(0) * tm) % seq == 0
        x = jnp.concatenate([hx_ref[...], x_ref[...]], axis=0)
        ms = jnp.mean(x * x, axis=-1, keepdims=True)
        h = x * lax.rsqrt(ms + NORM_EPS) * g_ref[...]
        rows = lax.broadcasted_iota(I32, (tm + SUBLANES, 1), 0)
        h = jnp.where(jnp.logical_and(first, rows < SUBLANES), 0.0, h)
        h_ref[...] = h.astype(BF16)
        acc_ref[...] = jnp.zeros_like(acc_ref)

    h = h_ref[...]
    gt = jnp.dot(h, wg_ref[...], preferred_element_type=F32)
    ut = jnp.dot(h, wu_ref[...], preferred_element_type=F32)
    gc = _causal_conv3(gt[SUBLANES:], gt[:SUBLANES], cg_ref[...])
    uc = _causal_conv3(ut[SUBLANES:], ut[:SUBLANES], cu_ref[...])
    act = (jax.nn.silu(gc) * uc).astype(BF16)
    acc_ref[...] += jnp.dot(act, wd_ref[...], preferred_element_type=F32)

    @pl.when(j == pl.num_programs(1) - 1)
    def _():
        o_ref[...] = x_ref[...] + acc_ref[...]


def _ffn(x2, gain, w_up, conv_w, w_down, seq):
    t, d = x2.shape
    dff = w_down.shape[0]
    tm = _pick(seq, (512, 256, 128))
    tf = _pick(dff, (512, 256, 128))
    nf = dff // tf
    hb = tm // SUBLANES
    return pl.pallas_call(
        functools.partial(_ffn_kernel, tm=tm, seq=seq),
        grid=(t // tm, nf),
        in_specs=[pl.BlockSpec((tm, d), lambda i, j: (i, 0)),
                  pl.BlockSpec((SUBLANES, d), lambda i, j: (jnp.maximum(i * hb - 1, 0), 0)),
                  pl.BlockSpec((1, d), lambda i, j: (0, 0)),
                  pl.BlockSpec((d, tf), lambda i, j: (0, j)),
                  pl.BlockSpec((d, tf), lambda i, j: (0, j + nf)),
                  pl.BlockSpec((FFN_CONV, tf), lambda i, j: (0, j)),
                  pl.BlockSpec((FFN_CONV, tf), lambda i, j: (0, j + nf)),
                  pl.BlockSpec((tf, d), lambda i, j: (j, 0))],
        out_specs=pl.BlockSpec((tm, d), lambda i, j: (i, 0)),
        out_shape=jax.ShapeDtypeStruct((t, d), F32),
        scratch_shapes=[pltpu.VMEM((tm + SUBLANES, d), BF16), pltpu.VMEM((tm, d), F32)],
        compiler_params=_cparams(("parallel", "arbitrary")),
        name="ffn",
    )(x2, x2, gain, w_up, w_up, conv_w, conv_w, w_down)


def _norm_kernel(x_ref, g_ref, o_ref):
    x = x_ref[...]
    ms = jnp.mean(x * x, axis=-1, keepdims=True)
    o_ref[...] = x * lax.rsqrt(ms + NORM_EPS) * g_ref[...]


def _final_norm(x2, gain):
    t, d = x2.shape
    tm = _pick(t, (512, 256, 128))
    return pl.pallas_call(
        _norm_kernel,
        grid=(t // tm,),
        in_specs=[pl.BlockSpec((tm, d), lambda i: (i, 0)), pl.BlockSpec((1, d), lambda i: (0, 0))],
        out_specs=pl.BlockSpec((tm, d), lambda i: (i, 0)),
        out_shape=jax.ShapeDtypeStruct((t, d), F32),
        compiler_params=_cparams(("parallel",)),
        name="final_norm",
    )(x2, gain)


def _prep_w_in(w):
    d = w.shape[0]
    pad = jnp.zeros((d, NP_COLS - w.shape[1]), w.dtype)
    return jnp.concatenate([w[:, _ORIG_GATE_OFF:], w[:, :_ORIG_GATE_OFF], pad], axis=1).astype(BF16)


def _prep_ssm(a_re, a_im, log_dt, b_re, b_im, c_re, c_im):
    a = lax.complex(a_re.astype(F32), a_im.astype(F32))
    dt = jnp.exp(log_dt.astype(F32))[:, None]
    a_bar = jnp.exp(dt * a)
    b_bar = ((a_bar - 1.0) / a)[..., None] * lax.complex(b_re.astype(F32), b_im.astype(F32))
    eye = jnp.eye(SSM_GROUPS, dtype=F32)

    def bdiag_in(m):
        return (jnp.transpose(m, (0, 2, 1))[:, :, None, :] * eye[:, None, :, None]).reshape(
            SSM_W, SSM_NS)

    def bdiag_out(m):
        return (jnp.transpose(m, (0, 2, 1))[:, :, None, :] * eye[:, None, :, None]).reshape(
            SSM_NS, SSM_W)

    bbd = jnp.concatenate([bdiag_in(jnp.real(b_bar)), bdiag_in(jnp.imag(b_bar))], axis=1).astype(BF16)
    cbd = jnp.concatenate([bdiag_out(c_re.astype(F32)), bdiag_out(-c_im.astype(F32))], axis=0).astype(BF16)
    a_rows = jnp.stack([jnp.real(a_bar).reshape(-1), jnp.imag(a_bar).reshape(-1)], axis=0)
    return bbd, cbd, a_rows


def kernel(x, positions, norm_mix, w_in, sc_conv, ssm_a_re, ssm_a_im, ssm_log_dt, ssm_b_re, ssm_b_im,
           ssm_c_re, ssm_c_im, ssm_d, ssm_glu, w_branch_a, w_branch_b, w_branch_c, w_out, norm_ffn,
           w_up, ffn_conv, w_down, norm_final):
    bsz, seq, d = x.shape
    depth = w_in.shape[0]
    t = bsz * seq
    x2 = x.reshape(t, d).astype(F32)
    tabs = _rope_tables(positions.astype(I32))
    for l in range(depth):
        p2 = _inproj(x2, norm_mix[l].reshape(1, d).astype(F32), _prep_w_in(w_in[l]))
        q_r, k_hm, v_hm, qi_r, ki_r, w_s = _rope(p2, tabs, bsz, seq)
        bbd, cbd, a_rows = _prep_ssm(ssm_a_re[l], ssm_a_im[l], ssm_log_dt[l], ssm_b_re[l], ssm_b_im[l],
                                     ssm_c_re[l], ssm_c_im[l])
        y_b = _ssm(p2.reshape(bsz, seq, NP_COLS), bbd, cbd, a_rows,
                   ssm_d[l].reshape(1, SSM_W).astype(F32), ssm_glu[l].astype(BF16))
        y_c = _dsa(qi_r, w_s, ki_r, q_r, k_hm, v_hm, bsz, seq)
        x2 = _merge(x2, p2, y_b.reshape(t, SSM_W), y_c, sc_conv[l].astype(F32),
                    w_branch_a[l].astype(BF16), w_branch_b[l].astype(BF16), w_branch_c[l].astype(BF16),
                    w_out[l].astype(BF16), seq)
        x2 = _ffn(x2, norm_ffn[l].reshape(1, d).astype(F32), w_up[l].astype(BF16),
                  ffn_conv[l].astype(F32), w_down[l].astype(BF16), seq)
    out = _final_norm(x2, norm_final.reshape(1, d).astype(F32))
    return out.reshape(bsz, seq, d).astype(x.dtype)
```

```python
import functools
import math

import jax
import jax.numpy as jnp
from jax import lax
from jax.experimental import pallas as pl
from jax.experimental.pallas import tpu as pltpu

F32 = jnp.float32
BF16 = jnp.bfloat16
I32 = jnp.int32

SC_W = 512
SC_CONV = 3
SSM_W = 512
SSM_GROUP = 16
SSM_GROUPS = SSM_W // SSM_GROUP
SSM_STATE = 64
SSM_NS = SSM_GROUPS * SSM_STATE
N_Q_HEADS = 16
N_KV_HEADS = 4
HEAD_DIM = 64
ATT_W = N_Q_HEADS * HEAD_DIM
KV_W = N_KV_HEADS * HEAD_DIM
REP = N_Q_HEADS // N_KV_HEADS
ROT_DIM = HEAD_DIM // 4
ROT_HALF = ROT_DIM // 2
ROPE_THETA = 500000.0
N_IDX_HEADS = 8
IDX_DIM = 64
TOPK_MAX = 256
D_FF = 5632
FFN_CONV = 3
NORM_EPS = 1e-6

LANES = 128
SUBLANES = 8
ONES_ROWS = 16
VMEM_LIMIT = 56 * 1024 * 1024
NEG_BIG = -1e30
LOG2E = math.log2(math.e)
INT_MIN = -(2 ** 31)

_ORIG_GATE_OFF = 3 * SC_W + SSM_W + ATT_W + 2 * KV_W + N_IDX_HEADS * IDX_DIM + IDX_DIM + N_IDX_HEADS
OFF_SCX = 3 * 2048
OFF_SCB = OFF_SCX + SC_W
OFF_SCC = OFF_SCB + SC_W
OFF_SSM = OFF_SCC + SC_W
OFF_Q = OFF_SSM + SSM_W
OFF_K = OFF_Q + ATT_W
OFF_V = OFF_K + KV_W
OFF_QI = OFF_V + KV_W
OFF_KI = OFF_QI + N_IDX_HEADS * IDX_DIM
NP_COLS = OFF_KI + LANES


def _cparams(sem, vmem=VMEM_LIMIT):
    return pltpu.CompilerParams(dimension_semantics=sem, vmem_limit_bytes=vmem)


def _pick(n, prefs):
    for p in prefs:
        if n % p == 0:
            return p
    return n


def _inproj_kernel(x_ref, g_ref, w_ref, o_ref, h_ref):
    @pl.when(pl.program_id(1) == 0)
    def _():
        x = x_ref[...]
        ms = jnp.mean(x * x, axis=-1, keepdims=True)
        h_ref[...] = (x * lax.rsqrt(ms + NORM_EPS) * g_ref[...]).astype(BF16)

    o_ref[...] = jnp.dot(h_ref[...], w_ref[...], preferred_element_type=F32).astype(o_ref.dtype)


def _inproj(x2, gain, w):
    t, d = x2.shape
    n = w.shape[1]
    tm = _pick(t, (1024, 512, 256, 128))
    tn = _pick(n, (1152, 384, 128))
    return pl.pallas_call(
        _inproj_kernel,
        grid=(t // tm, n // tn),
        in_specs=[pl.BlockSpec((tm, d), lambda i, j: (i, 0)),
                  pl.BlockSpec((1, d), lambda i, j: (0, 0)),
                  pl.BlockSpec((d, tn), lambda i, j: (0, j))],
        out_specs=pl.BlockSpec((tm, tn), lambda i, j: (i, j)),
        out_shape=jax.ShapeDtypeStruct((t, n), BF16),
        scratch_shapes=[pltpu.VMEM((tm, d), BF16)],
        compiler_params=_cparams(("parallel", "arbitrary")),
        name="inproj",
    )(x2, gain, w)


def _rope_table_kernel(pos_ref, invf_ref, a_ref, bm_ref, bp_ref):
    ang = pos_ref[...].astype(F32) * invf_ref[...]
    c = jnp.cos(ang)
    s = jnp.sin(ang)
    r = lax.broadcasted_iota(I32, ang.shape, 1) % HEAD_DIM
    lo = r < ROT_HALF
    hi = jnp.logical_and(r >= ROT_HALF, r < ROT_DIM)
    a_ref[...] = jnp.where(r < ROT_DIM, c, 1.0)
    bm_ref[...] = jnp.where(lo, -s, 0.0)
    bp_ref[...] = jnp.where(hi, s, 0.0)


def _rope_tables(positions):
    t = positions.size
    posb = jnp.broadcast_to(positions.reshape(t, 1), (t, LANES))
    inv_freq = ROPE_THETA ** (-jnp.arange(0, ROT_DIM, 2, dtype=F32) / ROT_DIM)
    lane = jnp.arange(LANES) % ROT_HALF
    invf = inv_freq[lane].reshape(1, LANES)
    tm = _pick(t, (1024, 512, 256, 128))
    spec = pl.BlockSpec((tm, LANES), lambda i: (i, 0))
    return pl.pallas_call(
        _rope_table_kernel,
        grid=(t // tm,),
        in_specs=[spec, pl.BlockSpec((1, LANES), lambda i: (0, 0))],
        out_specs=[spec, spec, spec],
        out_shape=[jax.ShapeDtypeStruct((t, LANES), F32)] * 3,
        compiler_params=_cparams(("parallel",)),
        name="rope_tables",
    )(posb, invf)


def _rope_kernel(q_ref, k_ref, v_ref, qi_ref, kiw_ref, a_ref, bm_ref, bp_ref,
                 qo_ref, ko_ref, vto_ref, qio_ref, kio_ref, wo_ref):
    a = a_ref[...]
    bm = bm_ref[...]
    bp = bp_ref[...]

    def rope(x):
        w = x.shape[1]
        reps = w // LANES
        xf = x.astype(F32)
        up = pltpu.roll(xf, w - ROT_HALF, 1)
        dn = pltpu.roll(xf, ROT_HALF, 1)
        return (xf * jnp.tile(a, (1, reps)) + up * jnp.tile(bm, (1, reps))
                + dn * jnp.tile(bp, (1, reps)))

    qo_ref[...] = (rope(q_ref[...]) * (HEAD_DIM ** -0.5 * LOG2E)).astype(BF16)
    ko_ref[...] = rope(k_ref[...]).astype(BF16)
    vto_ref[0] = v_ref[...].astype(F32).T.astype(BF16)
    qio_ref[...] = (rope(qi_ref[...]) * (IDX_DIM ** -0.5)).astype(BF16)
    kiw = kiw_ref[...]
    kio_ref[...] = rope(kiw).astype(BF16)
    wo_ref[...] = kiw.astype(F32) * (N_IDX_HEADS ** -0.5)


def _rope(p2, tabs, bsz, seq):
    t = p2.shape[0]
    tm = _pick(seq, (512, 256, 128))
    nsb = seq // tm
    a, bm, bp = tabs

    def col(width, off):
        return pl.BlockSpec((tm, width), lambda i, o=off // width: (i, o))

    tab = pl.BlockSpec((tm, LANES), lambda i: (i, 0))
    row = lambda w: pl.BlockSpec((tm, w), lambda i: (i, 0))
    return pl.pallas_call(
        _rope_kernel,
        grid=(t // tm,),
        in_specs=[col(ATT_W, OFF_Q), col(KV_W, OFF_K), col(KV_W, OFF_V),
                  col(N_IDX_HEADS * IDX_DIM, OFF_QI), col(LANES, OFF_KI), tab, tab, tab],
        out_specs=[row(ATT_W), row(KV_W),
                   pl.BlockSpec((1, KV_W, tm), lambda i: (i // nsb, 0, i % nsb)),
                   row(N_IDX_HEADS * IDX_DIM), row(LANES), row(LANES)],
        out_shape=[jax.ShapeDtypeStruct((t, ATT_W), BF16),
                   jax.ShapeDtypeStruct((t, KV_W), BF16),
                   jax.ShapeDtypeStruct((bsz, KV_W, seq), BF16),
                   jax.ShapeDtypeStruct((t, N_IDX_HEADS * IDX_DIM), BF16),
                   jax.ShapeDtypeStruct((t, LANES), BF16),
                   jax.ShapeDtypeStruct((t, LANES), F32)],
        compiler_params=_cparams(("parallel",)),
        name="rope",
    )(p2, p2, p2, p2, p2, a, bm, bp)


def _ssm_kernel(u_ref, bbd_ref, cbd_ref, a_ref, d_ref, wg_ref, o_ref, bu_ref, st_ref, *, chunk):
    @pl.when(pl.program_id(1) == 0)
    def _():
        st_ref[...] = jnp.zeros_like(st_ref)

    u = u_ref[0]
    bu_ref[...] = jnp.dot(u, bbd_ref[...], preferred_element_type=F32)
    ar = a_ref[0:1, :]
    ai = a_ref[1:2, :]

    def step(t, carry):
        xr, xi = carry
        br = bu_ref[pl.ds(t, 1), 0:SSM_NS]
        bi = bu_ref[pl.ds(t, 1), SSM_NS:2 * SSM_NS]
        nr = ar * xr - ai * xi + br
        ni = ar * xi + ai * xr + bi
        bu_ref[pl.ds(t, 1), 0:SSM_NS] = nr
        bu_ref[pl.ds(t, 1), SSM_NS:2 * SSM_NS] = ni
        return nr, ni

    xr, xi = lax.fori_loop(0, chunk, step, (st_ref[0:1, :], st_ref[1:2, :]), unroll=8)
    st_ref[0:1, :] = xr
    st_ref[1:2, :] = xi

    y = jnp.dot(bu_ref[...].astype(BF16), cbd_ref[...], preferred_element_type=F32)
    y = y + d_ref[...] * u.astype(F32)
    z = jax.nn.gelu(y)
    gate = jnp.dot(z.astype(BF16), wg_ref[...], preferred_element_type=F32)
    o_ref[0] = (z * jax.nn.sigmoid(gate)).astype(o_ref.dtype)


def _ssm(p3, bbd, cbd, a_bar, d_skip, w_glu):
    bsz, seq, _ = p3.shape
    chunk = _pick(seq, (256, 128))
    const = lambda shape: pl.BlockSpec(shape, lambda b, c: (0,) * len(shape))
    return pl.pallas_call(
        functools.partial(_ssm_kernel, chunk=chunk),
        grid=(bsz, seq // chunk),
        in_specs=[pl.BlockSpec((1, chunk, SSM_W), lambda b, c: (b, c, OFF_SSM // SSM_W)),
                  const((SSM_W, 2 * SSM_NS)), const((2 * SSM_NS, SSM_W)),
                  const((2, SSM_NS)), const((1, SSM_W)), const((SSM_W, SSM_W))],
        out_specs=pl.BlockSpec((1, chunk, SSM_W), lambda b, c: (b, c, 0)),
        out_shape=jax.ShapeDtypeStruct((bsz, seq, SSM_W), BF16),
        scratch_shapes=[pltpu.VMEM((chunk, 2 * SSM_NS), F32), pltpu.VMEM((2, SSM_NS), F32)],
        compiler_params=_cparams(("parallel", "arbitrary")),
        name="ssm",
    )(p3, bbd, cbd, a_bar, d_skip, w_glu)


def _dsa_kernel(qi_ref, ws_ref, ki_ref, q_ref, k_ref, vt_ref, o_ref,
                key_ref, qit_ref, d0_ref, d1_ref, qt_ref, s0_ref, s1_ref, p_ref, acc_ref,
                *, tq, seq, ksel, fixed_bits):
    qb = pl.program_id(1)
    gid = pl.program_id(2)
    t0 = qb * tq
    cw = tq
    nc = qb + 1
    ca = 2 * tq
    nca = (qb + 2) // 2
    ks = 128
    kf = float(ksel)
    jbits = (seq - 1).bit_length()
    qidx = t0 + lax.broadcasted_iota(I32, (1, tq), 1)

    @pl.when(gid == 0)
    def _select():
        qit = qi_ref[...].astype(F32).T
        qit_ref[...] = jnp.concatenate(
            [qit[h * IDX_DIM:(h + 1) * IDX_DIM, :] for h in range(N_IDX_HEADS)], axis=1).astype(BF16)
        wt = ws_ref[...].T
        wrow = jnp.concatenate([wt[IDX_DIM + h:IDX_DIM + h + 1, :] for h in range(N_IDX_HEADS)], axis=1)

        def idx_dots(c, d_ref):
            start = pl.multiple_of(c * cw, cw)
            kib = ki_ref[pl.ds(start, cw), :][:, :IDX_DIM]
            d_ref[...] = jnp.dot(kib, qit_ref[...], preferred_element_type=F32)

        def idx_scores(c, d_ref):
            for i in range(cw // ks):
                start = pl.multiple_of(c * cw, cw) + i * ks
                r = jnp.maximum(d_ref[i * ks:(i + 1) * ks, :], 0.0) * wrow
                isc = r[:, 0:tq]
                for h in range(1, N_IDX_HEADS):
                    isc = isc + r[:, h * tq:(h + 1) * tq]
                kidx = start + lax.broadcasted_iota(I32, (ks, tq), 0)
                isc = jnp.where(isc == 0.0, 0.0, isc)
                isc = jnp.where(kidx <= qidx, isc, -jnp.inf)
                bits = pltpu.bitcast(isc, I32)
                key_ref[pl.ds(start, ks), :] = bits ^ ((bits >> 31) & 0x7FFFFFFF)

        idx_dots(0, d0_ref)

        def idx_pair(c2, _):
            c = 2 * c2
            idx_dots(c + 1, d1_ref)
            idx_scores(c, d0_ref)
            idx_dots(jnp.minimum(c + 2, nc - 1), d0_ref)
            idx_scores(c + 1, d1_ref)
            return 0

        lax.fori_loop(0, nc // 2, idx_pair, 0)

        @pl.when(nc % 2 == 1)
        def _():
            idx_scores(nc - 1, d0_ref)

        @pl.when(qb % 2 == 0)
        def _():
            key_ref[pl.ds(pl.multiple_of((qb + 1) * cw, cw), cw), :] = pltpu.bitcast(
                jnp.full((cw, tq), NEG_BIG, F32), I32)

        small = qidx < ksel

        def key_rows(c):
            return pl.multiple_of(c * cw, cw) + lax.broadcasted_iota(I32, (cw, tq), 0)

        def count(pred):
            def body(c, cnt):
                blk = key_ref[pl.ds(pl.multiple_of(c * cw, cw), cw), :]
                hit = pred(blk, c)
                return cnt + jnp.sum(hit.reshape(cw // 32, 32, tq), axis=0)
            cnt = lax.fori_loop(0, nc, body, jnp.zeros((32, tq), F32))
            return jnp.sum(cnt, axis=0, keepdims=True)

        def count_ge(cand):
            return count(lambda x, c: jnp.where(x >= cand, 1.0, 0.0))

        c0 = count_ge(jnp.zeros((1, tq), I32))
        thr = jnp.where(c0 >= kf, 0, INT_MIN).astype(I32)
        done = jnp.where(jnp.logical_or(small, c0 == kf), 1.0, 0.0)

        def bit_step(i, thr, done):
            cand = thr + jnp.left_shift(jnp.int32(1), 30 - i)
            cnt = count_ge(cand)
            return (jnp.where(cnt >= kf, cand, thr),
                    jnp.maximum(done, jnp.where(cnt == kf, 1.0, 0.0)))

        thr, done = lax.fori_loop(0, fixed_bits, lambda i, c: bit_step(i, *c), (thr, done))

        def w_cond(c):
            return jnp.logical_and(c[0] < 31, c[3] < 0.5)

        def w_body(c):
            thr, done = bit_step(c[0], c[1], c[2])
            return c[0] + 1, thr, done, jnp.min(done)

        _, thr, done, settled = lax.while_loop(
            w_cond, w_body, (jnp.int32(fixed_bits), thr, done, jnp.min(done)))
        thr = jnp.where(small, INT_MIN, thr)

        def tie_index():
            n_ge = count_ge(thr)
            need = kf - count_ge(thr + 1)
            tie = jnp.logical_and(jnp.logical_not(small), n_ge > kf)

            def count_tie_below(cand):
                return count(lambda x, c: jnp.where(
                    x == thr, jnp.where(key_rows(c) < cand, 1.0, 0.0), 0.0))

            def jbit_body(i, jj):
                cand = jj + jnp.left_shift(jnp.int32(1), jbits - 1 - i)
                return jnp.where(count_tie_below(cand) < need, cand, jj)

            jj = lax.fori_loop(0, jbits, jbit_body, jnp.zeros((1, tq), I32))
            return jnp.where(tie, jj, seq)

        jmax = lax.cond(settled < 0.5, tie_index, lambda: jnp.full((1, tq), seq, I32))

        def bias_body(c, _):
            rows = pl.ds(pl.multiple_of(c * cw, cw), cw)
            x = key_ref[rows, :]
            kidx = key_rows(c)
            b = jnp.where(x > thr, 0.0, jnp.where(x == thr, jnp.where(kidx <= jmax, 0.0, NEG_BIG), NEG_BIG))
            b = jnp.where(kidx <= qidx, b, NEG_BIG)
            key_ref[rows, :] = pltpu.bitcast(b.astype(F32), I32)
            return 0

        lax.fori_loop(0, nc, bias_body, 0)

    qt = q_ref[...].astype(F32).T
    vrow = pl.multiple_of(gid * HEAD_DIM, HEAD_DIM)
    qt_ref[...] = jnp.zeros(qt_ref.shape, BF16)
    for r in range(REP):
        qt_ref[pl.ds(vrow, HEAD_DIM), r * tq:(r + 1) * tq] = qt[r * HEAD_DIM:(r + 1) * HEAD_DIM, :].astype(BF16)
    acc_ref[...] = jnp.zeros(acc_ref.shape, F32)
    ones_rows = jnp.ones((ONES_ROWS, ca), BF16)

    def scores(c, s_ref):
        start = pl.multiple_of(c * ca, ca)
        s_ref[...] = jnp.dot(k_ref[pl.ds(start, ca), :], qt_ref[...], preferred_element_type=F32)

    def softmax_chunk(c, s_ref, m_prev):
        def biased(i):
            rows = pl.ds(pl.multiple_of(c * ca, ca) + i * ks, ks)
            return s_ref[i * ks:(i + 1) * ks, :] + jnp.tile(pltpu.bitcast(key_ref[rows, :], F32), (1, REP))

        m_new = m_prev
        for i in range(ca // ks):
            m_new = jnp.maximum(m_new, jnp.max(biased(i), axis=0, keepdims=True))
        for i in range(ca // ks):
            p_ref[i * ks:(i + 1) * ks, :] = jnp.exp2(biased(i) - m_new).astype(BF16)
        vt = jnp.concatenate(
            [vt_ref[0, pl.ds(vrow, HEAD_DIM), pl.ds(pl.multiple_of(c * ca, ca), ca)], ones_rows], axis=0)
        acc_ref[...] = jnp.exp2(m_prev - m_new) * acc_ref[...] + jnp.dot(
            vt, p_ref[...], preferred_element_type=F32)
        return m_new

    scores(0, s0_ref)

    def att_pair(c2, m):
        c = 2 * c2
        scores(c + 1, s1_ref)
        m = softmax_chunk(c, s0_ref, m)
        scores(jnp.minimum(c + 2, nca - 1), s0_ref)
        return softmax_chunk(c + 1, s1_ref, m)

    m_run = lax.fori_loop(0, nca // 2, att_pair, jnp.full((1, REP * tq), NEG_BIG, F32))

    @pl.when(nca % 2 == 1)
    def _():
        softmax_chunk(nca - 1, s0_ref, m_run)

    acc = acc_ref[...]
    out_t = acc[:HEAD_DIM] / acc[HEAD_DIM:HEAD_DIM + 1]
    out_t = jnp.concatenate([out_t[:, r * tq:(r + 1) * tq] for r in range(REP)], axis=0)
    o_ref[...] = out_t.T.astype(o_ref.dtype)


def _dsa(qi_r, w_s, ki_r, q_r, k_r, v_t, bsz, seq):
    t = q_r.shape[0]
    tq = 256
    assert seq % (2 * tq) == 0
    ksel = min(TOPK_MAX, seq // 4)
    assert ksel <= tq
    nqb = seq // tq
    rowblk = lambda w: pl.BlockSpec((tq, w), lambda b, i, g: (b * nqb + i, 0))
    return pl.pallas_call(
        functools.partial(_dsa_kernel, tq=tq, seq=seq, ksel=ksel, fixed_bits=14),
        grid=(bsz, nqb, N_KV_HEADS),
        in_specs=[rowblk(N_IDX_HEADS * IDX_DIM), rowblk(LANES),
                  pl.BlockSpec((seq, LANES), lambda b, i, g: (b, 0)),
                  pl.BlockSpec((tq, REP * HEAD_DIM), lambda b, i, g: (b * nqb + i, g)),
                  pl.BlockSpec((seq, KV_W), lambda b, i, g: (b, 0)),
                  pl.BlockSpec((1, KV_W, seq), lambda b, i, g: (b, 0, 0))],
        out_specs=pl.BlockSpec((tq, REP * HEAD_DIM), lambda b, i, g: (b * nqb + i, g)),
        out_shape=jax.ShapeDtypeStruct((t, ATT_W), BF16),
        scratch_shapes=[pltpu.VMEM((seq, tq), I32),
                        pltpu.VMEM((IDX_DIM, N_IDX_HEADS * tq), BF16),
                        pltpu.VMEM((tq, N_IDX_HEADS * tq), F32),
                        pltpu.VMEM((tq, N_IDX_HEADS * tq), F32),
                        pltpu.VMEM((KV_W, REP * tq), BF16),
                        pltpu.VMEM((2 * tq, REP * tq), F32),
                        pltpu.VMEM((2 * tq, REP * tq), F32),
                        pltpu.VMEM((2 * tq, REP * tq), BF16),
                        pltpu.VMEM((HEAD_DIM + ONES_ROWS, REP * tq), F32)],
        compiler_params=_cparams(("parallel", "arbitrary", "arbitrary")),
        name="dsa",
    )(qi_r, w_s, ki_r, q_r, k_r, v_t)


def _causal_conv3(cur, halo, w):
    tm = cur.shape[0]
    ext = jnp.concatenate([halo, cur], axis=0)
    return (w[2:3, :] * cur + w[1:2, :] * ext[SUBLANES - 1:SUBLANES - 1 + tm]
            + w[0:1, :] * ext[SUBLANES - 2:SUBLANES - 2 + tm])


def _merge_kernel(x_ref, ga_ref, gb_ref, gc_ref, scx_ref, scb_ref, scc_ref, hx_ref, hc_ref,
                  yb_ref, yc_ref, cw_ref, wa_ref, wb_ref, wc_ref, wo_ref, o_ref, *, tm, seq):
    first = (pl.program_id(0) * tm) % seq == 0
    cx = scc_ref[...].astype(F32) * scx_ref[...].astype(F32)
    halo = hc_ref[...].astype(F32) * hx_ref[...].astype(F32)
    halo = jnp.where(first, 0.0, halo)
    ya = scb_ref[...].astype(F32) * _causal_conv3(cx, halo, cw_ref[...])
    dot = lambda a, w: jnp.dot(a, w[...], preferred_element_type=F32)
    m = jax.nn.sigmoid(ga_ref[...].astype(F32)) * dot(ya.astype(BF16), wa_ref)
    m = m + jax.nn.sigmoid(gb_ref[...].astype(F32)) * dot(yb_ref[...], wb_ref)
    m = m + jax.nn.sigmoid(gc_ref[...].astype(F32)) * dot(yc_ref[...], wc_ref)
    o_ref[...] = x_ref[...] + dot(m.astype(BF16), wo_ref)


def _merge(x2, p2, y_b, y_c, conv_w, w_a, w_b, w_c, w_o, seq):
    t, d = x2.shape
    tm = _pick(seq, (256, 128))
    hb = tm // SUBLANES

    def col(width, off):
        return pl.BlockSpec((tm, width), lambda i, o=off // width: (i, o))

    def halo(off):
        return pl.BlockSpec((SUBLANES, SC_W), lambda i, o=off // SC_W: (jnp.maximum(i * hb - 1, 0), o))

    def const(shape):
        return pl.BlockSpec(shape, lambda i: (0, 0), pipeline_mode=pl.Buffered(1))

    row = lambda w: pl.BlockSpec((tm, w), lambda i: (i, 0))
    return pl.pallas_call(
        functools.partial(_merge_kernel, tm=tm, seq=seq),
        grid=(t // tm,),
        in_specs=[row(d), col(d, 0), col(d, d), col(d, 2 * d),
                  col(SC_W, OFF_SCX), col(SC_W, OFF_SCB), col(SC_W, OFF_SCC),
                  halo(OFF_SCX), halo(OFF_SCC), row(SSM_W), row(ATT_W),
                  const((SC_CONV, SC_W)), const((SC_W, d)), const((SSM_W, d)),
                  const((ATT_W, d)), const((d, d))],
        out_specs=row(d),
        out_shape=jax.ShapeDtypeStruct((t, d), F32),
        compiler_params=_cparams(("parallel",)),
        name="merge",
    )(x2, p2, p2, p2, p2, p2, p2, p2, p2, y_b, y_c, conv_w, w_a, w_b, w_c, w_o)


def _ffn_kernel(x_ref, hx_ref, g_ref, wg_ref, wu_ref, cg_ref, cu_ref, wd_ref, o_ref,
                h_ref, acc_ref, *, tm, seq):
    j = pl.program_id(1)

    @pl.when(j == 0)
    def _():
        first = (pl.program_id(0) * tm) % seq == 0
        x = jnp.concatenate([hx_ref[...], x_ref[...]], axis=0)
        ms = jnp.mean(x * x, axis=-1, keepdims=True)
        h = x * lax.rsqrt(ms + NORM_EPS) * g_ref[...]
        rows = lax.broadcasted_iota(I32, (tm + SUBLANES, 1), 0)
        h = jnp.where(jnp.logical_and(first, rows < SUBLANES), 0.0, h)
        h_ref[...] = h.astype(BF16)
        acc_ref[...] = jnp.zeros_like(acc_ref)

    h = h_ref[...]
    gt = jnp.dot(h, wg_ref[...], preferred_element_type=F32)
    ut = jnp.dot(h, wu_ref[...], preferred_element_type=F32)
    gc = _causal_conv3(gt[SUBLANES:], gt[:SUBLANES], cg_ref[...])
    uc = _causal_conv3(ut[SUBLANES:], ut[:SUBLANES], cu_ref[...])
    act = (jax.nn.silu(gc) * uc).astype(BF16)
    acc_ref[...] += jnp.dot(act, wd_ref[...], preferred_element_type=F32)

    @pl.when(j == pl.num_programs(1) - 1)
    def _():
        o_ref[...] = x_ref[...] + acc_ref[...]


def _ffn(x2, gain, w_up, conv_w, w_down, seq):
    t, d = x2.shape
    dff = w_down.shape[0]
    tm = _pick(seq, (512, 256, 128))
    tf = _pick(dff, (512, 256, 128))
    nf = dff // tf
    hb = tm // SUBLANES
    return pl.pallas_call(
        functools.partial(_ffn_kernel, tm=tm, seq=seq),
        grid=(t // tm, nf),
        in_specs=[pl.BlockSpec((tm, d), lambda i, j: (i, 0)),
                  pl.BlockSpec((SUBLANES, d), lambda i, j: (jnp.maximum(i * hb - 1, 0), 0)),
                  pl.BlockSpec((1, d), lambda i, j: (0, 0)),
                  pl.BlockSpec((d, tf), lambda i, j: (0, j)),
                  pl.BlockSpec((d, tf), lambda i, j: (0, j + nf)),
                  pl.BlockSpec((FFN_CONV, tf), lambda i, j: (0, j)),
                  pl.BlockSpec((FFN_CONV, tf), lambda i, j: (0, j + nf)),
                  pl.BlockSpec((tf, d), lambda i, j: (j, 0))],
        out_specs=pl.BlockSpec((tm, d), lambda i, j: (i, 0)),
        out_shape=jax.ShapeDtypeStruct((t, d), F32),
        scratch_shapes=[pltpu.VMEM((tm + SUBLANES, d), BF16), pltpu.VMEM((tm, d), F32)],
        compiler_params=_cparams(("parallel", "arbitrary")),
        name="ffn",
    )(x2, x2, gain, w_up, w_up, conv_w, conv_w, w_down)


def _norm_kernel(x_ref, g_ref, o_ref):
    x = x_ref[...]
    ms = jnp.mean(x * x, axis=-1, keepdims=True)
    o_ref[...] = x * lax.rsqrt(ms + NORM_EPS) * g_ref[...]


def _final_norm(x2, gain):
    t, d = x2.shape
    tm = _pick(t, (512, 256, 128))
    return pl.pallas_call(
        _norm_kernel,
        grid=(t // tm,),
        in_specs=[pl.BlockSpec((tm, d), lambda i: (i, 0)), pl.BlockSpec((1, d), lambda i: (0, 0))],
        out_specs=pl.BlockSpec((tm, d), lambda i: (i, 0)),
        out_shape=jax.ShapeDtypeStruct((t, d), F32),
        compiler_params=_cparams(("parallel",)),
        name="final_norm",
    )(x2, gain)


def _prep_w_in(w):
    d = w.shape[0]
    pad = jnp.zeros((d, NP_COLS - w.shape[1]), w.dtype)
    return jnp.concatenate([w[:, _ORIG_GATE_OFF:], w[:, :_ORIG_GATE_OFF], pad], axis=1).astype(BF16)


def _prep_ssm(a_re, a_im, log_dt, b_re, b_im, c_re, c_im):
    ar, ai = a_re.astype(F32), a_im.astype(F32)
    dt = jnp.exp(log_dt.astype(F32))[:, None]
    mag = jnp.exp(dt * ar)
    abr, abi = mag * jnp.cos(dt * ai), mag * jnp.sin(dt * ai)
    den = ar * ar + ai * ai
    cr = ((abr - 1.0) * ar + abi * ai) / den
    ci = (abi * ar - (abr - 1.0) * ai) / den
    br, bi = b_re.astype(F32), b_im.astype(F32)
    bbr = cr[..., None] * br - ci[..., None] * bi
    bbi = cr[..., None] * bi + ci[..., None] * br
    eye = jnp.eye(SSM_GROUPS, dtype=F32)

    def bdiag_in(m):
        return (jnp.transpose(m, (0, 2, 1))[:, :, None, :] * eye[:, None, :, None]).reshape(
            SSM_W, SSM_NS)

    def bdiag_out(m):
        return (jnp.transpose(m, (0, 2, 1))[:, :, None, :] * eye[:, None, :, None]).reshape(
            SSM_NS, SSM_W)

    bbd = jnp.concatenate([bdiag_in(bbr), bdiag_in(bbi)], axis=1).astype(BF16)
    cbd = jnp.concatenate([bdiag_out(c_re.astype(F32)), bdiag_out(-c_im.astype(F32))], axis=0).astype(BF16)
    a_rows = jnp.stack([abr.reshape(-1), abi.reshape(-1)], axis=0)
    return bbd, cbd, a_rows


def kernel(x, positions, norm_mix, w_in, sc_conv, ssm_a_re, ssm_a_im, ssm_log_dt, ssm_b_re, ssm_b_im,
           ssm_c_re, ssm_c_im, ssm_d, ssm_glu, w_branch_a, w_branch_b, w_branch_c, w_out, norm_ffn,
           w_up, ffn_conv, w_down, norm_final):
    bsz, seq, d = x.shape
    depth = w_in.shape[0]
    t = bsz * seq
    x2 = x.reshape(t, d).astype(F32)
    tabs = _rope_tables(positions.astype(I32))
    for l in range(depth):
        p2 = _inproj(x2, norm_mix[l].reshape(1, d).astype(F32), _prep_w_in(w_in[l]))
        q_r, k_r, v_t, qi_r, ki_r, w_s = _rope(p2, tabs, bsz, seq)
        bbd, cbd, a_rows = _prep_ssm(ssm_a_re[l], ssm_a_im[l], ssm_log_dt[l], ssm_b_re[l], ssm_b_im[l],
                                     ssm_c_re[l], ssm_c_im[l])
        y_b = _ssm(p2.reshape(bsz, seq, NP_COLS), bbd, cbd, a_rows,
                   ssm_d[l].reshape(1, SSM_W).astype(F32), ssm_glu[l].astype(BF16))
        y_c = _dsa(qi_r, w_s, ki_r, q_r, k_r, v_t, bsz, seq)
        x2 = _merge(x2, p2, y_b.reshape(t, SSM_W), y_c, sc_conv[l].astype(F32),
                    w_branch_a[l].astype(BF16), w_branch_b[l].astype(BF16), w_branch_c[l].astype(BF16),
                    w_out[l].astype(BF16), seq)
        x2 = _ffn(x2, norm_ffn[l].reshape(1, d).astype(F32), w_up[l].astype(BF16),
                  ffn_conv[l].astype(F32), w_down[l].astype(BF16), seq)
    out = _final_norm(x2, norm_final.reshape(1, d).astype(F32))
    return out.reshape(bsz, seq, d).astype(x.dtype)
```

```python
import functools
import math

import jax
import jax.numpy as jnp
from jax import lax
from jax.experimental import pallas as pl
from jax.experimental.pallas import tpu as pltpu

F32 = jnp.float32
BF16 = jnp.bfloat16
I32 = jnp.int32

SC_W = 512
SC_CONV = 3
SSM_W = 512
SSM_GROUP = 16
SSM_GROUPS = SSM_W // SSM_GROUP
SSM_STATE = 64
SSM_NS = SSM_GROUPS * SSM_STATE
N_Q_HEADS = 16
N_KV_HEADS = 4
HEAD_DIM = 64
ATT_W = N_Q_HEADS * HEAD_DIM
KV_W = N_KV_HEADS * HEAD_DIM
REP = N_Q_HEADS // N_KV_HEADS
ROT_DIM = HEAD_DIM // 4
ROT_HALF = ROT_DIM // 2
ROPE_THETA = 500000.0
N_IDX_HEADS = 8
IDX_DIM = 64
TOPK_MAX = 256
D_FF = 5632
FFN_CONV = 3
NORM_EPS = 1e-6

LANES = 128
SUBLANES = 8
ONES_ROWS = 16
VMEM_LIMIT = 56 * 1024 * 1024
NEG_BIG = -1e30
LOG2E = math.log2(math.e)
DENOM_FLOOR = 2.0 ** -64
INT_MIN = -(2 ** 31)

_ORIG_GATE_OFF = 3 * SC_W + SSM_W + ATT_W + 2 * KV_W + N_IDX_HEADS * IDX_DIM + IDX_DIM + N_IDX_HEADS
OFF_SCX = 3 * 2048
OFF_SCB = OFF_SCX + SC_W
OFF_SCC = OFF_SCB + SC_W
OFF_SSM = OFF_SCC + SC_W
OFF_Q = OFF_SSM + SSM_W
OFF_K = OFF_Q + ATT_W
OFF_V = OFF_K + KV_W
OFF_QI = OFF_V + KV_W
OFF_KI = OFF_QI + N_IDX_HEADS * IDX_DIM
NP_COLS = OFF_KI + LANES


def _cparams(sem, vmem=VMEM_LIMIT):
    return pltpu.CompilerParams(dimension_semantics=sem, vmem_limit_bytes=vmem)


def _pick(n, prefs):
    for p in prefs:
        if n % p == 0:
            return p
    return n


def _inproj_kernel(x_ref, g_ref, w_ref, o_ref, h_ref):
    @pl.when(pl.program_id(1) == 0)
    def _():
        x = x_ref[...]
        ms = jnp.mean(x * x, axis=-1, keepdims=True)
        h_ref[...] = (x * lax.rsqrt(ms + NORM_EPS) * g_ref[...]).astype(BF16)

    o_ref[...] = jnp.dot(h_ref[...], w_ref[...], preferred_element_type=F32).astype(o_ref.dtype)


def _inproj(x2, gain, w):
    t, d = x2.shape
    n = w.shape[1]
    tm = _pick(t, (1024, 512, 256, 128))
    tn = _pick(n, (1152, 384, 128))
    return pl.pallas_call(
        _inproj_kernel,
        grid=(t // tm, n // tn),
        in_specs=[pl.BlockSpec((tm, d), lambda i, j: (i, 0)),
                  pl.BlockSpec((1, d), lambda i, j: (0, 0)),
                  pl.BlockSpec((d, tn), lambda i, j: (0, j))],
        out_specs=pl.BlockSpec((tm, tn), lambda i, j: (i, j)),
        out_shape=jax.ShapeDtypeStruct((t, n), BF16),
        scratch_shapes=[pltpu.VMEM((tm, d), BF16)],
        compiler_params=_cparams(("parallel", "arbitrary")),
        name="inproj",
    )(x2, gain, w)


def _rope_table_kernel(pos_ref, invf_ref, a_ref, bm_ref, bp_ref):
    ang = pos_ref[...].astype(F32) * invf_ref[...]
    c = jnp.cos(ang)
    s = jnp.sin(ang)
    r = lax.broadcasted_iota(I32, ang.shape, 1) % HEAD_DIM
    lo = r < ROT_HALF
    hi = jnp.logical_and(r >= ROT_HALF, r < ROT_DIM)
    a_ref[...] = jnp.where(r < ROT_DIM, c, 1.0)
    bm_ref[...] = jnp.where(lo, -s, 0.0)
    bp_ref[...] = jnp.where(hi, s, 0.0)


def _rope_tables(positions):
    t = positions.size
    posb = jnp.broadcast_to(positions.reshape(t, 1), (t, LANES))
    inv_freq = ROPE_THETA ** (-jnp.arange(0, ROT_DIM, 2, dtype=F32) / ROT_DIM)
    lane = jnp.arange(LANES) % ROT_HALF
    invf = inv_freq[lane].reshape(1, LANES)
    tm = _pick(t, (1024, 512, 256, 128))
    spec = pl.BlockSpec((tm, LANES), lambda i: (i, 0))
    return pl.pallas_call(
        _rope_table_kernel,
        grid=(t // tm,),
        in_specs=[spec, pl.BlockSpec((1, LANES), lambda i: (0, 0))],
        out_specs=[spec, spec, spec],
        out_shape=[jax.ShapeDtypeStruct((t, LANES), F32)] * 3,
        compiler_params=_cparams(("parallel",)),
        name="rope_tables",
    )(posb, invf)


def _rope_kernel(q_ref, k_ref, v_ref, qi_ref, kiw_ref, a_ref, bm_ref, bp_ref,
                 qo_ref, ko_ref, vto_ref, qio_ref, kio_ref, wo_ref):
    a = a_ref[...]
    bm = bm_ref[...]
    bp = bp_ref[...]

    def rope(x):
        w = x.shape[1]
        reps = w // LANES
        xf = x.astype(F32)
        up = pltpu.roll(xf, w - ROT_HALF, 1)
        dn = pltpu.roll(xf, ROT_HALF, 1)
        return (xf * jnp.tile(a, (1, reps)) + up * jnp.tile(bm, (1, reps))
                + dn * jnp.tile(bp, (1, reps)))

    qo_ref[...] = (rope(q_ref[...]) * (HEAD_DIM ** -0.5 * LOG2E)).astype(BF16)
    ko_ref[...] = rope(k_ref[...]).astype(BF16)
    vto_ref[0] = v_ref[...].astype(F32).T.astype(BF16)
    qio_ref[...] = (rope(qi_ref[...]) * (IDX_DIM ** -0.5)).astype(BF16)
    kiw = kiw_ref[...]
    kio_ref[...] = rope(kiw).astype(BF16)
    wo_ref[...] = kiw.astype(F32) * (N_IDX_HEADS ** -0.5)


def _rope(p2, tabs, bsz, seq):
    t = p2.shape[0]
    tm = _pick(seq, (512, 256, 128))
    nsb = seq // tm
    a, bm, bp = tabs

    def col(width, off):
        return pl.BlockSpec((tm, width), lambda i, o=off // width: (i, o))

    tab = pl.BlockSpec((tm, LANES), lambda i: (i, 0))
    row = lambda w: pl.BlockSpec((tm, w), lambda i: (i, 0))
    return pl.pallas_call(
        _rope_kernel,
        grid=(t // tm,),
        in_specs=[col(ATT_W, OFF_Q), col(KV_W, OFF_K), col(KV_W, OFF_V),
                  col(N_IDX_HEADS * IDX_DIM, OFF_QI), col(LANES, OFF_KI), tab, tab, tab],
        out_specs=[row(ATT_W), row(KV_W),
                   pl.BlockSpec((1, KV_W, tm), lambda i: (i // nsb, 0, i % nsb)),
                   row(N_IDX_HEADS * IDX_DIM), row(LANES), row(LANES)],
        out_shape=[jax.ShapeDtypeStruct((t, ATT_W), BF16),
                   jax.ShapeDtypeStruct((t, KV_W), BF16),
                   jax.ShapeDtypeStruct((bsz, KV_W, seq), BF16),
                   jax.ShapeDtypeStruct((t, N_IDX_HEADS * IDX_DIM), BF16),
                   jax.ShapeDtypeStruct((t, LANES), BF16),
                   jax.ShapeDtypeStruct((t, LANES), F32)],
        compiler_params=_cparams(("parallel",)),
        name="rope",
    )(p2, p2, p2, p2, p2, a, bm, bp)


def _ssm_kernel(u_ref, bbd_ref, cbd_ref, a_ref, d_ref, wg_ref, o_ref, bu_ref, st_ref, *, chunk):
    @pl.when(pl.program_id(1) == 0)
    def _():
        st_ref[...] = jnp.zeros_like(st_ref)

    u = u_ref[0]
    bu_ref[...] = jnp.dot(u, bbd_ref[...], preferred_element_type=F32)
    ar = a_ref[0:1, :]
    ai = a_ref[1:2, :]

    def step(t, carry):
        xr, xi = carry
        br = bu_ref[pl.ds(t, 1), 0:SSM_NS]
        bi = bu_ref[pl.ds(t, 1), SSM_NS:2 * SSM_NS]
        nr = ar * xr - ai * xi + br
        ni = ar * xi + ai * xr + bi
        bu_ref[pl.ds(t, 1), 0:SSM_NS] = nr
        bu_ref[pl.ds(t, 1), SSM_NS:2 * SSM_NS] = ni
        return nr, ni

    xr, xi = lax.fori_loop(0, chunk, step, (st_ref[0:1, :], st_ref[1:2, :]), unroll=8)
    st_ref[0:1, :] = xr
    st_ref[1:2, :] = xi

    y = jnp.dot(bu_ref[...].astype(BF16), cbd_ref[...], preferred_element_type=F32)
    y = y + d_ref[...] * u.astype(F32)
    z = jax.nn.gelu(y)
    gate = jnp.dot(z.astype(BF16), wg_ref[...], preferred_element_type=F32)
    o_ref[0] = (z * jax.nn.sigmoid(gate)).astype(o_ref.dtype)


def _ssm(p3, bbd, cbd, a_bar, d_skip, w_glu):
    bsz, seq, _ = p3.shape
    chunk = _pick(seq, (256, 128))
    const = lambda shape: pl.BlockSpec(shape, lambda b, c: (0,) * len(shape))
    return pl.pallas_call(
        functools.partial(_ssm_kernel, chunk=chunk),
        grid=(bsz, seq // chunk),
        in_specs=[pl.BlockSpec((1, chunk, SSM_W), lambda b, c: (b, c, OFF_SSM // SSM_W)),
                  const((SSM_W, 2 * SSM_NS)), const((2 * SSM_NS, SSM_W)),
                  const((2, SSM_NS)), const((1, SSM_W)), const((SSM_W, SSM_W))],
        out_specs=pl.BlockSpec((1, chunk, SSM_W), lambda b, c: (b, c, 0)),
        out_shape=jax.ShapeDtypeStruct((bsz, seq, SSM_W), BF16),
        scratch_shapes=[pltpu.VMEM((chunk, 2 * SSM_NS), F32), pltpu.VMEM((2, SSM_NS), F32)],
        compiler_params=_cparams(("parallel", "arbitrary")),
        name="ssm",
    )(p3, bbd, cbd, a_bar, d_skip, w_glu)


def _dsa_kernel(qi_ref, ws_ref, ki_ref, q_ref, k_ref, vt_ref, o_ref,
                key_ref, mask_ref, qit_ref, d0_ref, d1_ref, qt_ref, s0_ref, s1_ref, s2_ref, p_ref, acc_ref, m_ref,
                *, tq, seq, ksel, fixed_bits):
    qb = pl.program_id(1)
    gid = pl.program_id(2)
    t0 = qb * tq
    cw = tq
    nc = qb + 1
    ca = 2 * tq
    nca = (qb + 2) // 2
    ks = 128
    vs = 64
    kf = float(ksel)
    jbits = (seq - 1).bit_length()
    qidx = t0 + lax.broadcasted_iota(I32, (1, tq), 1)

    @pl.when(gid == 0)
    def _select():
        qit = qi_ref[...].astype(F32).T
        qit_ref[...] = jnp.concatenate(
            [qit[h * IDX_DIM:(h + 1) * IDX_DIM, :] for h in range(N_IDX_HEADS)], axis=1).astype(BF16)
        wt = ws_ref[...].T
        wrow = jnp.concatenate([wt[IDX_DIM + h:IDX_DIM + h + 1, :] for h in range(N_IDX_HEADS)], axis=1)

        def idx_dots(c, d_ref):
            start = pl.multiple_of(c * cw, cw)
            kib = ki_ref[pl.ds(start, cw), :][:, :IDX_DIM]
            d_ref[...] = jnp.dot(kib, qit_ref[...], preferred_element_type=F32)

        def idx_scores(c, d_ref):
            for i in range(cw // ks):
                start = pl.multiple_of(c * cw, cw) + i * ks
                r = jnp.maximum(d_ref[i * ks:(i + 1) * ks, :], 0.0) * wrow
                isc = r[:, 0:tq]
                for h in range(1, N_IDX_HEADS):
                    isc = isc + r[:, h * tq:(h + 1) * tq]
                kidx = start + lax.broadcasted_iota(I32, (ks, tq), 0)
                isc = jnp.where(isc == 0.0, 0.0, isc)
                isc = jnp.where(kidx <= qidx, isc, -jnp.inf)
                bits = pltpu.bitcast(isc, I32)
                key_ref[pl.ds(start, ks), :] = bits ^ ((bits >> 31) & 0x7FFFFFFF)

        idx_dots(0, d0_ref)

        def idx_pair(c2, _):
            c = 2 * c2
            idx_dots(c + 1, d1_ref)
            idx_scores(c, d0_ref)
            idx_dots(jnp.minimum(c + 2, nc - 1), d0_ref)
            idx_scores(c + 1, d1_ref)
            return 0

        lax.fori_loop(0, nc // 2, idx_pair, 0)

        @pl.when(nc % 2 == 1)
        def _():
            idx_scores(nc - 1, d0_ref)

        @pl.when(qb % 2 == 0)
        def _():
            mask_ref[pl.ds(pl.multiple_of((qb + 1) * cw, cw), cw), :] = jnp.zeros((cw, tq), BF16)

        small = qidx < ksel

        def key_rows(c):
            return pl.multiple_of(c * cw, cw) + lax.broadcasted_iota(I32, (cw, tq), 0)

        def count(pred):
            def body(c, cnt):
                blk = key_ref[pl.ds(pl.multiple_of(c * cw, cw), cw), :]
                hit = pred(blk, c)
                return cnt + jnp.sum(hit.reshape(cw // 32, 32, tq), axis=0)
            cnt = lax.fori_loop(0, nc, body, jnp.zeros((32, tq), F32))
            return jnp.sum(cnt, axis=0, keepdims=True)

        def count_ge(cand):
            return count(lambda x, c: jnp.where(x >= cand, 1.0, 0.0))

        c0 = count_ge(jnp.zeros((1, tq), I32))
        thr = jnp.where(c0 >= kf, 0, INT_MIN).astype(I32)
        done = jnp.where(jnp.logical_or(small, c0 == kf), 1.0, 0.0)

        def bit_step(i, thr, done):
            cand = thr + jnp.left_shift(jnp.int32(1), 30 - i)
            cnt = count_ge(cand)
            return (jnp.where(cnt >= kf, cand, thr),
                    jnp.maximum(done, jnp.where(cnt == kf, 1.0, 0.0)))

        thr, done = lax.fori_loop(0, fixed_bits, lambda i, c: bit_step(i, *c), (thr, done))

        def w_cond(c):
            return jnp.logical_and(c[0] < 31, c[3] < 0.5)

        def w_body(c):
            thr, done = bit_step(c[0], c[1], c[2])
            thr, done = bit_step(c[0] + 1, thr, done)
            return c[0] + 2, thr, done, jnp.min(done)

        _, thr, done, settled = lax.while_loop(
            w_cond, w_body, (jnp.int32(fixed_bits), thr, done, jnp.min(done)))
        thr = jnp.where(small, INT_MIN, thr)

        def tie_index():
            n_ge = count_ge(thr)
            need = kf - count_ge(thr + 1)
            tie = jnp.logical_and(jnp.logical_not(small), n_ge > kf)

            def count_tie_below(cand):
                return count(lambda x, c: jnp.where(
                    x == thr, jnp.where(key_rows(c) < cand, 1.0, 0.0), 0.0))

            def jbit_body(i, jj):
                cand = jj + jnp.left_shift(jnp.int32(1), jbits - 1 - i)
                return jnp.where(count_tie_below(cand) < need, cand, jj)

            jj = lax.fori_loop(0, jbits, jbit_body, jnp.zeros((1, tq), I32))
            return jnp.where(tie, jj, seq)

        jmax = lax.cond(settled < 0.5, tie_index, lambda: jnp.full((1, tq), seq, I32))

        def mask_body(c, _):
            rows = pl.ds(pl.multiple_of(c * cw, cw), cw)
            x = key_ref[rows, :]
            kidx = key_rows(c)
            b = jnp.where(x > thr, 1.0, jnp.where(x == thr, jnp.where(kidx <= jmax, 1.0, 0.0), 0.0))
            mask_ref[rows, :] = jnp.where(kidx <= qidx, b, 0.0).astype(BF16)
            return 0

        lax.fori_loop(0, nc, mask_body, 0)

    qt = q_ref[...].astype(F32).T
    vrow = pl.multiple_of(gid * HEAD_DIM, HEAD_DIM)
    qt_ref[...] = jnp.zeros(qt_ref.shape, BF16)
    for r in range(REP):
        qt_ref[pl.ds(vrow, HEAD_DIM), r * tq:(r + 1) * tq] = qt[r * HEAD_DIM:(r + 1) * HEAD_DIM, :].astype(BF16)
    acc_ref[...] = jnp.zeros(acc_ref.shape, F32)
    ones_rows = jnp.ones((ONES_ROWS, ca), BF16)

    def scores(c, s_ref):
        start = pl.multiple_of(c * ca, ca)
        s_ref[...] = jnp.dot(k_ref[pl.ds(start, ca), :], qt_ref[...], preferred_element_type=F32)

    def accumulate(c, m_prev, m_new):
        vt = jnp.concatenate(
            [vt_ref[0, pl.ds(vrow, HEAD_DIM), pl.ds(pl.multiple_of(c * ca, ca), ca)], ones_rows], axis=0)
        acc_ref[...] = jnp.exp2(m_prev - m_new) * acc_ref[...] + jnp.dot(
            vt, p_ref[...], preferred_element_type=F32)

    def mask_rows(c, i):
        return mask_ref[pl.ds(pl.multiple_of(c * ca, ca) + i * vs, vs), :]

    def softmax_chunk(c, s_ref, m_prev):
        m_new = m_prev
        for i in range(ca // vs):
            m_new = jnp.maximum(m_new, jnp.max(s_ref[i * vs:(i + 1) * vs, :], axis=0, keepdims=True))
        for i in range(ca // vs):
            p = jnp.exp2(s_ref[i * vs:(i + 1) * vs, :] - m_new).astype(BF16)
            p_ref[i * vs:(i + 1) * vs, :] = p * jnp.tile(mask_rows(c, i), (1, REP))
        accumulate(c, m_prev, m_new)
        return m_new

    m_init = jnp.full((1, REP * tq), NEG_BIG, F32)
    scores(0, s0_ref)

    def att_triple(c3, m):
        c = 3 * c3
        scores(c + 1, s1_ref)
        m = softmax_chunk(c, s0_ref, m)
        scores(c + 2, s2_ref)
        m = softmax_chunk(c + 1, s1_ref, m)
        scores(jnp.minimum(c + 3, nca - 1), s0_ref)
        return softmax_chunk(c + 2, s2_ref, m)

    m_ref[...] = lax.fori_loop(0, nca // 3, att_triple, m_init)
    rem = nca % 3
    base = nca - rem

    @pl.when(rem >= 1)
    def _():
        scores(jnp.minimum(base + 1, nca - 1), s1_ref)
        m_ref[...] = softmax_chunk(base, s0_ref, m_ref[...])

    @pl.when(rem == 2)
    def _():
        softmax_chunk(base + 1, s1_ref, m_ref[...])

    @pl.when(jnp.min(acc_ref[HEAD_DIM:HEAD_DIM + 1, :]) < DENOM_FLOOR)
    def _():
        acc_ref[...] = jnp.zeros(acc_ref.shape, F32)

        def exact_chunk(c, m_prev):
            scores(c, s0_ref)
            m_new = m_prev
            for i in range(ca // vs):
                b = jnp.where(jnp.tile(mask_rows(c, i), (1, REP)) > 0, s0_ref[i * vs:(i + 1) * vs, :], NEG_BIG)
                s1_ref[i * vs:(i + 1) * vs, :] = b
                m_new = jnp.maximum(m_new, jnp.max(b, axis=0, keepdims=True))
            for i in range(ca // vs):
                p_ref[i * vs:(i + 1) * vs, :] = jnp.exp2(s1_ref[i * vs:(i + 1) * vs, :] - m_new).astype(BF16)
            accumulate(c, m_prev, m_new)
            return m_new

        lax.fori_loop(0, nca, exact_chunk, m_init)

    acc = acc_ref[...]
    out_t = acc[:HEAD_DIM] / acc[HEAD_DIM:HEAD_DIM + 1]
    out_t = jnp.concatenate([out_t[:, r * tq:(r + 1) * tq] for r in range(REP)], axis=0)
    o_ref[...] = out_t.T.astype(o_ref.dtype)


def _dsa(qi_r, w_s, ki_r, q_r, k_r, v_t, bsz, seq):
    t = q_r.shape[0]
    tq = 256
    assert seq % (2 * tq) == 0
    ksel = min(TOPK_MAX, seq // 4)
    assert ksel <= tq
    nqb = seq // tq
    rowblk = lambda w: pl.BlockSpec((tq, w), lambda b, i, g: (b * nqb + i, 0))
    return pl.pallas_call(
        functools.partial(_dsa_kernel, tq=tq, seq=seq, ksel=ksel, fixed_bits=15),
        grid=(bsz, nqb, N_KV_HEADS),
        in_specs=[rowblk(N_IDX_HEADS * IDX_DIM), rowblk(LANES),
                  pl.BlockSpec((seq, LANES), lambda b, i, g: (b, 0)),
                  pl.BlockSpec((tq, REP * HEAD_DIM), lambda b, i, g: (b * nqb + i, g)),
                  pl.BlockSpec((seq, KV_W), lambda b, i, g: (b, 0)),
                  pl.BlockSpec((1, KV_W, seq), lambda b, i, g: (b, 0, 0))],
        out_specs=pl.BlockSpec((tq, REP * HEAD_DIM), lambda b, i, g: (b * nqb + i, g)),
        out_shape=jax.ShapeDtypeStruct((t, ATT_W), BF16),
        scratch_shapes=[pltpu.VMEM((seq, tq), I32),
                        pltpu.VMEM((seq, tq), BF16),
                        pltpu.VMEM((IDX_DIM, N_IDX_HEADS * tq), BF16),
                        pltpu.VMEM((tq, N_IDX_HEADS * tq), F32),
                        pltpu.VMEM((tq, N_IDX_HEADS * tq), F32),
                        pltpu.VMEM((KV_W, REP * tq), BF16),
                        pltpu.VMEM((2 * tq, REP * tq), F32),
                        pltpu.VMEM((2 * tq, REP * tq), F32),
                        pltpu.VMEM((2 * tq, REP * tq), F32),
                        pltpu.VMEM((2 * tq, REP * tq), BF16),
                        pltpu.VMEM((HEAD_DIM + ONES_ROWS, REP * tq), F32),
                        pltpu.VMEM((1, REP * tq), F32)],
        compiler_params=_cparams(("parallel", "arbitrary", "arbitrary")),
        name="dsa",
    )(qi_r, w_s, ki_r, q_r, k_r, v_t)


def _causal_conv3(cur, halo, w):
    tm = cur.shape[0]
    ext = jnp.concatenate([halo, cur], axis=0)
    return (w[2:3, :] * cur + w[1:2, :] * ext[SUBLANES - 1:SUBLANES - 1 + tm]
            + w[0:1, :] * ext[SUBLANES - 2:SUBLANES - 2 + tm])


def _merge_kernel(x_ref, ga_ref, gb_ref, gc_ref, scx_ref, scb_ref, scc_ref, hx_ref, hc_ref,
                  yb_ref, yc_ref, cw_ref, wa_ref, wb_ref, wc_ref, wo_ref, o_ref, *, tm, seq):
    first = (pl.program_id(0) * tm) % seq == 0
    cx = scc_ref[...].astype(F32) * scx_ref[...].astype(F32)
    halo = hc_ref[...].astype(F32) * hx_ref[...].astype(F32)
    halo = jnp.where(first, 0.0, halo)
    ya = scb_ref[...].astype(F32) * _causal_conv3(cx, halo, cw_ref[...])
    dot = lambda a, w: jnp.dot(a, w[...], preferred_element_type=F32)
    m = jax.nn.sigmoid(ga_ref[...].astype(F32)) * dot(ya.astype(BF16), wa_ref)
    m = m + jax.nn.sigmoid(gb_ref[...].astype(F32)) * dot(yb_ref[...], wb_ref)
    m = m + jax.nn.sigmoid(gc_ref[...].astype(F32)) * dot(yc_ref[...], wc_ref)
    o_ref[...] = x_ref[...] + dot(m.astype(BF16), wo_ref)


def _merge(x2, p2, y_b, y_c, conv_w, w_a, w_b, w_c, w_o, seq):
    t, d = x2.shape
    tm = _pick(seq, (256, 128))
    hb = tm // SUBLANES

    def col(width, off):
        return pl.BlockSpec((tm, width), lambda i, o=off // width: (i, o))

    def halo(off):
        return pl.BlockSpec((SUBLANES, SC_W), lambda i, o=off // SC_W: (jnp.maximum(i * hb - 1, 0), o))

    def const(shape):
        return pl.BlockSpec(shape, lambda i: (0, 0), pipeline_mode=pl.Buffered(1))

    row = lambda w: pl.BlockSpec((tm, w), lambda i: (i, 0))
    return pl.pallas_call(
        functools.partial(_merge_kernel, tm=tm, seq=seq),
        grid=(t // tm,),
        in_specs=[row(d), col(d, 0), col(d, d), col(d, 2 * d),
                  col(SC_W, OFF_SCX), col(SC_W, OFF_SCB), col(SC_W, OFF_SCC),
                  halo(OFF_SCX), halo(OFF_SCC), row(SSM_W), row(ATT_W),
                  const((SC_CONV, SC_W)), const((SC_W, d)), const((SSM_W, d)),
                  const((ATT_W, d)), const((d, d))],
        out_specs=row(d),
        out_shape=jax.ShapeDtypeStruct((t, d), F32),
        compiler_params=_cparams(("parallel",)),
        name="merge",
    )(x2, p2, p2, p2, p2, p2, p2, p2, p2, y_b, y_c, conv_w, w_a, w_b, w_c, w_o)


def _ffn_kernel(x_ref, hx_ref, g_ref, wg_ref, wu_ref, cg_ref, cu_ref, wd_ref, og_ref, o_ref,
                h_ref, acc_ref, *, tm, seq, out_norm):
    j = pl.program_id(1)

    @pl.when(j == 0)
    def _():
        first = (pl.program_id(0) * tm) % seq == 0
        x = jnp.concatenate([hx_ref[...], x_ref[...]], axis=0)
        ms = jnp.mean(x * x, axis=-1, keepdims=True)
        h = x * lax.rsqrt(ms + NORM_EPS) * g_ref[...]
        rows = lax.broadcasted_iota(I32, (tm + SUBLANES, 1), 0)
        h = jnp.where(jnp.logical_and(first, rows < SUBLANES), 0.0, h)
        h_ref[...] = h.astype(BF16)
        acc_ref[...] = jnp.zeros_like(acc_ref)

    h = h_ref[...]
    gt = jnp.dot(h, wg_ref[...], preferred_element_type=F32)
    ut = jnp.dot(h, wu_ref[...], preferred_element_type=F32)
    gc = _causal_conv3(gt[SUBLANES:], gt[:SUBLANES], cg_ref[...])
    uc = _causal_conv3(ut[SUBLANES:], ut[:SUBLANES], cu_ref[...])
    act = (jax.nn.silu(gc) * uc).astype(BF16)
    acc_ref[...] += jnp.dot(act, wd_ref[...], preferred_element_type=F32)

    @pl.when(j == pl.num_programs(1) - 1)
    def _():
        y = x_ref[...] + acc_ref[...]
        if out_norm:
            ms = jnp.mean(y * y, axis=-1, keepdims=True)
            y = y * lax.rsqrt(ms + NORM_EPS) * og_ref[...]
        o_ref[...] = y


def _ffn(x2, gain, w_up, conv_w, w_down, out_gain, seq, out_norm):
    t, d = x2.shape
    dff = w_down.shape[0]
    tm = _pick(seq, (512, 256, 128))
    tf = _pick(dff, (512, 256, 128))
    nf = dff // tf
    hb = tm // SUBLANES
    return pl.pallas_call(
        functools.partial(_ffn_kernel, tm=tm, seq=seq, out_norm=out_norm),
        grid=(t // tm, nf),
        in_specs=[pl.BlockSpec((tm, d), lambda i, j: (i, 0)),
                  pl.BlockSpec((SUBLANES, d), lambda i, j: (jnp.maximum(i * hb - 1, 0), 0)),
                  pl.BlockSpec((1, d), lambda i, j: (0, 0)),
                  pl.BlockSpec((d, tf), lambda i, j: (0, j)),
                  pl.BlockSpec((d, tf), lambda i, j: (0, j + nf)),
                  pl.BlockSpec((FFN_CONV, tf), lambda i, j: (0, j)),
                  pl.BlockSpec((FFN_CONV, tf), lambda i, j: (0, j + nf)),
                  pl.BlockSpec((tf, d), lambda i, j: (j, 0)),
                  pl.BlockSpec((1, d), lambda i, j: (0, 0))],
        out_specs=pl.BlockSpec((tm, d), lambda i, j: (i, 0)),
        out_shape=jax.ShapeDtypeStruct((t, d), F32),
        scratch_shapes=[pltpu.VMEM((tm + SUBLANES, d), BF16), pltpu.VMEM((tm, d), F32)],
        compiler_params=_cparams(("parallel", "arbitrary")),
        name="ffn",
    )(x2, x2, gain, w_up, w_up, conv_w, conv_w, w_down, out_gain)


def _prep_w_in(w):
    d = w.shape[0]
    pad = jnp.zeros((d, NP_COLS - w.shape[1]), BF16)
    return jnp.concatenate([w[:, _ORIG_GATE_OFF:].astype(BF16), w[:, :_ORIG_GATE_OFF].astype(BF16), pad], axis=1)


def _prep_ssm(a_re, a_im, log_dt, b_re, b_im, c_re, c_im):
    ar, ai = a_re.astype(F32), a_im.astype(F32)
    dt = jnp.exp(log_dt.astype(F32))[:, None]
    mag = jnp.exp(dt * ar)
    abr, abi = mag * jnp.cos(dt * ai), mag * jnp.sin(dt * ai)
    den = ar * ar + ai * ai
    cr = ((abr - 1.0) * ar + abi * ai) / den
    ci = (abi * ar - (abr - 1.0) * ai) / den
    br, bi = b_re.astype(F32), b_im.astype(F32)
    bbr = cr[..., None] * br - ci[..., None] * bi
    bbi = cr[..., None] * bi + ci[..., None] * br
    eye = jnp.eye(SSM_GROUPS, dtype=F32)

    def bdiag_in(m):
        return (jnp.transpose(m, (0, 2, 1))[:, :, None, :] * eye[:, None, :, None]).reshape(
            SSM_W, SSM_NS)

    def bdiag_out(m):
        return (jnp.transpose(m, (0, 2, 1))[:, :, None, :] * eye[:, None, :, None]).reshape(
            SSM_NS, SSM_W)

    bbd = jnp.concatenate([bdiag_in(bbr), bdiag_in(bbi)], axis=1).astype(BF16)
    cbd = jnp.concatenate([bdiag_out(c_re.astype(F32)), bdiag_out(-c_im.astype(F32))], axis=0).astype(BF16)
    a_rows = jnp.stack([abr.reshape(-1), abi.reshape(-1)], axis=0)
    return bbd, cbd, a_rows


def kernel(x, positions, norm_mix, w_in, sc_conv, ssm_a_re, ssm_a_im, ssm_log_dt, ssm_b_re, ssm_b_im,
           ssm_c_re, ssm_c_im, ssm_d, ssm_glu, w_branch_a, w_branch_b, w_branch_c, w_out, norm_ffn,
           w_up, ffn_conv, w_down, norm_final):
    bsz, seq, d = x.shape
    depth = w_in.shape[0]
    t = bsz * seq
    x2 = x.reshape(t, d).astype(F32)
    tabs = _rope_tables(positions.astype(I32))
    for l in range(depth):
        p2 = _inproj(x2, norm_mix[l].reshape(1, d).astype(F32), _prep_w_in(w_in[l]))
        q_r, k_r, v_t, qi_r, ki_r, w_s = _rope(p2, tabs, bsz, seq)
        bbd, cbd, a_rows = _prep_ssm(ssm_a_re[l], ssm_a_im[l], ssm_log_dt[l], ssm_b_re[l], ssm_b_im[l],
                                     ssm_c_re[l], ssm_c_im[l])
        y_b = _ssm(p2.reshape(bsz, seq, NP_COLS), bbd, cbd, a_rows,
                   ssm_d[l].reshape(1, SSM_W).astype(F32), ssm_glu[l].astype(BF16))
        y_c = _dsa(qi_r, w_s, ki_r, q_r, k_r, v_t, bsz, seq)
        x2 = _merge(x2, p2, y_b.reshape(t, SSM_W), y_c, sc_conv[l].astype(F32),
                    w_branch_a[l].astype(BF16), w_branch_b[l].astype(BF16), w_branch_c[l].astype(BF16),
                    w_out[l].astype(BF16), seq)
        x2 = _ffn(x2, norm_ffn[l].reshape(1, d).astype(F32), w_up[l].astype(BF16),
                  ffn_conv[l].astype(F32), w_down[l].astype(BF16),
                  norm_final.reshape(1, d).astype(F32), seq, out_norm=(l == depth - 1))
    return x2.reshape(bsz, seq, d).astype(x.dtype)
```

```python
import functools
import math

import jax
import jax.numpy as jnp
from jax import lax
from jax.experimental import pallas as pl
from jax.experimental.pallas import tpu as pltpu

F32 = jnp.float32
BF16 = jnp.bfloat16
I32 = jnp.int32

SC_W = 512
SC_CONV = 3
SSM_W = 512
SSM_GROUP = 16
SSM_GROUPS = SSM_W // SSM_GROUP
SSM_STATE = 64
SSM_NS = SSM_GROUPS * SSM_STATE
N_Q_HEADS = 16
N_KV_HEADS = 4
HEAD_DIM = 64
ATT_W = N_Q_HEADS * HEAD_DIM
KV_W = N_KV_HEADS * HEAD_DIM
REP = N_Q_HEADS // N_KV_HEADS
ROT_DIM = HEAD_DIM // 4
ROT_HALF = ROT_DIM // 2
ROPE_THETA = 500000.0
N_IDX_HEADS = 8
IDX_DIM = 64
TOPK_MAX = 256
D_FF = 5632
FFN_CONV = 3
NORM_EPS = 1e-6

LANES = 128
SUBLANES = 8
ONES_ROWS = 16
VMEM_LIMIT = 56 * 1024 * 1024
NEG_BIG = -1e30
LOG2E = math.log2(math.e)
DENOM_FLOOR = 2.0 ** -64
INT_MIN = -(2 ** 31)

_ORIG_GATE_OFF = 3 * SC_W + SSM_W + ATT_W + 2 * KV_W + N_IDX_HEADS * IDX_DIM + IDX_DIM + N_IDX_HEADS
OFF_SCX = 3 * 2048
OFF_SCB = OFF_SCX + SC_W
OFF_SCC = OFF_SCB + SC_W
OFF_SSM = OFF_SCC + SC_W
OFF_Q = OFF_SSM + SSM_W
OFF_K = OFF_Q + ATT_W
OFF_V = OFF_K + KV_W
OFF_QI = OFF_V + KV_W
OFF_KI = OFF_QI + N_IDX_HEADS * IDX_DIM
NP_COLS = OFF_KI + LANES


def _cparams(sem, vmem=VMEM_LIMIT):
    return pltpu.CompilerParams(dimension_semantics=sem, vmem_limit_bytes=vmem)


def _pick(n, prefs):
    for p in prefs:
        if n % p == 0:
            return p
    return n


def _inproj_kernel(x_ref, g_ref, w_ref, o_ref, h_ref):
    @pl.when(pl.program_id(1) == 0)
    def _():
        x = x_ref[...]
        ms = jnp.mean(x * x, axis=-1, keepdims=True)
        h_ref[...] = (x * lax.rsqrt(ms + NORM_EPS) * g_ref[...]).astype(BF16)

    o_ref[...] = jnp.dot(h_ref[...], w_ref[...], preferred_element_type=F32).astype(o_ref.dtype)


def _inproj(x2, gain, w):
    t, d = x2.shape
    n = w.shape[1]
    tm = _pick(t, (1024, 512, 256, 128))
    tn = _pick(n, (1152, 384, 128))
    return pl.pallas_call(
        _inproj_kernel,
        grid=(t // tm, n // tn),
        in_specs=[pl.BlockSpec((tm, d), lambda i, j: (i, 0)),
                  pl.BlockSpec((1, d), lambda i, j: (0, 0)),
                  pl.BlockSpec((d, tn), lambda i, j: (0, j))],
        out_specs=pl.BlockSpec((tm, tn), lambda i, j: (i, j)),
        out_shape=jax.ShapeDtypeStruct((t, n), BF16),
        scratch_shapes=[pltpu.VMEM((tm, d), BF16)],
        compiler_params=_cparams(("parallel", "arbitrary")),
        name="inproj",
    )(x2, gain, w)


def _rope_table_kernel(pos_ref, invf_ref, a_ref, bm_ref, bp_ref):
    ang = pos_ref[...].astype(F32) * invf_ref[...]
    c = jnp.cos(ang)
    s = jnp.sin(ang)
    r = lax.broadcasted_iota(I32, ang.shape, 1) % HEAD_DIM
    lo = r < ROT_HALF
    hi = jnp.logical_and(r >= ROT_HALF, r < ROT_DIM)
    a_ref[...] = jnp.where(r < ROT_DIM, c, 1.0)
    bm_ref[...] = jnp.where(lo, -s, 0.0)
    bp_ref[...] = jnp.where(hi, s, 0.0)


def _rope_tables(positions):
    t = positions.size
    posb = jnp.broadcast_to(positions.reshape(t, 1), (t, LANES))
    inv_freq = ROPE_THETA ** (-jnp.arange(0, ROT_DIM, 2, dtype=F32) / ROT_DIM)
    lane = jnp.arange(LANES) % ROT_HALF
    invf = inv_freq[lane].reshape(1, LANES)
    tm = _pick(t, (1024, 512, 256, 128))
    spec = pl.BlockSpec((tm, LANES), lambda i: (i, 0))
    return pl.pallas_call(
        _rope_table_kernel,
        grid=(t // tm,),
        in_specs=[spec, pl.BlockSpec((1, LANES), lambda i: (0, 0))],
        out_specs=[spec, spec, spec],
        out_shape=[jax.ShapeDtypeStruct((t, LANES), F32)] * 3,
        compiler_params=_cparams(("parallel",)),
        name="rope_tables",
    )(posb, invf)


def _rope_kernel(q_ref, k_ref, v_ref, qi_ref, kiw_ref, a_ref, bm_ref, bp_ref,
                 qo_ref, ko_ref, vto_ref, qio_ref, kio_ref, wo_ref, kn_ref):
    a = a_ref[...]
    bm = bm_ref[...]
    bp = bp_ref[...]

    def rope(x):
        w = x.shape[1]
        reps = w // LANES
        xf = x.astype(F32)
        up = pltpu.roll(xf, w - ROT_HALF, 1)
        dn = pltpu.roll(xf, ROT_HALF, 1)
        return (xf * jnp.tile(a, (1, reps)) + up * jnp.tile(bm, (1, reps))
                + dn * jnp.tile(bp, (1, reps)))

    qo_ref[...] = (rope(q_ref[...]) * (HEAD_DIM ** -0.5 * LOG2E)).astype(BF16)
    kr = rope(k_ref[...]).astype(BF16)
    ko_ref[...] = kr
    ksq = kr.astype(F32) * kr.astype(F32)
    lane = lax.broadcasted_iota(I32, (1, LANES), 1)
    kn = jnp.zeros((1, LANES), F32)
    for g in range(N_KV_HEADS):
        n2 = jnp.sum(ksq[:, g * HEAD_DIM:(g + 1) * HEAD_DIM], axis=1, keepdims=True)
        kn = jnp.where(lane == g, jnp.max(n2, axis=0, keepdims=True), kn)
    kn_ref[0] = jnp.broadcast_to(kn, (SUBLANES, LANES))
    vto_ref[0] = v_ref[...].astype(F32).T.astype(BF16)
    qio_ref[...] = (rope(qi_ref[...]) * (IDX_DIM ** -0.5)).astype(BF16)
    kiw = kiw_ref[...]
    kio_ref[...] = rope(kiw).astype(BF16)
    wo_ref[...] = kiw.astype(F32) * (N_IDX_HEADS ** -0.5)


def _rope(p2, tabs, bsz, seq):
    t = p2.shape[0]
    tm = _pick(seq, (512, 256, 128))
    nsb = seq // tm
    a, bm, bp = tabs

    def col(width, off):
        return pl.BlockSpec((tm, width), lambda i, o=off // width: (i, o))

    tab = pl.BlockSpec((tm, LANES), lambda i: (i, 0))
    row = lambda w: pl.BlockSpec((tm, w), lambda i: (i, 0))
    return pl.pallas_call(
        _rope_kernel,
        grid=(t // tm,),
        in_specs=[col(ATT_W, OFF_Q), col(KV_W, OFF_K), col(KV_W, OFF_V),
                  col(N_IDX_HEADS * IDX_DIM, OFF_QI), col(LANES, OFF_KI), tab, tab, tab],
        out_specs=[row(ATT_W), row(KV_W),
                   pl.BlockSpec((1, KV_W, tm), lambda i: (i // nsb, 0, i % nsb)),
                   row(N_IDX_HEADS * IDX_DIM), row(LANES), row(LANES),
                   pl.BlockSpec((1, SUBLANES, LANES), lambda i: (i, 0, 0))],
        out_shape=[jax.ShapeDtypeStruct((t, ATT_W), BF16),
                   jax.ShapeDtypeStruct((t, KV_W), BF16),
                   jax.ShapeDtypeStruct((bsz, KV_W, seq), BF16),
                   jax.ShapeDtypeStruct((t, N_IDX_HEADS * IDX_DIM), BF16),
                   jax.ShapeDtypeStruct((t, LANES), BF16),
                   jax.ShapeDtypeStruct((t, LANES), F32),
                   jax.ShapeDtypeStruct((t // tm, SUBLANES, LANES), F32)],
        compiler_params=_cparams(("parallel",)),
        name="rope",
    )(p2, p2, p2, p2, p2, a, bm, bp)


def _ssm_kernel(u_ref, bbd_ref, cbd_ref, a_ref, d_ref, wg_ref, o_ref, bu_ref, st_ref, *, chunk):
    @pl.when(pl.program_id(1) == 0)
    def _():
        st_ref[...] = jnp.zeros_like(st_ref)

    u = u_ref[0]
    bu_ref[...] = jnp.dot(u, bbd_ref[...], preferred_element_type=F32)
    ar = a_ref[0:1, :]
    ai = a_ref[1:2, :]

    def step(t, carry):
        xr, xi = carry
        br = bu_ref[pl.ds(t, 1), 0:SSM_NS]
        bi = bu_ref[pl.ds(t, 1), SSM_NS:2 * SSM_NS]
        nr = ar * xr - ai * xi + br
        ni = ar * xi + ai * xr + bi
        bu_ref[pl.ds(t, 1), 0:SSM_NS] = nr
        bu_ref[pl.ds(t, 1), SSM_NS:2 * SSM_NS] = ni
        return nr, ni

    xr, xi = lax.fori_loop(0, chunk, step, (st_ref[0:1, :], st_ref[1:2, :]), unroll=8)
    st_ref[0:1, :] = xr
    st_ref[1:2, :] = xi

    y = jnp.dot(bu_ref[...].astype(BF16), cbd_ref[...], preferred_element_type=F32)
    y = y + d_ref[...] * u.astype(F32)
    z = jax.nn.gelu(y)
    gate = jnp.dot(z.astype(BF16), wg_ref[...], preferred_element_type=F32)
    o_ref[0] = (z * jax.nn.sigmoid(gate)).astype(o_ref.dtype)


def _ssm(p3, bbd, cbd, a_bar, d_skip, w_glu):
    bsz, seq, _ = p3.shape
    chunk = _pick(seq, (256, 128))
    const = lambda shape: pl.BlockSpec(shape, lambda b, c: (0,) * len(shape))
    return pl.pallas_call(
        functools.partial(_ssm_kernel, chunk=chunk),
        grid=(bsz, seq // chunk),
        in_specs=[pl.BlockSpec((1, chunk, SSM_W), lambda b, c: (b, c, OFF_SSM // SSM_W)),
                  const((SSM_W, 2 * SSM_NS)), const((2 * SSM_NS, SSM_W)),
                  const((2, SSM_NS)), const((1, SSM_W)), const((SSM_W, SSM_W))],
        out_specs=pl.BlockSpec((1, chunk, SSM_W), lambda b, c: (b, c, 0)),
        out_shape=jax.ShapeDtypeStruct((bsz, seq, SSM_W), BF16),
        scratch_shapes=[pltpu.VMEM((chunk, 2 * SSM_NS), F32), pltpu.VMEM((2, SSM_NS), F32)],
        compiler_params=_cparams(("parallel", "arbitrary")),
        name="ssm",
    )(p3, bbd, cbd, a_bar, d_skip, w_glu)


def _dsa_kernel(qi_ref, ws_ref, ki_ref, q_ref, k_ref, vt_ref, kn_ref, o_ref,
                key_ref, mask_ref, qit_ref, d0_ref, d1_ref, qt_ref, s0_ref, s1_ref, s2_ref, p_ref, acc_ref,
                *, tq, seq, ksel, fixed_bits):
    qb = pl.program_id(1)
    gid = pl.program_id(2)
    t0 = qb * tq
    cw = tq
    nc = qb + 1
    ca = 2 * tq
    nca = (qb + 2) // 2
    ks = 128
    vs = 64
    kf = float(ksel)
    jbits = (seq - 1).bit_length()
    qidx = t0 + lax.broadcasted_iota(I32, (1, tq), 1)

    @pl.when(gid == 0)
    def _select():
        qit = qi_ref[...].astype(F32).T
        qit_ref[...] = jnp.concatenate(
            [qit[h * IDX_DIM:(h + 1) * IDX_DIM, :] for h in range(N_IDX_HEADS)], axis=1).astype(BF16)
        wt = ws_ref[...].T
        wrow = jnp.concatenate([wt[IDX_DIM + h:IDX_DIM + h + 1, :] for h in range(N_IDX_HEADS)], axis=1)

        def idx_dots(c, d_ref):
            start = pl.multiple_of(c * cw, cw)
            kib = ki_ref[pl.ds(start, cw), :][:, :IDX_DIM]
            d_ref[...] = jnp.dot(kib, qit_ref[...], preferred_element_type=F32)

        def idx_scores(c, d_ref):
            for i in range(cw // ks):
                start = pl.multiple_of(c * cw, cw) + i * ks
                r = jnp.maximum(d_ref[i * ks:(i + 1) * ks, :], 0.0) * wrow
                isc = r[:, 0:tq]
                for h in range(1, N_IDX_HEADS):
                    isc = isc + r[:, h * tq:(h + 1) * tq]
                kidx = start + lax.broadcasted_iota(I32, (ks, tq), 0)
                isc = jnp.where(isc == 0.0, 0.0, isc)
                isc = jnp.where(kidx <= qidx, isc, -jnp.inf)
                bits = pltpu.bitcast(isc, I32)
                key_ref[pl.ds(start, ks), :] = bits ^ ((bits >> 31) & 0x7FFFFFFF)

        idx_dots(0, d0_ref)

        def idx_pair(c2, _):
            c = 2 * c2
            idx_dots(c + 1, d1_ref)
            idx_scores(c, d0_ref)
            idx_dots(jnp.minimum(c + 2, nc - 1), d0_ref)
            idx_scores(c + 1, d1_ref)
            return 0

        lax.fori_loop(0, nc // 2, idx_pair, 0)

        @pl.when(nc % 2 == 1)
        def _():
            idx_scores(nc - 1, d0_ref)

        @pl.when(qb % 2 == 0)
        def _():
            mask_ref[pl.ds(pl.multiple_of((qb + 1) * cw, cw), cw), :] = jnp.zeros((cw, tq), BF16)

        small = qidx < ksel

        def key_rows(c):
            return pl.multiple_of(c * cw, cw) + lax.broadcasted_iota(I32, (cw, tq), 0)

        def count(pred):
            def body(c, cnt):
                blk = key_ref[pl.ds(pl.multiple_of(c * cw, cw), cw), :]
                hit = pred(blk, c)
                return cnt + jnp.sum(hit.reshape(cw // 32, 32, tq), axis=0)
            cnt = lax.fori_loop(0, nc, body, jnp.zeros((32, tq), F32))
            return jnp.sum(cnt, axis=0, keepdims=True)

        def count_ge(cand):
            return count(lambda x, c: jnp.where(x >= cand, 1.0, 0.0))

        c0 = count_ge(jnp.zeros((1, tq), I32))
        thr = jnp.where(c0 >= kf, 0, INT_MIN).astype(I32)
        done = jnp.where(jnp.logical_or(small, c0 == kf), 1.0, 0.0)

        def bit_step(i, thr, done):
            cand = thr + jnp.left_shift(jnp.int32(1), 30 - i)
            cnt = count_ge(cand)
            return (jnp.where(cnt >= kf, cand, thr),
                    jnp.maximum(done, jnp.where(cnt == kf, 1.0, 0.0)))

        thr, done = lax.fori_loop(0, fixed_bits, lambda i, c: bit_step(i, *c), (thr, done))

        def w_cond(c):
            return jnp.logical_and(c[0] < 31, c[3] < 0.5)

        def w_body(c):
            thr, done = bit_step(c[0], c[1], c[2])
            thr, done = bit_step(c[0] + 1, thr, done)
            return c[0] + 2, thr, done, jnp.min(done)

        _, thr, done, settled = lax.while_loop(
            w_cond, w_body, (jnp.int32(fixed_bits), thr, done, jnp.min(done)))
        thr = jnp.where(small, INT_MIN, thr)

        def tie_index():
            n_ge = count_ge(thr)
            need = kf - count_ge(thr + 1)
            tie = jnp.logical_and(jnp.logical_not(small), n_ge > kf)

            def count_tie_below(cand):
                return count(lambda x, c: jnp.where(
                    x == thr, jnp.where(key_rows(c) < cand, 1.0, 0.0), 0.0))

            def jbit_body(i, jj):
                cand = jj + jnp.left_shift(jnp.int32(1), jbits - 1 - i)
                return jnp.where(count_tie_below(cand) < need, cand, jj)

            jj = lax.fori_loop(0, jbits, jbit_body, jnp.zeros((1, tq), I32))
            return jnp.where(tie, jj, seq)

        jmax = lax.cond(settled < 0.5, tie_index, lambda: jnp.full((1, tq), seq, I32))

        def mask_body(c, _):
            rows = pl.ds(pl.multiple_of(c * cw, cw), cw)
            x = key_ref[rows, :]
            kidx = key_rows(c)
            b = jnp.where(x > thr, 1.0, jnp.where(x == thr, jnp.where(kidx <= jmax, 1.0, 0.0), 0.0))
            mask_ref[rows, :] = jnp.where(kidx <= qidx, b, 0.0).astype(BF16)
            return 0

        lax.fori_loop(0, nc, mask_body, 0)

    qt = q_ref[...].astype(F32).T.astype(BF16)
    vrow = pl.multiple_of(gid * HEAD_DIM, HEAD_DIM)
    qt_ref[...] = jnp.zeros(qt_ref.shape, BF16)
    for r in range(REP):
        qt_ref[pl.ds(vrow, HEAD_DIM), r * tq:(r + 1) * tq] = qt[r * HEAD_DIM:(r + 1) * HEAD_DIM, :]
    acc_ref[...] = jnp.zeros(acc_ref.shape, F32)
    ones_rows = jnp.ones((ONES_ROWS, ca), BF16)

    qsq = qt.astype(F32) * qt.astype(F32)
    qn2 = jnp.concatenate([jnp.sum(qsq[r * HEAD_DIM:(r + 1) * HEAD_DIM, :], axis=0, keepdims=True)
                           for r in range(REP)], axis=1)
    kn2 = jnp.max(kn_ref[...], axis=(0, 1), keepdims=True)[0]
    kn2 = jnp.max(jnp.where(lax.broadcasted_iota(I32, (1, LANES), 1) == gid, kn2, 0.0), axis=1, keepdims=True)
    shift = jnp.sqrt(qn2) * jnp.sqrt(kn2)

    def scores(c, s_ref):
        start = pl.multiple_of(c * ca, ca)
        s_ref[...] = jnp.dot(k_ref[pl.ds(start, ca), :], qt_ref[...], preferred_element_type=F32)

    def accumulate(c, scale):
        vt = jnp.concatenate(
            [vt_ref[0, pl.ds(vrow, HEAD_DIM), pl.ds(pl.multiple_of(c * ca, ca), ca)], ones_rows], axis=0)
        acc_ref[...] = scale * acc_ref[...] + jnp.dot(vt, p_ref[...], preferred_element_type=F32)

    def mask_rows(c, i):
        return mask_ref[pl.ds(pl.multiple_of(c * ca, ca) + i * vs, vs), :]

    def softmax_chunk(c, s_ref):
        for i in range(ca // vs):
            p = jnp.exp2(s_ref[i * vs:(i + 1) * vs, :] - shift).astype(BF16)
            p_ref[i * vs:(i + 1) * vs, :] = p * jnp.tile(mask_rows(c, i), (1, REP))
        accumulate(c, 1.0)

    scores(0, s0_ref)

    def att_triple(c3, _):
        c = 3 * c3
        scores(c + 1, s1_ref)
        softmax_chunk(c, s0_ref)
        scores(c + 2, s2_ref)
        softmax_chunk(c + 1, s1_ref)
        scores(jnp.minimum(c + 3, nca - 1), s0_ref)
        softmax_chunk(c + 2, s2_ref)
        return 0

    lax.fori_loop(0, nca // 3, att_triple, 0)
    rem = nca % 3
    base = nca - rem

    @pl.when(rem >= 1)
    def _():
        scores(jnp.minimum(base + 1, nca - 1), s1_ref)
        softmax_chunk(base, s0_ref)

    @pl.when(rem == 2)
    def _():
        softmax_chunk(base + 1, s1_ref)

    @pl.when(jnp.min(acc_ref[HEAD_DIM:HEAD_DIM + 1, :]) < DENOM_FLOOR)
    def _():
        acc_ref[...] = jnp.zeros(acc_ref.shape, F32)

        def exact_chunk(c, m_prev):
            scores(c, s0_ref)
            m_new = m_prev
            for i in range(ca // vs):
                b = jnp.where(jnp.tile(mask_rows(c, i), (1, REP)) > 0, s0_ref[i * vs:(i + 1) * vs, :], NEG_BIG)
                s1_ref[i * vs:(i + 1) * vs, :] = b
                m_new = jnp.maximum(m_new, jnp.max(b, axis=0, keepdims=True))
            for i in range(ca // vs):
                p_ref[i * vs:(i + 1) * vs, :] = jnp.exp2(s1_ref[i * vs:(i + 1) * vs, :] - m_new).astype(BF16)
            accumulate(c, jnp.exp2(m_prev - m_new))
            return m_new

        lax.fori_loop(0, nca, exact_chunk, jnp.full((1, REP * tq), NEG_BIG, F32))

    acc = acc_ref[...]
    out_t = acc[:HEAD_DIM] / acc[HEAD_DIM:HEAD_DIM + 1]
    out_t = jnp.concatenate([out_t[:, r * tq:(r + 1) * tq] for r in range(REP)], axis=0)
    o_ref[...] = out_t.T.astype(o_ref.dtype)


def _dsa(qi_r, w_s, ki_r, q_r, k_r, v_t, k_n, bsz, seq):
    t = q_r.shape[0]
    tq = 256
    assert seq % (2 * tq) == 0
    ksel = min(TOPK_MAX, seq // 4)
    assert ksel <= tq
    nqb = seq // tq
    rowblk = lambda w: pl.BlockSpec((tq, w), lambda b, i, g: (b * nqb + i, 0))
    return pl.pallas_call(
        functools.partial(_dsa_kernel, tq=tq, seq=seq, ksel=ksel, fixed_bits=15),
        grid=(bsz, nqb, N_KV_HEADS),
        in_specs=[rowblk(N_IDX_HEADS * IDX_DIM), rowblk(LANES),
                  pl.BlockSpec((seq, LANES), lambda b, i, g: (b, 0)),
                  pl.BlockSpec((tq, REP * HEAD_DIM), lambda b, i, g: (b * nqb + i, g)),
                  pl.BlockSpec((seq, KV_W), lambda b, i, g: (b, 0)),
                  pl.BlockSpec((1, KV_W, seq), lambda b, i, g: (b, 0, 0)),
                  pl.BlockSpec((k_n.shape[0] // bsz, SUBLANES, LANES), lambda b, i, g: (b, 0, 0))],
        out_specs=pl.BlockSpec((tq, REP * HEAD_DIM), lambda b, i, g: (b * nqb + i, g)),
        out_shape=jax.ShapeDtypeStruct((t, ATT_W), BF16),
        scratch_shapes=[pltpu.VMEM((seq, tq), I32),
                        pltpu.VMEM((seq, tq), BF16),
                        pltpu.VMEM((IDX_DIM, N_IDX_HEADS * tq), BF16),
                        pltpu.VMEM((tq, N_IDX_HEADS * tq), F32),
                        pltpu.VMEM((tq, N_IDX_HEADS * tq), F32),
                        pltpu.VMEM((KV_W, REP * tq), BF16),
                        pltpu.VMEM((2 * tq, REP * tq), F32),
                        pltpu.VMEM((2 * tq, REP * tq), F32),
                        pltpu.VMEM((2 * tq, REP * tq), F32),
                        pltpu.VMEM((2 * tq, REP * tq), BF16),
                        pltpu.VMEM((HEAD_DIM + ONES_ROWS, REP * tq), F32)],
        compiler_params=_cparams(("parallel", "arbitrary", "arbitrary")),
        name="dsa",
    )(qi_r, w_s, ki_r, q_r, k_r, v_t, k_n)


def _causal_conv3(cur, halo, w):
    tm = cur.shape[0]
    ext = jnp.concatenate([halo, cur], axis=0)
    return (w[2:3, :] * cur + w[1:2, :] * ext[SUBLANES - 1:SUBLANES - 1 + tm]
            + w[0:1, :] * ext[SUBLANES - 2:SUBLANES - 2 + tm])


def _merge_kernel(x_ref, ga_ref, gb_ref, gc_ref, scx_ref, scb_ref, scc_ref, hx_ref, hc_ref,
                  yb_ref, yc_ref, cw_ref, wa_ref, wb_ref, wc_ref, wo_ref, o_ref, *, tm, seq):
    first = (pl.program_id(0) * tm) % seq == 0
    cx = scc_ref[...].astype(F32) * scx_ref[...].astype(F32)
    halo = hc_ref[...].astype(F32) * hx_ref[...].astype(F32)
    halo = jnp.where(first, 0.0, halo)
    ya = scb_ref[...].astype(F32) * _causal_conv3(cx, halo, cw_ref[...])
    dot = lambda a, w: jnp.dot(a, w[...], preferred_element_type=F32)
    m = jax.nn.sigmoid(ga_ref[...].astype(F32)) * dot(ya.astype(BF16), wa_ref)
    m = m + jax.nn.sigmoid(gb_ref[...].astype(F32)) * dot(yb_ref[...], wb_ref)
    m = m + jax.nn.sigmoid(gc_ref[...].astype(F32)) * dot(yc_ref[...], wc_ref)
    o_ref[...] = x_ref[...] + dot(m.astype(BF16), wo_ref)


def _merge(x2, p2, y_b, y_c, conv_w, w_a, w_b, w_c, w_o, seq):
    t, d = x2.shape
    tm = _pick(seq, (256, 128))
    hb = tm // SUBLANES

    def col(width, off):
        return pl.BlockSpec((tm, width), lambda i, o=off // width: (i, o))

    def halo(off):
        return pl.BlockSpec((SUBLANES, SC_W), lambda i, o=off // SC_W: (jnp.maximum(i * hb - 1, 0), o))

    def const(shape):
        return pl.BlockSpec(shape, lambda i: (0, 0), pipeline_mode=pl.Buffered(1))

    row = lambda w: pl.BlockSpec((tm, w), lambda i: (i, 0))
    return pl.pallas_call(
        functools.partial(_merge_kernel, tm=tm, seq=seq),
        grid=(t // tm,),
        in_specs=[row(d), col(d, 0), col(d, d), col(d, 2 * d),
                  col(SC_W, OFF_SCX), col(SC_W, OFF_SCB), col(SC_W, OFF_SCC),
                  halo(OFF_SCX), halo(OFF_SCC), row(SSM_W), row(ATT_W),
                  const((SC_CONV, SC_W)), const((SC_W, d)), const((SSM_W, d)),
                  const((ATT_W, d)), const((d, d))],
        out_specs=row(d),
        out_shape=jax.ShapeDtypeStruct((t, d), F32),
        compiler_params=_cparams(("parallel",)),
        name="merge",
    )(x2, p2, p2, p2, p2, p2, p2, p2, p2, y_b, y_c, conv_w, w_a, w_b, w_c, w_o)


def _ffn_kernel(x_ref, hx_ref, g_ref, wg_ref, wu_ref, cg_ref, cu_ref, wd_ref, og_ref, o_ref,
                h_ref, acc_ref, *, tm, seq, out_norm):
    j = pl.program_id(1)

    @pl.when(j == 0)
    def _():
        first = (pl.program_id(0) * tm) % seq == 0
        x = jnp.concatenate([hx_ref[...], x_ref[...]], axis=0)
        ms = jnp.mean(x * x, axis=-1, keepdims=True)
        h = x * lax.rsqrt(ms + NORM_EPS) * g_ref[...]
        rows = lax.broadcasted_iota(I32, (tm + SUBLANES, 1), 0)
        h = jnp.where(jnp.logical_and(first, rows < SUBLANES), 0.0, h)
        h_ref[...] = h.astype(BF16)
        acc_ref[...] = jnp.zeros_like(acc_ref)

    h = h_ref[...]
    gt = jnp.dot(h, wg_ref[...], preferred_element_type=F32)
    ut = jnp.dot(h, wu_ref[...], preferred_element_type=F32)
    gc = _causal_conv3(gt[SUBLANES:], gt[:SUBLANES], cg_ref[...])
    uc = _causal_conv3(ut[SUBLANES:], ut[:SUBLANES], cu_ref[...])
    act = (jax.nn.silu(gc) * uc).astype(BF16)
    acc_ref[...] += jnp.dot(act, wd_ref[...], preferred_element_type=F32)

    @pl.when(j == pl.num_programs(1) - 1)
    def _():
        y = x_ref[...] + acc_ref[...]
        if out_norm:
            ms = jnp.mean(y * y, axis=-1, keepdims=True)
            y = y * lax.rsqrt(ms + NORM_EPS) * og_ref[...]
        o_ref[...] = y


def _ffn(x2, gain, w_up, conv_w, w_down, out_gain, seq, out_norm):
    t, d = x2.shape
    dff = w_down.shape[0]
    tm = _pick(seq, (512, 256, 128))
    tf = _pick(dff, (512, 256, 128))
    nf = dff // tf
    hb = tm // SUBLANES
    return pl.pallas_call(
        functools.partial(_ffn_kernel, tm=tm, seq=seq, out_norm=out_norm),
        grid=(t // tm, nf),
        in_specs=[pl.BlockSpec((tm, d), lambda i, j: (i, 0)),
                  pl.BlockSpec((SUBLANES, d), lambda i, j: (jnp.maximum(i * hb - 1, 0), 0)),
                  pl.BlockSpec((1, d), lambda i, j: (0, 0)),
                  pl.BlockSpec((d, tf), lambda i, j: (0, j)),
                  pl.BlockSpec((d, tf), lambda i, j: (0, j + nf)),
                  pl.BlockSpec((FFN_CONV, tf), lambda i, j: (0, j)),
                  pl.BlockSpec((FFN_CONV, tf), lambda i, j: (0, j + nf)),
                  pl.BlockSpec((tf, d), lambda i, j: (j, 0)),
                  pl.BlockSpec((1, d), lambda i, j: (0, 0))],
        out_specs=pl.BlockSpec((tm, d), lambda i, j: (i, 0)),
        out_shape=jax.ShapeDtypeStruct((t, d), F32),
        scratch_shapes=[pltpu.VMEM((tm + SUBLANES, d), BF16), pltpu.VMEM((tm, d), F32)],
        compiler_params=_cparams(("parallel", "arbitrary")),
        name="ffn",
    )(x2, x2, gain, w_up, w_up, conv_w, conv_w, w_down, out_gain)


def _prep_w_in(w):
    d = w.shape[0]
    pad = jnp.zeros((d, NP_COLS - w.shape[1]), BF16)
    return jnp.concatenate([w[:, _ORIG_GATE_OFF:].astype(BF16), w[:, :_ORIG_GATE_OFF].astype(BF16), pad], axis=1)


def _prep_ssm(a_re, a_im, log_dt, b_re, b_im, c_re, c_im):
    ar, ai = a_re.astype(F32), a_im.astype(F32)
    dt = jnp.exp(log_dt.astype(F32))[:, None]
    mag = jnp.exp(dt * ar)
    abr, abi = mag * jnp.cos(dt * ai), mag * jnp.sin(dt * ai)
    den = ar * ar + ai * ai
    cr = ((abr - 1.0) * ar + abi * ai) / den
    ci = (abi * ar - (abr - 1.0) * ai) / den
    br, bi = b_re.astype(F32), b_im.astype(F32)
    bbr = cr[..., None] * br - ci[..., None] * bi
    bbi = cr[..., None] * bi + ci[..., None] * br
    eye = jnp.eye(SSM_GROUPS, dtype=F32)

    def bdiag_in(m):
        return (jnp.transpose(m, (0, 2, 1))[:, :, None, :] * eye[:, None, :, None]).reshape(
            SSM_W, SSM_NS)

    def bdiag_out(m):
        return (jnp.transpose(m, (0, 2, 1))[:, :, None, :] * eye[:, None, :, None]).reshape(
            SSM_NS, SSM_W)

    bbd = jnp.concatenate([bdiag_in(bbr), bdiag_in(bbi)], axis=1).astype(BF16)
    cbd = jnp.concatenate([bdiag_out(c_re.astype(F32)), bdiag_out(-c_im.astype(F32))], axis=0).astype(BF16)
    a_rows = jnp.stack([abr.reshape(-1), abi.reshape(-1)], axis=0)
    return bbd, cbd, a_rows


def kernel(x, positions, norm_mix, w_in, sc_conv, ssm_a_re, ssm_a_im, ssm_log_dt, ssm_b_re, ssm_b_im,
           ssm_c_re, ssm_c_im, ssm_d, ssm_glu, w_branch_a, w_branch_b, w_branch_c, w_out, norm_ffn,
           w_up, ffn_conv, w_down, norm_final):
    bsz, seq, d = x.shape
    depth = w_in.shape[0]
    t = bsz * seq
    x2 = x.reshape(t, d).astype(F32)
    tabs = _rope_tables(positions.astype(I32))
    for l in range(depth):
        p2 = _inproj(x2, norm_mix[l].reshape(1, d).astype(F32), _prep_w_in(w_in[l]))
        q_r, k_r, v_t, qi_r, ki_r, w_s, k_n = _rope(p2, tabs, bsz, seq)
        bbd, cbd, a_rows = _prep_ssm(ssm_a_re[l], ssm_a_im[l], ssm_log_dt[l], ssm_b_re[l], ssm_b_im[l],
                                     ssm_c_re[l], ssm_c_im[l])
        y_b = _ssm(p2.reshape(bsz, seq, NP_COLS), bbd, cbd, a_rows,
                   ssm_d[l].reshape(1, SSM_W).astype(F32), ssm_glu[l].astype(BF16))
        y_c = _dsa(qi_r, w_s, ki_r, q_r, k_r, v_t, k_n, bsz, seq)
        x2 = _merge(x2, p2, y_b.reshape(t, SSM_W), y_c, sc_conv[l].astype(F32),
                    w_branch_a[l].astype(BF16), w_branch_b[l].astype(BF16), w_branch_c[l].astype(BF16),
                    w_out[l].astype(BF16), seq)
        x2 = _ffn(x2, norm_ffn[l].reshape(1, d).astype(F32), w_up[l].astype(BF16),
                  ffn_conv[l].astype(F32), w_down[l].astype(BF16),
                  norm_final.reshape(1, d).astype(F32), seq, out_norm=(l == depth - 1))
    return x2.reshape(bsz, seq, d).astype(x.dtype)
```

```python
import functools
import math

import jax
import jax.numpy as jnp
from jax import lax
from jax.experimental import pallas as pl
from jax.experimental.pallas import tpu as pltpu

F32 = jnp.float32
BF16 = jnp.bfloat16
I32 = jnp.int32
I16 = jnp.int16

SC_W = 512
SC_CONV = 3
SSM_W = 512
SSM_GROUP = 16
SSM_GROUPS = SSM_W // SSM_GROUP
SSM_STATE = 64
SSM_NS = SSM_GROUPS * SSM_STATE
N_Q_HEADS = 16
N_KV_HEADS = 4
HEAD_DIM = 64
ATT_W = N_Q_HEADS * HEAD_DIM
KV_W = N_KV_HEADS * HEAD_DIM
REP = N_Q_HEADS // N_KV_HEADS
ROT_DIM = HEAD_DIM // 4
ROT_HALF = ROT_DIM // 2
ROPE_THETA = 500000.0
N_IDX_HEADS = 8
IDX_DIM = 64
TOPK_MAX = 256
D_FF = 5632
FFN_CONV = 3
NORM_EPS = 1e-6

LANES = 128
SUBLANES = 8
ONES_ROWS = 16
VMEM_LIMIT = 56 * 1024 * 1024
NEG_BIG = -1e30
LOG2E = math.log2(math.e)
DENOM_FLOOR = 2.0 ** -64
INT_MIN = -(2 ** 31)
HIGH_BITS = 15

_ORIG_GATE_OFF = 3 * SC_W + SSM_W + ATT_W + 2 * KV_W + N_IDX_HEADS * IDX_DIM + IDX_DIM + N_IDX_HEADS
OFF_SCX = 3 * 2048
OFF_SCB = OFF_SCX + SC_W
OFF_SCC = OFF_SCB + SC_W
OFF_SSM = OFF_SCC + SC_W
OFF_Q = OFF_SSM + SSM_W
OFF_K = OFF_Q + ATT_W
OFF_V = OFF_K + KV_W
OFF_QI = OFF_V + KV_W
OFF_KI = OFF_QI + N_IDX_HEADS * IDX_DIM
NP_COLS = OFF_KI + LANES


def _cparams(sem, vmem=VMEM_LIMIT):
    return pltpu.CompilerParams(dimension_semantics=sem, vmem_limit_bytes=vmem)


def _pick(n, prefs):
    for p in prefs:
        if n % p == 0:
            return p
    return n


def _inproj_kernel(x_ref, g_ref, w_ref, o_ref, h_ref):
    @pl.when(pl.program_id(1) == 0)
    def _():
        x = x_ref[...]
        ms = jnp.mean(x * x, axis=-1, keepdims=True)
        h_ref[...] = (x * lax.rsqrt(ms + NORM_EPS) * g_ref[...]).astype(BF16)

    o_ref[...] = jnp.dot(h_ref[...], w_ref[...], preferred_element_type=F32).astype(o_ref.dtype)


def _inproj(x2, gain, w):
    t, d = x2.shape
    n = w.shape[1]
    tm = _pick(t, (1024, 512, 256, 128))
    tn = _pick(n, (1152, 384, 128))
    return pl.pallas_call(
        _inproj_kernel,
        grid=(t // tm, n // tn),
        in_specs=[pl.BlockSpec((tm, d), lambda i, j: (i, 0)),
                  pl.BlockSpec((1, d), lambda i, j: (0, 0)),
                  pl.BlockSpec((d, tn), lambda i, j: (0, j))],
        out_specs=pl.BlockSpec((tm, tn), lambda i, j: (i, j)),
        out_shape=jax.ShapeDtypeStruct((t, n), BF16),
        scratch_shapes=[pltpu.VMEM((tm, d), BF16)],
        compiler_params=_cparams(("parallel", "arbitrary")),
        name="inproj",
    )(x2, gain, w)


def _rope_table_kernel(pos_ref, invf_ref, a_ref, bm_ref, bp_ref):
    ang = pos_ref[...].astype(F32) * invf_ref[...]
    c = jnp.cos(ang)
    s = jnp.sin(ang)
    r = lax.broadcasted_iota(I32, ang.shape, 1) % HEAD_DIM
    lo = r < ROT_HALF
    hi = jnp.logical_and(r >= ROT_HALF, r < ROT_DIM)
    a_ref[...] = jnp.where(r < ROT_DIM, c, 1.0)
    bm_ref[...] = jnp.where(lo, -s, 0.0)
    bp_ref[...] = jnp.where(hi, s, 0.0)


def _rope_tables(positions):
    t = positions.size
    posb = jnp.broadcast_to(positions.reshape(t, 1), (t, LANES))
    inv_freq = ROPE_THETA ** (-jnp.arange(0, ROT_DIM, 2, dtype=F32) / ROT_DIM)
    lane = jnp.arange(LANES) % ROT_HALF
    invf = inv_freq[lane].reshape(1, LANES)
    tm = _pick(t, (1024, 512, 256, 128))
    spec = pl.BlockSpec((tm, LANES), lambda i: (i, 0))
    return pl.pallas_call(
        _rope_table_kernel,
        grid=(t // tm,),
        in_specs=[spec, pl.BlockSpec((1, LANES), lambda i: (0, 0))],
        out_specs=[spec, spec, spec],
        out_shape=[jax.ShapeDtypeStruct((t, LANES), F32)] * 3,
        compiler_params=_cparams(("parallel",)),
        name="rope_tables",
    )(posb, invf)


def _rope_kernel(q_ref, k_ref, v_ref, qi_ref, kiw_ref, a_ref, bm_ref, bp_ref,
                 qo_ref, ko_ref, vto_ref, qio_ref, kio_ref, wo_ref, kn_ref):
    a = a_ref[...]
    bm = bm_ref[...]
    bp = bp_ref[...]

    def rope(x):
        w = x.shape[1]
        reps = w // LANES
        xf = x.astype(F32)
        up = pltpu.roll(xf, w - ROT_HALF, 1)
        dn = pltpu.roll(xf, ROT_HALF, 1)
        return (xf * jnp.tile(a, (1, reps)) + up * jnp.tile(bm, (1, reps))
                + dn * jnp.tile(bp, (1, reps)))

    qo_ref[...] = (rope(q_ref[...]) * (HEAD_DIM ** -0.5 * LOG2E)).astype(BF16)
    kr = rope(k_ref[...]).astype(BF16)
    ko_ref[...] = kr
    ksq = kr.astype(F32) * kr.astype(F32)
    lane = lax.broadcasted_iota(I32, (1, LANES), 1)
    kn = jnp.zeros((1, LANES), F32)
    for g in range(N_KV_HEADS):
        n2 = jnp.sum(ksq[:, g * HEAD_DIM:(g + 1) * HEAD_DIM], axis=1, keepdims=True)
        kn = jnp.where(lane == g, jnp.max(n2, axis=0, keepdims=True), kn)
    kn_ref[0] = jnp.broadcast_to(kn, (SUBLANES, LANES))
    vto_ref[0] = v_ref[...].astype(F32).T.astype(BF16)
    qio_ref[...] = (rope(qi_ref[...]) * (IDX_DIM ** -0.5)).astype(BF16)
    kiw = kiw_ref[...]
    kio_ref[...] = rope(kiw).astype(BF16)
    wo_ref[...] = kiw.astype(F32) * (N_IDX_HEADS ** -0.5)


def _rope(p2, tabs, bsz, seq):
    t = p2.shape[0]
    tm = _pick(seq, (512, 256, 128))
    nsb = seq // tm
    a, bm, bp = tabs

    def col(width, off):
        return pl.BlockSpec((tm, width), lambda i, o=off // width: (i, o))

    tab = pl.BlockSpec((tm, LANES), lambda i: (i, 0))
    row = lambda w: pl.BlockSpec((tm, w), lambda i: (i, 0))
    return pl.pallas_call(
        _rope_kernel,
        grid=(t // tm,),
        in_specs=[col(ATT_W, OFF_Q), col(KV_W, OFF_K), col(KV_W, OFF_V),
                  col(N_IDX_HEADS * IDX_DIM, OFF_QI), col(LANES, OFF_KI), tab, tab, tab],
        out_specs=[row(ATT_W), row(KV_W),
                   pl.BlockSpec((1, KV_W, tm), lambda i: (i // nsb, 0, i % nsb)),
                   row(N_IDX_HEADS * IDX_DIM), row(LANES), row(LANES),
                   pl.BlockSpec((1, SUBLANES, LANES), lambda i: (i, 0, 0))],
        out_shape=[jax.ShapeDtypeStruct((t, ATT_W), BF16),
                   jax.ShapeDtypeStruct((t, KV_W), BF16),
                   jax.ShapeDtypeStruct((bsz, KV_W, seq), BF16),
                   jax.ShapeDtypeStruct((t, N_IDX_HEADS * IDX_DIM), BF16),
                   jax.ShapeDtypeStruct((t, LANES), BF16),
                   jax.ShapeDtypeStruct((t, LANES), F32),
                   jax.ShapeDtypeStruct((t // tm, SUBLANES, LANES), F32)],
        compiler_params=_cparams(("parallel",)),
        name="rope",
    )(p2, p2, p2, p2, p2, a, bm, bp)


def _ssm_kernel(u_ref, bbd_ref, cbd_ref, a_ref, d_ref, wg_ref, o_ref, bu_ref, st_ref, *, chunk):
    @pl.when(pl.program_id(1) == 0)
    def _():
        st_ref[...] = jnp.zeros_like(st_ref)

    u = u_ref[0]
    bu_ref[...] = jnp.dot(u, bbd_ref[...], preferred_element_type=F32)
    ar = a_ref[0:1, :]
    ai = a_ref[1:2, :]

    def step(t, carry):
        xr, xi = carry
        br = bu_ref[pl.ds(t, 1), 0:SSM_NS]
        bi = bu_ref[pl.ds(t, 1), SSM_NS:2 * SSM_NS]
        nr = ar * xr - ai * xi + br
        ni = ar * xi + ai * xr + bi
        bu_ref[pl.ds(t, 1), 0:SSM_NS] = nr
        bu_ref[pl.ds(t, 1), SSM_NS:2 * SSM_NS] = ni
        return nr, ni

    xr, xi = lax.fori_loop(0, chunk, step, (st_ref[0:1, :], st_ref[1:2, :]), unroll=8)
    st_ref[0:1, :] = xr
    st_ref[1:2, :] = xi

    y = jnp.dot(bu_ref[...].astype(BF16), cbd_ref[...], preferred_element_type=F32)
    y = y + d_ref[...] * u.astype(F32)
    z = jax.nn.gelu(y)
    gate = jnp.dot(z.astype(BF16), wg_ref[...], preferred_element_type=F32)
    o_ref[0] = (z * jax.nn.sigmoid(gate)).astype(o_ref.dtype)


def _ssm(p3, bbd, cbd, a_bar, d_skip, w_glu):
    bsz, seq, _ = p3.shape
    chunk = _pick(seq, (256, 128))
    const = lambda shape: pl.BlockSpec(shape, lambda b, c: (0,) * len(shape))
    return pl.pallas_call(
        functools.partial(_ssm_kernel, chunk=chunk),
        grid=(bsz, seq // chunk),
        in_specs=[pl.BlockSpec((1, chunk, SSM_W), lambda b, c: (b, c, OFF_SSM // SSM_W)),
                  const((SSM_W, 2 * SSM_NS)), const((2 * SSM_NS, SSM_W)),
                  const((2, SSM_NS)), const((1, SSM_W)), const((SSM_W, SSM_W))],
        out_specs=pl.BlockSpec((1, chunk, SSM_W), lambda b, c: (b, c, 0)),
        out_shape=jax.ShapeDtypeStruct((bsz, seq, SSM_W), BF16),
        scratch_shapes=[pltpu.VMEM((chunk, 2 * SSM_NS), F32), pltpu.VMEM((2, SSM_NS), F32)],
        compiler_params=_cparams(("parallel", "arbitrary")),
        name="ssm",
    )(p3, bbd, cbd, a_bar, d_skip, w_glu)


def _dsa_kernel(qi_ref, ws_ref, ki_ref, q_ref, k_ref, vt_ref, kn_ref, o_ref,
                key_ref, hkey_ref, mask_ref, qit_ref, d0_ref, d1_ref, qt_ref, s0_ref, s1_ref, s2_ref, p_ref, acc_ref,
                *, tq, seq, ksel):
    qb = pl.program_id(1)
    gid = pl.program_id(2)
    t0 = qb * tq
    cw = tq
    nc = qb + 1
    ca = 2 * tq
    nca = (qb + 2) // 2
    ks = 128
    vs = 64
    kf = float(ksel)
    jbits = (seq - 1).bit_length()
    qidx = t0 + lax.broadcasted_iota(I32, (1, tq), 1)

    @pl.when(gid == 0)
    def _select():
        qit = qi_ref[...].astype(F32).T
        qit_ref[...] = jnp.concatenate(
            [qit[h * IDX_DIM:(h + 1) * IDX_DIM, :] for h in range(N_IDX_HEADS)], axis=1).astype(BF16)
        wt = ws_ref[...].T
        wrow = jnp.concatenate([wt[IDX_DIM + h:IDX_DIM + h + 1, :] for h in range(N_IDX_HEADS)], axis=1)

        def idx_dots(c, d_ref):
            start = pl.multiple_of(c * cw, cw)
            kib = ki_ref[pl.ds(start, cw), :][:, :IDX_DIM]
            d_ref[...] = jnp.dot(kib, qit_ref[...], preferred_element_type=F32)

        def idx_scores(c, d_ref):
            for i in range(cw // ks):
                start = pl.multiple_of(c * cw, cw) + i * ks
                r = jnp.maximum(d_ref[i * ks:(i + 1) * ks, :], 0.0) * wrow
                isc = r[:, 0:tq]
                for h in range(1, N_IDX_HEADS):
                    isc = isc + r[:, h * tq:(h + 1) * tq]
                kidx = start + lax.broadcasted_iota(I32, (ks, tq), 0)
                isc = jnp.where(isc == 0.0, 0.0, isc)
                isc = jnp.where(kidx <= qidx, isc, -jnp.inf)
                bits = pltpu.bitcast(isc, I32)
                key = bits ^ ((bits >> 31) & 0x7FFFFFFF)
                key_ref[pl.ds(start, ks), :] = key
                hkey_ref[pl.ds(start, ks), :] = (key >> 16).astype(I16)

        idx_dots(0, d0_ref)

        def idx_pair(c2, _):
            c = 2 * c2
            idx_dots(c + 1, d1_ref)
            idx_scores(c, d0_ref)
            idx_dots(jnp.minimum(c + 2, nc - 1), d0_ref)
            idx_scores(c + 1, d1_ref)
            return 0

        lax.fori_loop(0, nc // 2, idx_pair, 0)

        @pl.when(nc % 2 == 1)
        def _():
            idx_scores(nc - 1, d0_ref)

        @pl.when(qb % 2 == 0)
        def _():
            mask_ref[pl.ds(pl.multiple_of((qb + 1) * cw, cw), cw), :] = jnp.zeros((cw, tq), BF16)

        small = qidx < ksel

        def key_rows(c):
            return pl.multiple_of(c * cw, cw) + lax.broadcasted_iota(I32, (cw, tq), 0)

        def count(pred):
            def body(c, cnt):
                blk = key_ref[pl.ds(pl.multiple_of(c * cw, cw), cw), :]
                hit = pred(blk, c)
                return cnt + jnp.sum(hit.reshape(cw // 32, 32, tq), axis=0)
            cnt = lax.fori_loop(0, nc, body, jnp.zeros((32, tq), F32))
            return jnp.sum(cnt, axis=0, keepdims=True)

        def count_ge(cand):
            return count(lambda x, c: jnp.where(x >= cand, 1.0, 0.0))

        def count_high_ge(cand):
            c16 = (cand >> 16).astype(I16)
            def body(c, cnt):
                blk = hkey_ref[pl.ds(pl.multiple_of(c * cw, cw), cw), :]
                hit = jnp.where(blk >= c16, jnp.ones((), I16), jnp.zeros((), I16))
                for r in range(0, cw, 32):
                    cnt = cnt + hit[r:r + 32, :]
                return cnt
            cnt = lax.fori_loop(0, nc, body, jnp.zeros((32, tq), I16))
            return jnp.sum(cnt.astype(F32), axis=0, keepdims=True)

        c0 = count_high_ge(jnp.zeros((1, tq), I32))
        thr = jnp.where(c0 >= kf, 0, INT_MIN).astype(I32)
        done = jnp.where(jnp.logical_or(small, c0 == kf), 1.0, 0.0)

        def bit_step(i, thr, done, counter):
            cand = thr + jnp.left_shift(jnp.int32(1), 30 - i)
            cnt = counter(cand)
            return (jnp.where(cnt >= kf, cand, thr),
                    jnp.maximum(done, jnp.where(cnt == kf, 1.0, 0.0)))

        thr, done = lax.fori_loop(0, HIGH_BITS, lambda i, c: bit_step(i, *c, count_high_ge), (thr, done))

        def w_cond(c):
            return jnp.logical_and(c[0] < 31, c[3] < 0.5)

        def w_body(c):
            thr, done = bit_step(c[0], c[1], c[2], count_ge)
            thr, done = bit_step(c[0] + 1, thr, done, count_ge)
            return c[0] + 2, thr, done, jnp.min(done)

        _, thr, done, settled = lax.while_loop(
            w_cond, w_body, (jnp.int32(HIGH_BITS), thr, done, jnp.min(done)))
        thr = jnp.where(small, INT_MIN, thr)

        def tie_index():
            n_ge = count_ge(thr)
            need = kf - count_ge(thr + 1)
            tie = jnp.logical_and(jnp.logical_not(small), n_ge > kf)

            def count_tie_below(cand):
                return count(lambda x, c: jnp.where(
                    x == thr, jnp.where(key_rows(c) < cand, 1.0, 0.0), 0.0))

            def jbit_body(i, jj):
                cand = jj + jnp.left_shift(jnp.int32(1), jbits - 1 - i)
                return jnp.where(count_tie_below(cand) < need, cand, jj)

            jj = lax.fori_loop(0, jbits, jbit_body, jnp.zeros((1, tq), I32))
            return jnp.where(tie, jj, seq)

        jmax = lax.cond(settled < 0.5, tie_index, lambda: jnp.full((1, tq), seq, I32))

        def mask_body(c, _):
            rows = pl.ds(pl.multiple_of(c * cw, cw), cw)
            x = key_ref[rows, :]
            kidx = key_rows(c)
            b = jnp.where(x > thr, 1.0, jnp.where(x == thr, jnp.where(kidx <= jmax, 1.0, 0.0), 0.0))
            mask_ref[rows, :] = jnp.where(kidx <= qidx, b, 0.0).astype(BF16)
            return 0

        lax.fori_loop(0, nc, mask_body, 0)

    qt = q_ref[...].astype(F32).T.astype(BF16)
    vrow = pl.multiple_of(gid * HEAD_DIM, HEAD_DIM)
    qt_ref[...] = jnp.zeros(qt_ref.shape, BF16)
    for r in range(REP):
        qt_ref[pl.ds(vrow, HEAD_DIM), r * tq:(r + 1) * tq] = qt[r * HEAD_DIM:(r + 1) * HEAD_DIM, :]
    acc_ref[...] = jnp.zeros(acc_ref.shape, F32)
    ones_rows = jnp.ones((ONES_ROWS, ca), BF16)

    qsq = qt.astype(F32) * qt.astype(F32)
    qn2 = jnp.concatenate([jnp.sum(qsq[r * HEAD_DIM:(r + 1) * HEAD_DIM, :], axis=0, keepdims=True)
                           for r in range(REP)], axis=1)
    kn2 = jnp.max(kn_ref[...], axis=(0, 1), keepdims=True)[0]
    kn2 = jnp.max(jnp.where(lax.broadcasted_iota(I32, (1, LANES), 1) == gid, kn2, 0.0), axis=1, keepdims=True)
    shift = jnp.sqrt(qn2) * jnp.sqrt(kn2)

    def scores(c, s_ref):
        start = pl.multiple_of(c * ca, ca)
        s_ref[...] = jnp.dot(k_ref[pl.ds(start, ca), :], qt_ref[...], preferred_element_type=F32)

    def accumulate(c, scale):
        vt = jnp.concatenate(
            [vt_ref[0, pl.ds(vrow, HEAD_DIM), pl.ds(pl.multiple_of(c * ca, ca), ca)], ones_rows], axis=0)
        acc_ref[...] = scale * acc_ref[...] + jnp.dot(vt, p_ref[...], preferred_element_type=F32)

    def mask_rows(c, i):
        return mask_ref[pl.ds(pl.multiple_of(c * ca, ca) + i * vs, vs), :]

    def softmax_chunk(c, s_ref):
        for i in range(ca // vs):
            p = jnp.exp2(s_ref[i * vs:(i + 1) * vs, :] - shift).astype(BF16)
            p_ref[i * vs:(i + 1) * vs, :] = p * jnp.tile(mask_rows(c, i), (1, REP))
        accumulate(c, 1.0)

    scores(0, s0_ref)

    def att_triple(c3, _):
        c = 3 * c3
        scores(c + 1, s1_ref)
        softmax_chunk(c, s0_ref)
        scores(c + 2, s2_ref)
        softmax_chunk(c + 1, s1_ref)
        scores(jnp.minimum(c + 3, nca - 1), s0_ref)
        softmax_chunk(c + 2, s2_ref)
        return 0

    lax.fori_loop(0, nca // 3, att_triple, 0)
    rem = nca % 3
    base = nca - rem

    @pl.when(rem >= 1)
    def _():
        scores(jnp.minimum(base + 1, nca - 1), s1_ref)
        softmax_chunk(base, s0_ref)

    @pl.when(rem == 2)
    def _():
        softmax_chunk(base + 1, s1_ref)

    @pl.when(jnp.min(acc_ref[HEAD_DIM:HEAD_DIM + 1, :]) < DENOM_FLOOR)
    def _():
        acc_ref[...] = jnp.zeros(acc_ref.shape, F32)

        def exact_chunk(c, m_prev):
            scores(c, s0_ref)
            m_new = m_prev
            for i in range(ca // vs):
                b = jnp.where(jnp.tile(mask_rows(c, i), (1, REP)) > 0, s0_ref[i * vs:(i + 1) * vs, :], NEG_BIG)
                s1_ref[i * vs:(i + 1) * vs, :] = b
                m_new = jnp.maximum(m_new, jnp.max(b, axis=0, keepdims=True))
            for i in range(ca // vs):
                p_ref[i * vs:(i + 1) * vs, :] = jnp.exp2(s1_ref[i * vs:(i + 1) * vs, :] - m_new).astype(BF16)
            accumulate(c, jnp.exp2(m_prev - m_new))
            return m_new

        lax.fori_loop(0, nca, exact_chunk, jnp.full((1, REP * tq), NEG_BIG, F32))

    acc = acc_ref[...]
    out_t = acc[:HEAD_DIM] / acc[HEAD_DIM:HEAD_DIM + 1]
    out_t = jnp.concatenate([out_t[:, r * tq:(r + 1) * tq] for r in range(REP)], axis=0)
    o_ref[...] = out_t.T.astype(o_ref.dtype)


def _dsa(qi_r, w_s, ki_r, q_r, k_r, v_t, k_n, bsz, seq):
    t = q_r.shape[0]
    tq = 256
    assert seq % (2 * tq) == 0
    ksel = min(TOPK_MAX, seq // 4)
    assert ksel <= tq
    nqb = seq // tq
    rowblk = lambda w: pl.BlockSpec((tq, w), lambda b, i, g: (b * nqb + i, 0))
    return pl.pallas_call(
        functools.partial(_dsa_kernel, tq=tq, seq=seq, ksel=ksel),
        grid=(bsz, nqb, N_KV_HEADS),
        in_specs=[rowblk(N_IDX_HEADS * IDX_DIM), rowblk(LANES),
                  pl.BlockSpec((seq, LANES), lambda b, i, g: (b, 0)),
                  pl.BlockSpec((tq, REP * HEAD_DIM), lambda b, i, g: (b * nqb + i, g)),
                  pl.BlockSpec((seq, KV_W), lambda b, i, g: (b, 0)),
                  pl.BlockSpec((1, KV_W, seq), lambda b, i, g: (b, 0, 0)),
                  pl.BlockSpec((k_n.shape[0] // bsz, SUBLANES, LANES), lambda b, i, g: (b, 0, 0))],
        out_specs=pl.BlockSpec((tq, REP * HEAD_DIM), lambda b, i, g: (b * nqb + i, g)),
        out_shape=jax.ShapeDtypeStruct((t, ATT_W), BF16),
        scratch_shapes=[pltpu.VMEM((seq, tq), I32),
                        pltpu.VMEM((seq, tq), I16),
                        pltpu.VMEM((seq, tq), BF16),
                        pltpu.VMEM((IDX_DIM, N_IDX_HEADS * tq), BF16),
                        pltpu.VMEM((tq, N_IDX_HEADS * tq), F32),
                        pltpu.VMEM((tq, N_IDX_HEADS * tq), F32),
                        pltpu.VMEM((KV_W, REP * tq), BF16),
                        pltpu.VMEM((2 * tq, REP * tq), F32),
                        pltpu.VMEM((2 * tq, REP * tq), F32),
                        pltpu.VMEM((2 * tq, REP * tq), F32),
                        pltpu.VMEM((2 * tq, REP * tq), BF16),
                        pltpu.VMEM((HEAD_DIM + ONES_ROWS, REP * tq), F32)],
        compiler_params=_cparams(("parallel", "arbitrary", "arbitrary")),
        name="dsa",
    )(qi_r, w_s, ki_r, q_r, k_r, v_t, k_n)


def _causal_conv3(cur, halo, w):
    tm = cur.shape[0]
    ext = jnp.concatenate([halo, cur], axis=0)
    return (w[2:3, :] * cur + w[1:2, :] * ext[SUBLANES - 1:SUBLANES - 1 + tm]
            + w[0:1, :] * ext[SUBLANES - 2:SUBLANES - 2 + tm])


def _merge_kernel(x_ref, ga_ref, gb_ref, gc_ref, scx_ref, scb_ref, scc_ref, hx_ref, hc_ref,
                  yb_ref, yc_ref, cw_ref, wa_ref, wb_ref, wc_ref, wo_ref, o_ref, *, tm, seq):
    first = (pl.program_id(0) * tm) % seq == 0
    cx = scc_ref[...].astype(F32) * scx_ref[...].astype(F32)
    halo = hc_ref[...].astype(F32) * hx_ref[...].astype(F32)
    halo = jnp.where(first, 0.0, halo)
    ya = scb_ref[...].astype(F32) * _causal_conv3(cx, halo, cw_ref[...])
    dot = lambda a, w: jnp.dot(a, w[...], preferred_element_type=F32)
    m = jax.nn.sigmoid(ga_ref[...].astype(F32)) * dot(ya.astype(BF16), wa_ref)
    m = m + jax.nn.sigmoid(gb_ref[...].astype(F32)) * dot(yb_ref[...], wb_ref)
    m = m + jax.nn.sigmoid(gc_ref[...].astype(F32)) * dot(yc_ref[...], wc_ref)
    o_ref[...] = x_ref[...] + dot(m.astype(BF16), wo_ref)


def _merge(x2, p2, y_b, y_c, conv_w, w_a, w_b, w_c, w_o, seq):
    t, d = x2.shape
    tm = _pick(seq, (256, 128))
    hb = tm // SUBLANES

    def col(width, off):
        return pl.BlockSpec((tm, width), lambda i, o=off // width: (i, o))

    def halo(off):
        return pl.BlockSpec((SUBLANES, SC_W), lambda i, o=off // SC_W: (jnp.maximum(i * hb - 1, 0), o))

    def const(shape):
        return pl.BlockSpec(shape, lambda i: (0, 0), pipeline_mode=pl.Buffered(1))

    row = lambda w: pl.BlockSpec((tm, w), lambda i: (i, 0))
    return pl.pallas_call(
        functools.partial(_merge_kernel, tm=tm, seq=seq),
        grid=(t // tm,),
        in_specs=[row(d), col(d, 0), col(d, d), col(d, 2 * d),
                  col(SC_W, OFF_SCX), col(SC_W, OFF_SCB), col(SC_W, OFF_SCC),
                  halo(OFF_SCX), halo(OFF_SCC), row(SSM_W), row(ATT_W),
                  const((SC_CONV, SC_W)), const((SC_W, d)), const((SSM_W, d)),
                  const((ATT_W, d)), const((d, d))],
        out_specs=row(d),
        out_shape=jax.ShapeDtypeStruct((t, d), F32),
        compiler_params=_cparams(("parallel",)),
        name="merge",
    )(x2, p2, p2, p2, p2, p2, p2, p2, p2, y_b, y_c, conv_w, w_a, w_b, w_c, w_o)


def _ffn_kernel(x_ref, hx_ref, g_ref, wg_ref, wu_ref, cg_ref, cu_ref, wd_ref, og_ref, o_ref,
                h_ref, acc_ref, *, tm, seq, out_norm):
    j = pl.program_id(1)

    @pl.when(j == 0)
    def _():
        first = (pl.program_id(0) * tm) % seq == 0
        x = jnp.concatenate([hx_ref[...], x_ref[...]], axis=0)
        ms = jnp.mean(x * x, axis=-1, keepdims=True)
        h = x * lax.rsqrt(ms + NORM_EPS) * g_ref[...]
        rows = lax.broadcasted_iota(I32, (tm + SUBLANES, 1), 0)
        h = jnp.where(jnp.logical_and(first, rows < SUBLANES), 0.0, h)
        h_ref[...] = h.astype(BF16)
        acc_ref[...] = jnp.zeros_like(acc_ref)

    h = h_ref[...]
    gt = jnp.dot(h, wg_ref[...], preferred_element_type=F32)
    ut = jnp.dot(h, wu_ref[...], preferred_element_type=F32)
    gc = _causal_conv3(gt[SUBLANES:], gt[:SUBLANES], cg_ref[...])
    uc = _causal_conv3(ut[SUBLANES:], ut[:SUBLANES], cu_ref[...])
    act = (jax.nn.silu(gc) * uc).astype(BF16)
    acc_ref[...] += jnp.dot(act, wd_ref[...], preferred_element_type=F32)

    @pl.when(j == pl.num_programs(1) - 1)
    def _():
        y = x_ref[...] + acc_ref[...]
        if out_norm:
            ms = jnp.mean(y * y, axis=-1, keepdims=True)
            y = y * lax.rsqrt(ms + NORM_EPS) * og_ref[...]
        o_ref[...] = y


def _ffn(x2, gain, w_up, conv_w, w_down, out_gain, seq, out_norm):
    t, d = x2.shape
    dff = w_down.shape[0]
    tm = _pick(seq, (512, 256, 128))
    tf = _pick(dff, (512, 256, 128))
    nf = dff // tf
    hb = tm // SUBLANES
    return pl.pallas_call(
        functools.partial(_ffn_kernel, tm=tm, seq=seq, out_norm=out_norm),
        grid=(t // tm, nf),
        in_specs=[pl.BlockSpec((tm, d), lambda i, j: (i, 0)),
                  pl.BlockSpec((SUBLANES, d), lambda i, j: (jnp.maximum(i * hb - 1, 0), 0)),
                  pl.BlockSpec((1, d), lambda i, j: (0, 0)),
                  pl.BlockSpec((d, tf), lambda i, j: (0, j)),
                  pl.BlockSpec((d, tf), lambda i, j: (0, j + nf)),
                  pl.BlockSpec((FFN_CONV, tf), lambda i, j: (0, j)),
                  pl.BlockSpec((FFN_CONV, tf), lambda i, j: (0, j + nf)),
                  pl.BlockSpec((tf, d), lambda i, j: (j, 0)),
                  pl.BlockSpec((1, d), lambda i, j: (0, 0))],
        out_specs=pl.BlockSpec((tm, d), lambda i, j: (i, 0)),
        out_shape=jax.ShapeDtypeStruct((t, d), F32),
        scratch_shapes=[pltpu.VMEM((tm + SUBLANES, d), BF16), pltpu.VMEM((tm, d), F32)],
        compiler_params=_cparams(("parallel", "arbitrary")),
        name="ffn",
    )(x2, x2, gain, w_up, w_up, conv_w, conv_w, w_down, out_gain)


def _prep_w_in(w):
    d = w.shape[0]
    pad = jnp.zeros((d, NP_COLS - w.shape[1]), BF16)
    return jnp.concatenate([w[:, _ORIG_GATE_OFF:].astype(BF16), w[:, :_ORIG_GATE_OFF].astype(BF16), pad], axis=1)


def _prep_ssm(a_re, a_im, log_dt, b_re, b_im, c_re, c_im):
    ar, ai = a_re.astype(F32), a_im.astype(F32)
    dt = jnp.exp(log_dt.astype(F32))[:, None]
    mag = jnp.exp(dt * ar)
    abr, abi = mag * jnp.cos(dt * ai), mag * jnp.sin(dt * ai)
    den = ar * ar + ai * ai
    cr = ((abr - 1.0) * ar + abi * ai) / den
    ci = (abi * ar - (abr - 1.0) * ai) / den
    br, bi = b_re.astype(F32), b_im.astype(F32)
    bbr = cr[..., None] * br - ci[..., None] * bi
    bbi = cr[..., None] * bi + ci[..., None] * br
    eye = jnp.eye(SSM_GROUPS, dtype=F32)

    def bdiag_in(m):
        return (jnp.transpose(m, (0, 2, 1))[:, :, None, :] * eye[:, None, :, None]).reshape(
            SSM_W, SSM_NS)

    def bdiag_out(m):
        return (jnp.transpose(m, (0, 2, 1))[:, :, None, :] * eye[:, None, :, None]).reshape(
            SSM_NS, SSM_W)

    bbd = jnp.concatenate([bdiag_in(bbr), bdiag_in(bbi)], axis=1).astype(BF16)
    cbd = jnp.concatenate([bdiag_out(c_re.astype(F32)), bdiag_out(-c_im.astype(F32))], axis=0).astype(BF16)
    a_rows = jnp.stack([abr.reshape(-1), abi.reshape(-1)], axis=0)
    return bbd, cbd, a_rows


def kernel(x, positions, norm_mix, w_in, sc_conv, ssm_a_re, ssm_a_im, ssm_log_dt, ssm_b_re, ssm_b_im,
           ssm_c_re, ssm_c_im, ssm_d, ssm_glu, w_branch_a, w_branch_b, w_branch_c, w_out, norm_ffn,
           w_up, ffn_conv, w_down, norm_final):
    bsz, seq, d = x.shape
    depth = w_in.shape[0]
    t = bsz * seq
    x2 = x.reshape(t, d).astype(F32)
    tabs = _rope_tables(positions.astype(I32))
    for l in range(depth):
        p2 = _inproj(x2, norm_mix[l].reshape(1, d).astype(F32), _prep_w_in(w_in[l]))
        q_r, k_r, v_t, qi_r, ki_r, w_s, k_n = _rope(p2, tabs, bsz, seq)
        bbd, cbd, a_rows = _prep_ssm(ssm_a_re[l], ssm_a_im[l], ssm_log_dt[l], ssm_b_re[l], ssm_b_im[l],
                                     ssm_c_re[l], ssm_c_im[l])
        y_b = _ssm(p2.reshape(bsz, seq, NP_COLS), bbd, cbd, a_rows,
                   ssm_d[l].reshape(1, SSM_W).astype(F32), ssm_glu[l].astype(BF16))
        y_c = _dsa(qi_r, w_s, ki_r, q_r, k_r, v_t, k_n, bsz, seq)
        x2 = _merge(x2, p2, y_b.reshape(t, SSM_W), y_c, sc_conv[l].astype(F32),
                    w_branch_a[l].astype(BF16), w_branch_b[l].astype(BF16), w_branch_c[l].astype(BF16),
                    w_out[l].astype(BF16), seq)
        x2 = _ffn(x2, norm_ffn[l].reshape(1, d).astype(F32), w_up[l].astype(BF16),
                  ffn_conv[l].astype(F32), w_down[l].astype(BF16),
                  norm_final.reshape(1, d).astype(F32), seq, out_norm=(l == depth - 1))
    return x2.reshape(bsz, seq, d).astype(x.dtype)
```

```python
import functools
import math

import jax
import jax.numpy as jnp
from jax import lax
from jax.experimental import pallas as pl
from jax.experimental.pallas import tpu as pltpu

F32 = jnp.float32
BF16 = jnp.bfloat16
I32 = jnp.int32
I16 = jnp.int16

SC_W = 512
SC_CONV = 3
SSM_W = 512
SSM_GROUP = 16
SSM_GROUPS = SSM_W // SSM_GROUP
SSM_STATE = 64
SSM_NS = SSM_GROUPS * SSM_STATE
N_Q_HEADS = 16
N_KV_HEADS = 4
HEAD_DIM = 64
ATT_W = N_Q_HEADS * HEAD_DIM
KV_W = N_KV_HEADS * HEAD_DIM
REP = N_Q_HEADS // N_KV_HEADS
ROT_DIM = HEAD_DIM // 4
ROT_HALF = ROT_DIM // 2
ROPE_THETA = 500000.0
N_IDX_HEADS = 8
IDX_DIM = 64
TOPK_MAX = 256
D_FF = 5632
FFN_CONV = 3
NORM_EPS = 1e-6

LANES = 128
SUBLANES = 8
ONES_ROWS = 16
VMEM_LIMIT = 56 * 1024 * 1024
NEG_BIG = -1e30
LOG2E = math.log2(math.e)
DENOM_FLOOR = 2.0 ** -64
INT_MIN = -(2 ** 31)
HIGH_BITS = 15

_ORIG_GATE_OFF = 3 * SC_W + SSM_W + ATT_W + 2 * KV_W + N_IDX_HEADS * IDX_DIM + IDX_DIM + N_IDX_HEADS
OFF_SCX = 3 * 2048
OFF_SCB = OFF_SCX + SC_W
OFF_SCC = OFF_SCB + SC_W
OFF_SSM = OFF_SCC + SC_W
OFF_Q = OFF_SSM + SSM_W
OFF_K = OFF_Q + ATT_W
OFF_V = OFF_K + KV_W
OFF_QI = OFF_V + KV_W
OFF_KI = OFF_QI + N_IDX_HEADS * IDX_DIM
NP_COLS = OFF_KI + LANES


def _cparams(sem, vmem=VMEM_LIMIT):
    return pltpu.CompilerParams(dimension_semantics=sem, vmem_limit_bytes=vmem)


def _pick(n, prefs):
    for p in prefs:
        if n % p == 0:
            return p
    return n


def _inproj_kernel(x_ref, g_ref, w_ref, o_ref, h_ref):
    @pl.when(pl.program_id(1) == 0)
    def _():
        x = x_ref[...]
        ms = jnp.mean(x * x, axis=-1, keepdims=True)
        h_ref[...] = (x * lax.rsqrt(ms + NORM_EPS) * g_ref[...]).astype(BF16)

    o_ref[...] = jnp.dot(h_ref[...], w_ref[...], preferred_element_type=F32).astype(o_ref.dtype)


def _inproj(x2, gain, w):
    t, d = x2.shape
    n = w.shape[1]
    tm = _pick(t, (1024, 512, 256, 128))
    tn = _pick(n, (1152, 384, 128))
    return pl.pallas_call(
        _inproj_kernel,
        grid=(t // tm, n // tn),
        in_specs=[pl.BlockSpec((tm, d), lambda i, j: (i, 0)),
                  pl.BlockSpec((1, d), lambda i, j: (0, 0)),
                  pl.BlockSpec((d, tn), lambda i, j: (0, j))],
        out_specs=pl.BlockSpec((tm, tn), lambda i, j: (i, j)),
        out_shape=jax.ShapeDtypeStruct((t, n), BF16),
        scratch_shapes=[pltpu.VMEM((tm, d), BF16)],
        compiler_params=_cparams(("parallel", "arbitrary")),
        name="inproj",
    )(x2, gain, w)


def _rope_table_kernel(pos_ref, invf_ref, a_ref, bm_ref, bp_ref):
    ang = pos_ref[...].astype(F32) * invf_ref[...]
    c = jnp.cos(ang)
    s = jnp.sin(ang)
    r = lax.broadcasted_iota(I32, ang.shape, 1) % HEAD_DIM
    lo = r < ROT_HALF
    hi = jnp.logical_and(r >= ROT_HALF, r < ROT_DIM)
    a_ref[...] = jnp.where(r < ROT_DIM, c, 1.0)
    bm_ref[...] = jnp.where(lo, -s, 0.0)
    bp_ref[...] = jnp.where(hi, s, 0.0)


def _rope_tables(positions):
    t = positions.size
    posb = jnp.broadcast_to(positions.reshape(t, 1), (t, LANES))
    inv_freq = ROPE_THETA ** (-jnp.arange(0, ROT_DIM, 2, dtype=F32) / ROT_DIM)
    lane = jnp.arange(LANES) % ROT_HALF
    invf = inv_freq[lane].reshape(1, LANES)
    tm = _pick(t, (1024, 512, 256, 128))
    spec = pl.BlockSpec((tm, LANES), lambda i: (i, 0))
    return pl.pallas_call(
        _rope_table_kernel,
        grid=(t // tm,),
        in_specs=[spec, pl.BlockSpec((1, LANES), lambda i: (0, 0))],
        out_specs=[spec, spec, spec],
        out_shape=[jax.ShapeDtypeStruct((t, LANES), F32)] * 3,
        compiler_params=_cparams(("parallel",)),
        name="rope_tables",
    )(posb, invf)


def _rope_kernel(q_ref, k_ref, v_ref, qi_ref, kiw_ref, a_ref, bm_ref, bp_ref,
                 qo_ref, ko_ref, vto_ref, qio_ref, kio_ref, wo_ref, kn_ref):
    a = a_ref[...]
    bm = bm_ref[...]
    bp = bp_ref[...]

    def rope(x):
        w = x.shape[1]
        reps = w // LANES
        xf = x.astype(F32)
        up = pltpu.roll(xf, w - ROT_HALF, 1)
        dn = pltpu.roll(xf, ROT_HALF, 1)
        return (xf * jnp.tile(a, (1, reps)) + up * jnp.tile(bm, (1, reps))
                + dn * jnp.tile(bp, (1, reps)))

    qo_ref[...] = (rope(q_ref[...]) * (HEAD_DIM ** -0.5 * LOG2E)).astype(BF16)
    kr = rope(k_ref[...]).astype(BF16)
    ko_ref[...] = kr
    ksq = kr.astype(F32) * kr.astype(F32)
    lane = lax.broadcasted_iota(I32, (1, LANES), 1)
    kn = jnp.zeros((1, LANES), F32)
    for g in range(N_KV_HEADS):
        n2 = jnp.sum(ksq[:, g * HEAD_DIM:(g + 1) * HEAD_DIM], axis=1, keepdims=True)
        kn = jnp.where(lane == g, jnp.max(n2, axis=0, keepdims=True), kn)
    kn_ref[0] = jnp.broadcast_to(kn, (SUBLANES, LANES))
    vto_ref[0] = v_ref[...].astype(F32).T.astype(BF16)
    qio_ref[...] = (rope(qi_ref[...]) * (IDX_DIM ** -0.5)).astype(BF16)
    kiw = kiw_ref[...]
    kio_ref[...] = rope(kiw).astype(BF16)
    wo_ref[...] = kiw.astype(F32) * (N_IDX_HEADS ** -0.5)


def _rope(p2, tabs, bsz, seq):
    t = p2.shape[0]
    tm = _pick(seq, (512, 256, 128))
    nsb = seq // tm
    a, bm, bp = tabs

    def col(width, off):
        return pl.BlockSpec((tm, width), lambda i, o=off // width: (i, o))

    tab = pl.BlockSpec((tm, LANES), lambda i: (i, 0))
    row = lambda w: pl.BlockSpec((tm, w), lambda i: (i, 0))
    return pl.pallas_call(
        _rope_kernel,
        grid=(t // tm,),
        in_specs=[col(ATT_W, OFF_Q), col(KV_W, OFF_K), col(KV_W, OFF_V),
                  col(N_IDX_HEADS * IDX_DIM, OFF_QI), col(LANES, OFF_KI), tab, tab, tab],
        out_specs=[row(ATT_W), row(KV_W),
                   pl.BlockSpec((1, KV_W, tm), lambda i: (i // nsb, 0, i % nsb)),
                   row(N_IDX_HEADS * IDX_DIM), row(LANES), row(LANES),
                   pl.BlockSpec((1, SUBLANES, LANES), lambda i: (i, 0, 0))],
        out_shape=[jax.ShapeDtypeStruct((t, ATT_W), BF16),
                   jax.ShapeDtypeStruct((t, KV_W), BF16),
                   jax.ShapeDtypeStruct((bsz, KV_W, seq), BF16),
                   jax.ShapeDtypeStruct((t, N_IDX_HEADS * IDX_DIM), BF16),
                   jax.ShapeDtypeStruct((t, LANES), BF16),
                   jax.ShapeDtypeStruct((t, LANES), F32),
                   jax.ShapeDtypeStruct((t // tm, SUBLANES, LANES), F32)],
        compiler_params=_cparams(("parallel",)),
        name="rope",
    )(p2, p2, p2, p2, p2, a, bm, bp)


def _ssm_kernel(u_ref, bbd_ref, cbd_ref, a_ref, d_ref, wg_ref, o_ref, bu_ref, st_ref, *, chunk):
    @pl.when(pl.program_id(1) == 0)
    def _():
        st_ref[...] = jnp.zeros_like(st_ref)

    u = u_ref[0]
    bu_ref[...] = jnp.dot(u, bbd_ref[...], preferred_element_type=F32)
    ar = a_ref[0:1, :]
    ai = a_ref[1:2, :]

    def step(t, carry):
        xr, xi = carry
        br = bu_ref[pl.ds(t, 1), 0:SSM_NS]
        bi = bu_ref[pl.ds(t, 1), SSM_NS:2 * SSM_NS]
        nr = ar * xr - ai * xi + br
        ni = ar * xi + ai * xr + bi
        bu_ref[pl.ds(t, 1), 0:SSM_NS] = nr
        bu_ref[pl.ds(t, 1), SSM_NS:2 * SSM_NS] = ni
        return nr, ni

    xr, xi = lax.fori_loop(0, chunk, step, (st_ref[0:1, :], st_ref[1:2, :]), unroll=8)
    st_ref[0:1, :] = xr
    st_ref[1:2, :] = xi

    y = jnp.dot(bu_ref[...].astype(BF16), cbd_ref[...], preferred_element_type=F32)
    y = y + d_ref[...] * u.astype(F32)
    z = jax.nn.gelu(y)
    gate = jnp.dot(z.astype(BF16), wg_ref[...], preferred_element_type=F32)
    o_ref[0] = (z * jax.nn.sigmoid(gate)).astype(o_ref.dtype)


def _ssm(p3, bbd, cbd, a_bar, d_skip, w_glu):
    bsz, seq, _ = p3.shape
    chunk = _pick(seq, (256, 128))
    const = lambda shape: pl.BlockSpec(shape, lambda b, c: (0,) * len(shape))
    return pl.pallas_call(
        functools.partial(_ssm_kernel, chunk=chunk),
        grid=(bsz, seq // chunk),
        in_specs=[pl.BlockSpec((1, chunk, SSM_W), lambda b, c: (b, c, OFF_SSM // SSM_W)),
                  const((SSM_W, 2 * SSM_NS)), const((2 * SSM_NS, SSM_W)),
                  const((2, SSM_NS)), const((1, SSM_W)), const((SSM_W, SSM_W))],
        out_specs=pl.BlockSpec((1, chunk, SSM_W), lambda b, c: (b, c, 0)),
        out_shape=jax.ShapeDtypeStruct((bsz, seq, SSM_W), BF16),
        scratch_shapes=[pltpu.VMEM((chunk, 2 * SSM_NS), F32), pltpu.VMEM((2, SSM_NS), F32)],
        compiler_params=_cparams(("parallel", "arbitrary")),
        name="ssm",
    )(p3, bbd, cbd, a_bar, d_skip, w_glu)


def _dsa_kernel(qi_ref, ws_ref, ki_ref, q_ref, k_ref, vt_ref, kn_ref, o_ref,
                key_ref, hkey_ref, mask_ref, qit_ref, d0_ref, d1_ref, qt_ref, s0_ref, s1_ref, s2_ref, p_ref, acc_ref,
                *, tq, seq, ksel):
    qb = pl.program_id(1)
    gid = pl.program_id(2)
    t0 = qb * tq
    cw = tq
    nc = qb + 1
    ca = 2 * tq
    nca = (qb + 2) // 2
    ks = 128
    vs = 64
    kf = float(ksel)
    jbits = (seq - 1).bit_length()
    qidx = t0 + lax.broadcasted_iota(I32, (1, tq), 1)

    @pl.when(gid == 0)
    def _select():
        qit = qi_ref[...].astype(F32).T
        qit_ref[...] = jnp.concatenate(
            [qit[h * IDX_DIM:(h + 1) * IDX_DIM, :] for h in range(N_IDX_HEADS)], axis=1).astype(BF16)
        wt = ws_ref[...].T
        wrow = jnp.concatenate([wt[IDX_DIM + h:IDX_DIM + h + 1, :] for h in range(N_IDX_HEADS)], axis=1)

        def idx_dots(c, d_ref):
            start = pl.multiple_of(c * cw, cw)
            kib = ki_ref[pl.ds(start, cw), :][:, :IDX_DIM]
            d_ref[...] = jnp.dot(kib, qit_ref[...], preferred_element_type=F32)

        def idx_scores(c, d_ref):
            for i in range(cw // ks):
                start = pl.multiple_of(c * cw, cw) + i * ks
                r = jnp.maximum(d_ref[i * ks:(i + 1) * ks, :], 0.0) * wrow
                isc = r[:, 0:tq]
                for h in range(1, N_IDX_HEADS):
                    isc = isc + r[:, h * tq:(h + 1) * tq]
                kidx = start + lax.broadcasted_iota(I32, (ks, tq), 0)
                isc = jnp.where(isc == 0.0, 0.0, isc)
                isc = jnp.where(kidx <= qidx, isc, -jnp.inf)
                bits = pltpu.bitcast(isc, I32)
                key = bits ^ ((bits >> 31) & 0x7FFFFFFF)
                key_ref[pl.ds(start, ks), :] = key
                hkey_ref[pl.ds(start, ks), :] = (key >> 16).astype(I16)

        idx_dots(0, d0_ref)

        def idx_pair(c2, _):
            c = 2 * c2
            idx_dots(c + 1, d1_ref)
            idx_scores(c, d0_ref)
            idx_dots(jnp.minimum(c + 2, nc - 1), d0_ref)
            idx_scores(c + 1, d1_ref)
            return 0

        lax.fori_loop(0, nc // 2, idx_pair, 0)

        @pl.when(nc % 2 == 1)
        def _():
            idx_scores(nc - 1, d0_ref)

        @pl.when(qb % 2 == 0)
        def _():
            mask_ref[pl.ds(pl.multiple_of((qb + 1) * cw, cw), cw), :] = jnp.zeros((cw, tq), BF16)

        small = qidx < ksel

        def key_rows(c):
            return pl.multiple_of(c * cw, cw) + lax.broadcasted_iota(I32, (cw, tq), 0)

        def count(pred):
            def body(c, cnt):
                blk = key_ref[pl.ds(pl.multiple_of(c * cw, cw), cw), :]
                hit = pred(blk, c)
                return cnt + jnp.sum(hit.reshape(cw // 32, 32, tq), axis=0)
            cnt = lax.fori_loop(0, nc, body, jnp.zeros((32, tq), F32))
            return jnp.sum(cnt, axis=0, keepdims=True)

        def count_ge(cand):
            return count(lambda x, c: jnp.where(x >= cand, 1.0, 0.0))

        def count_half_ge(c16):
            def body(c, cnt):
                blk = hkey_ref[pl.ds(pl.multiple_of(c * cw, cw), cw), :]
                hit = jnp.where(blk >= c16, jnp.ones((), I16), jnp.zeros((), I16))
                for r in range(0, cw, 32):
                    cnt = cnt + hit[r:r + 32, :]
                return cnt
            cnt = lax.fori_loop(0, nc, body, jnp.zeros((32, tq), I16))
            return jnp.sum(cnt.astype(F32), axis=0, keepdims=True)

        def count_high_ge(cand):
            return count_half_ge((cand >> 16).astype(I16))

        def count_low_ge(cand):
            return count_half_ge(((cand & 0xFFFF) - 32768).astype(I16))

        c0 = count_high_ge(jnp.zeros((1, tq), I32))
        thr = jnp.where(c0 >= kf, 0, INT_MIN).astype(I32)
        done = jnp.where(jnp.logical_or(small, c0 == kf), 1.0, 0.0)

        def bit_step(i, thr, done, counter):
            cand = thr + jnp.left_shift(jnp.int32(1), 30 - i)
            cnt = counter(cand)
            return (jnp.where(cnt >= kf, cand, thr),
                    jnp.maximum(done, jnp.where(cnt == kf, 1.0, 0.0)))

        thr, done = lax.fori_loop(0, HIGH_BITS, lambda i, c: bit_step(i, *c, count_high_ge), (thr, done))

        t16 = (thr >> 16).astype(I16)

        def low_body(c, _):
            rows = pl.ds(pl.multiple_of(c * cw, cw), cw)
            h = hkey_ref[rows, :]
            low = ((key_ref[rows, :] & 0xFFFF) - 32768).astype(I16)
            hkey_ref[rows, :] = jnp.where(h == t16, low, jnp.where(h > t16, jnp.full((), 32767, I16),
                                                                    jnp.full((), -32768, I16)))
            return 0

        lax.fori_loop(0, nc, low_body, 0)

        def w_cond(c):
            return jnp.logical_and(c[0] < 31, c[3] < 0.5)

        def w_body(c):
            thr, done = bit_step(c[0], c[1], c[2], count_low_ge)
            thr, done = bit_step(c[0] + 1, thr, done, count_low_ge)
            return c[0] + 2, thr, done, jnp.min(done)

        _, thr, done, settled = lax.while_loop(
            w_cond, w_body, (jnp.int32(HIGH_BITS), thr, done, jnp.min(done)))
        thr = jnp.where(small, INT_MIN, thr)

        def tie_index():
            n_ge = count_ge(thr)
            need = kf - count_ge(thr + 1)
            tie = jnp.logical_and(jnp.logical_not(small), n_ge > kf)

            def count_tie_below(cand):
                return count(lambda x, c: jnp.where(
                    x == thr, jnp.where(key_rows(c) < cand, 1.0, 0.0), 0.0))

            def jbit_body(i, jj):
                cand = jj + jnp.left_shift(jnp.int32(1), jbits - 1 - i)
                return jnp.where(count_tie_below(cand) < need, cand, jj)

            jj = lax.fori_loop(0, jbits, jbit_body, jnp.zeros((1, tq), I32))
            return jnp.where(tie, jj, seq)

        jmax = lax.cond(settled < 0.5, tie_index, lambda: jnp.full((1, tq), seq, I32))

        def mask_body(c, _):
            rows = pl.ds(pl.multiple_of(c * cw, cw), cw)
            x = key_ref[rows, :]
            kidx = key_rows(c)
            b = jnp.where(x > thr, 1.0, jnp.where(x == thr, jnp.where(kidx <= jmax, 1.0, 0.0), 0.0))
            mask_ref[rows, :] = jnp.where(kidx <= qidx, b, 0.0).astype(BF16)
            return 0

        lax.fori_loop(0, nc, mask_body, 0)

    qt = q_ref[...].astype(F32).T.astype(BF16)
    vrow = pl.multiple_of(gid * HEAD_DIM, HEAD_DIM)
    qt_ref[...] = jnp.zeros(qt_ref.shape, BF16)
    for r in range(REP):
        qt_ref[pl.ds(vrow, HEAD_DIM), r * tq:(r + 1) * tq] = qt[r * HEAD_DIM:(r + 1) * HEAD_DIM, :]
    acc_ref[...] = jnp.zeros(acc_ref.shape, F32)
    ones_rows = jnp.ones((ONES_ROWS, ca), BF16)

    qsq = qt.astype(F32) * qt.astype(F32)
    qn2 = jnp.concatenate([jnp.sum(qsq[r * HEAD_DIM:(r + 1) * HEAD_DIM, :], axis=0, keepdims=True)
                           for r in range(REP)], axis=1)
    kn2 = jnp.max(kn_ref[...], axis=(0, 1), keepdims=True)[0]
    kn2 = jnp.max(jnp.where(lax.broadcasted_iota(I32, (1, LANES), 1) == gid, kn2, 0.0), axis=1, keepdims=True)
    shift = jnp.sqrt(qn2) * jnp.sqrt(kn2)

    def scores(c, s_ref):
        start = pl.multiple_of(c * ca, ca)
        s_ref[...] = jnp.dot(k_ref[pl.ds(start, ca), :], qt_ref[...], preferred_element_type=F32)

    def accumulate(c, scale):
        vt = jnp.concatenate(
            [vt_ref[0, pl.ds(vrow, HEAD_DIM), pl.ds(pl.multiple_of(c * ca, ca), ca)], ones_rows], axis=0)
        acc_ref[...] = scale * acc_ref[...] + jnp.dot(vt, p_ref[...], preferred_element_type=F32)

    def mask_rows(c, i):
        return mask_ref[pl.ds(pl.multiple_of(c * ca, ca) + i * vs, vs), :]

    def softmax_chunk(c, s_ref):
        for i in range(ca // vs):
            p = jnp.exp2(s_ref[i * vs:(i + 1) * vs, :] - shift).astype(BF16)
            p_ref[i * vs:(i + 1) * vs, :] = p * jnp.tile(mask_rows(c, i), (1, REP))
        accumulate(c, 1.0)

    scores(0, s0_ref)

    def att_triple(c3, _):
        c = 3 * c3
        scores(c + 1, s1_ref)
        softmax_chunk(c, s0_ref)
        scores(c + 2, s2_ref)
        softmax_chunk(c + 1, s1_ref)
        scores(jnp.minimum(c + 3, nca - 1), s0_ref)
        softmax_chunk(c + 2, s2_ref)
        return 0

    lax.fori_loop(0, nca // 3, att_triple, 0)
    rem = nca % 3
    base = nca - rem

    @pl.when(rem >= 1)
    def _():
        scores(jnp.minimum(base + 1, nca - 1), s1_ref)
        softmax_chunk(base, s0_ref)

    @pl.when(rem == 2)
    def _():
        softmax_chunk(base + 1, s1_ref)

    @pl.when(jnp.min(acc_ref[HEAD_DIM:HEAD_DIM + 1, :]) < DENOM_FLOOR)
    def _():
        acc_ref[...] = jnp.zeros(acc_ref.shape, F32)

        def exact_chunk(c, m_prev):
            scores(c, s0_ref)
            m_new = m_prev
            for i in range(ca // vs):
                b = jnp.where(jnp.tile(mask_rows(c, i), (1, REP)) > 0, s0_ref[i * vs:(i + 1) * vs, :], NEG_BIG)
                s1_ref[i * vs:(i + 1) * vs, :] = b
                m_new = jnp.maximum(m_new, jnp.max(b, axis=0, keepdims=True))
            for i in range(ca // vs):
                p_ref[i * vs:(i + 1) * vs, :] = jnp.exp2(s1_ref[i * vs:(i + 1) * vs, :] - m_new).astype(BF16)
            accumulate(c, jnp.exp2(m_prev - m_new))
            return m_new

        lax.fori_loop(0, nca, exact_chunk, jnp.full((1, REP * tq), NEG_BIG, F32))

    acc = acc_ref[...]
    out_t = acc[:HEAD_DIM] / acc[HEAD_DIM:HEAD_DIM + 1]
    out_t = jnp.concatenate([out_t[:, r * tq:(r + 1) * tq] for r in range(REP)], axis=0)
    o_ref[...] = out_t.T.astype(o_ref.dtype)


def _dsa(qi_r, w_s, ki_r, q_r, k_r, v_t, k_n, bsz, seq):
    t = q_r.shape[0]
    tq = 256
    assert seq % (2 * tq) == 0
    ksel = min(TOPK_MAX, seq // 4)
    assert ksel <= tq
    nqb = seq // tq
    rowblk = lambda w: pl.BlockSpec((tq, w), lambda b, i, g: (b * nqb + i, 0))
    return pl.pallas_call(
        functools.partial(_dsa_kernel, tq=tq, seq=seq, ksel=ksel),
        grid=(bsz, nqb, N_KV_HEADS),
        in_specs=[rowblk(N_IDX_HEADS * IDX_DIM), rowblk(LANES),
                  pl.BlockSpec((seq, LANES), lambda b, i, g: (b, 0)),
                  pl.BlockSpec((tq, REP * HEAD_DIM), lambda b, i, g: (b * nqb + i, g)),
                  pl.BlockSpec((seq, KV_W), lambda b, i, g: (b, 0)),
                  pl.BlockSpec((1, KV_W, seq), lambda b, i, g: (b, 0, 0)),
                  pl.BlockSpec((k_n.shape[0] // bsz, SUBLANES, LANES), lambda b, i, g: (b, 0, 0))],
        out_specs=pl.BlockSpec((tq, REP * HEAD_DIM), lambda b, i, g: (b * nqb + i, g)),
        out_shape=jax.ShapeDtypeStruct((t, ATT_W), BF16),
        scratch_shapes=[pltpu.VMEM((seq, tq), I32),
                        pltpu.VMEM((seq, tq), I16),
                        pltpu.VMEM((seq, tq), BF16),
                        pltpu.VMEM((IDX_DIM, N_IDX_HEADS * tq), BF16),
                        pltpu.VMEM((tq, N_IDX_HEADS * tq), F32),
                        pltpu.VMEM((tq, N_IDX_HEADS * tq), F32),
                        pltpu.VMEM((KV_W, REP * tq), BF16),
                        pltpu.VMEM((2 * tq, REP * tq), F32),
                        pltpu.VMEM((2 * tq, REP * tq), F32),
                        pltpu.VMEM((2 * tq, REP * tq), F32),
                        pltpu.VMEM((2 * tq, REP * tq), BF16),
                        pltpu.VMEM((HEAD_DIM + ONES_ROWS, REP * tq), F32)],
        compiler_params=_cparams(("parallel", "arbitrary", "arbitrary")),
        name="dsa",
    )(qi_r, w_s, ki_r, q_r, k_r, v_t, k_n)


def _causal_conv3(cur, halo, w):
    tm = cur.shape[0]
    ext = jnp.concatenate([halo, cur], axis=0)
    return (w[2:3, :] * cur + w[1:2, :] * ext[SUBLANES - 1:SUBLANES - 1 + tm]
            + w[0:1, :] * ext[SUBLANES - 2:SUBLANES - 2 + tm])


def _merge_kernel(x_ref, ga_ref, gb_ref, gc_ref, scx_ref, scb_ref, scc_ref, hx_ref, hc_ref,
                  yb_ref, yc_ref, cw_ref, wa_ref, wb_ref, wc_ref, wo_ref, o_ref, *, tm, seq):
    first = (pl.program_id(0) * tm) % seq == 0
    cx = scc_ref[...].astype(F32) * scx_ref[...].astype(F32)
    halo = hc_ref[...].astype(F32) * hx_ref[...].astype(F32)
    halo = jnp.where(first, 0.0, halo)
    ya = scb_ref[...].astype(F32) * _causal_conv3(cx, halo, cw_ref[...])
    dot = lambda a, w: jnp.dot(a, w[...], preferred_element_type=F32)
    m = jax.nn.sigmoid(ga_ref[...].astype(F32)) * dot(ya.astype(BF16), wa_ref)
    m = m + jax.nn.sigmoid(gb_ref[...].astype(F32)) * dot(yb_ref[...], wb_ref)
    m = m + jax.nn.sigmoid(gc_ref[...].astype(F32)) * dot(yc_ref[...], wc_ref)
    o_ref[...] = x_ref[...] + dot(m.astype(BF16), wo_ref)


def _merge(x2, p2, y_b, y_c, conv_w, w_a, w_b, w_c, w_o, seq):
    t, d = x2.shape
    tm = _pick(seq, (256, 128))
    hb = tm // SUBLANES

    def col(width, off):
        return pl.BlockSpec((tm, width), lambda i, o=off // width: (i, o))

    def halo(off):
        return pl.BlockSpec((SUBLANES, SC_W), lambda i, o=off // SC_W: (jnp.maximum(i * hb - 1, 0), o))

    def const(shape):
        return pl.BlockSpec(shape, lambda i: (0, 0), pipeline_mode=pl.Buffered(1))

    row = lambda w: pl.BlockSpec((tm, w), lambda i: (i, 0))
    return pl.pallas_call(
        functools.partial(_merge_kernel, tm=tm, seq=seq),
        grid=(t // tm,),
        in_specs=[row(d), col(d, 0), col(d, d), col(d, 2 * d),
                  col(SC_W, OFF_SCX), col(SC_W, OFF_SCB), col(SC_W, OFF_SCC),
                  halo(OFF_SCX), halo(OFF_SCC), row(SSM_W), row(ATT_W),
                  const((SC_CONV, SC_W)), const((SC_W, d)), const((SSM_W, d)),
                  const((ATT_W, d)), const((d, d))],
        out_specs=row(d),
        out_shape=jax.ShapeDtypeStruct((t, d), F32),
        compiler_params=_cparams(("parallel",)),
        name="merge",
    )(x2, p2, p2, p2, p2, p2, p2, p2, p2, y_b, y_c, conv_w, w_a, w_b, w_c, w_o)


def _ffn_kernel(x_ref, hx_ref, g_ref, wg_ref, wu_ref, cg_ref, cu_ref, wd_ref, og_ref, o_ref,
                h_ref, *, tm, seq, out_norm):
    j = pl.program_id(1)

    @pl.when(j == 0)
    def _():
        first = (pl.program_id(0) * tm) % seq == 0
        x = jnp.concatenate([hx_ref[...], x_ref[...]], axis=0)
        ms = jnp.mean(x * x, axis=-1, keepdims=True)
        h = x * lax.rsqrt(ms + NORM_EPS) * g_ref[...]
        rows = lax.broadcasted_iota(I32, (tm + SUBLANES, 1), 0)
        h = jnp.where(jnp.logical_and(first, rows < SUBLANES), 0.0, h)
        h_ref[...] = h.astype(BF16)
        o_ref[...] = jnp.zeros_like(o_ref)

    h = h_ref[...]
    gt = jnp.dot(h, wg_ref[...], preferred_element_type=F32)
    ut = jnp.dot(h, wu_ref[...], preferred_element_type=F32)
    gc = _causal_conv3(gt[SUBLANES:], gt[:SUBLANES], cg_ref[...])
    uc = _causal_conv3(ut[SUBLANES:], ut[:SUBLANES], cu_ref[...])
    act = (jax.nn.silu(gc) * uc).astype(BF16)
    o_ref[...] += jnp.dot(act, wd_ref[...], preferred_element_type=F32)

    @pl.when(j == pl.num_programs(1) - 1)
    def _():
        y = x_ref[...] + o_ref[...]
        if out_norm:
            ms = jnp.mean(y * y, axis=-1, keepdims=True)
            y = y * lax.rsqrt(ms + NORM_EPS) * og_ref[...]
        o_ref[...] = y


def _ffn(x2, gain, w_up, conv_w, w_down, out_gain, seq, out_norm):
    t, d = x2.shape
    dff = w_down.shape[0]
    tm = _pick(seq, (512, 256, 128))
    tf = _pick(dff, (512, 256, 128))
    nf = dff // tf
    hb = tm // SUBLANES
    return pl.pallas_call(
        functools.partial(_ffn_kernel, tm=tm, seq=seq, out_norm=out_norm),
        grid=(t // tm, nf),
        in_specs=[pl.BlockSpec((tm, d), lambda i, j: (i, 0)),
                  pl.BlockSpec((SUBLANES, d), lambda i, j: (jnp.maximum(i * hb - 1, 0), 0)),
                  pl.BlockSpec((1, d), lambda i, j: (0, 0)),
                  pl.BlockSpec((d, tf), lambda i, j: (0, j)),
                  pl.BlockSpec((d, tf), lambda i, j: (0, j + nf)),
                  pl.BlockSpec((FFN_CONV, tf), lambda i, j: (0, j)),
                  pl.BlockSpec((FFN_CONV, tf), lambda i, j: (0, j + nf)),
                  pl.BlockSpec((tf, d), lambda i, j: (j, 0)),
                  pl.BlockSpec((1, d), lambda i, j: (0, 0))],
        out_specs=pl.BlockSpec((tm, d), lambda i, j: (i, 0)),
        out_shape=jax.ShapeDtypeStruct((t, d), F32),
        scratch_shapes=[pltpu.VMEM((tm + SUBLANES, d), BF16)],
        compiler_params=_cparams(("parallel", "arbitrary")),
        name="ffn",
    )(x2, x2, gain, w_up, w_up, conv_w, conv_w, w_down, out_gain)


def _prep_w_in(w):
    d = w.shape[0]
    pad = jnp.zeros((d, NP_COLS - w.shape[1]), BF16)
    return jnp.concatenate([w[:, _ORIG_GATE_OFF:].astype(BF16), w[:, :_ORIG_GATE_OFF].astype(BF16), pad], axis=1)


def _prep_ssm(a_re, a_im, log_dt, b_re, b_im, c_re, c_im):
    ar, ai = a_re.astype(F32), a_im.astype(F32)
    dt = jnp.exp(log_dt.astype(F32))[:, None]
    mag = jnp.exp(dt * ar)
    abr, abi = mag * jnp.cos(dt * ai), mag * jnp.sin(dt * ai)
    den = ar * ar + ai * ai
    cr = ((abr - 1.0) * ar + abi * ai) / den
    ci = (abi * ar - (abr - 1.0) * ai) / den
    br, bi = b_re.astype(F32), b_im.astype(F32)
    bbr = cr[..., None] * br - ci[..., None] * bi
    bbi = cr[..., None] * bi + ci[..., None] * br
    eye = jnp.eye(SSM_GROUPS, dtype=F32)

    def bdiag_in(m):
        return (jnp.transpose(m, (0, 2, 1))[:, :, None, :] * eye[:, None, :, None]).reshape(
            SSM_W, SSM_NS)

    def bdiag_out(m):
        return (jnp.transpose(m, (0, 2, 1))[:, :, None, :] * eye[:, None, :, None]).reshape(
            SSM_NS, SSM_W)

    bbd = jnp.concatenate([bdiag_in(bbr), bdiag_in(bbi)], axis=1).astype(BF16)
    cbd = jnp.concatenate([bdiag_out(c_re.astype(F32)), bdiag_out(-c_im.astype(F32))], axis=0).astype(BF16)
    a_rows = jnp.stack([abr.reshape(-1), abi.reshape(-1)], axis=0)
    return bbd, cbd, a_rows


def kernel(x, positions, norm_mix, w_in, sc_conv, ssm_a_re, ssm_a_im, ssm_log_dt, ssm_b_re, ssm_b_im,
           ssm_c_re, ssm_c_im, ssm_d, ssm_glu, w_branch_a, w_branch_b, w_branch_c, w_out, norm_ffn,
           w_up, ffn_conv, w_down, norm_final):
    bsz, seq, d = x.shape
    depth = w_in.shape[0]
    t = bsz * seq
    x2 = x.reshape(t, d).astype(F32)
    tabs = _rope_tables(positions.astype(I32))
    for l in range(depth):
        p2 = _inproj(x2, norm_mix[l].reshape(1, d).astype(F32), _prep_w_in(w_in[l]))
        q_r, k_r, v_t, qi_r, ki_r, w_s, k_n = _rope(p2, tabs, bsz, seq)
        bbd, cbd, a_rows = _prep_ssm(ssm_a_re[l], ssm_a_im[l], ssm_log_dt[l], ssm_b_re[l], ssm_b_im[l],
                                     ssm_c_re[l], ssm_c_im[l])
        y_b = _ssm(p2.reshape(bsz, seq, NP_COLS), bbd, cbd, a_rows,
                   ssm_d[l].reshape(1, SSM_W).astype(F32), ssm_glu[l].astype(BF16))
        y_c = _dsa(qi_r, w_s, ki_r, q_r, k_r, v_t, k_n, bsz, seq)
        x2 = _merge(x2, p2, y_b.reshape(t, SSM_W), y_c, sc_conv[l].astype(F32),
                    w_branch_a[l].astype(BF16), w_branch_b[l].astype(BF16), w_branch_c[l].astype(BF16),
                    w_out[l].astype(BF16), seq)
        x2 = _ffn(x2, norm_ffn[l].reshape(1, d).astype(F32), w_up[l].astype(BF16),
                  ffn_conv[l].astype(F32), w_down[l].astype(BF16),
                  norm_final.reshape(1, d).astype(F32), seq, out_norm=(l == depth - 1))
    return x2.reshape(bsz, seq, d).astype(x.dtype)
```

```python
import functools
import math

import jax
import jax.numpy as jnp
from jax import lax
from jax.experimental import pallas as pl
from jax.experimental.pallas import tpu as pltpu

F32 = jnp.float32
BF16 = jnp.bfloat16
I32 = jnp.int32
I16 = jnp.int16

SC_W = 512
SC_CONV = 3
SSM_W = 512
SSM_GROUP = 16
SSM_GROUPS = SSM_W // SSM_GROUP
SSM_STATE = 64
SSM_NS = SSM_GROUPS * SSM_STATE
N_Q_HEADS = 16
N_KV_HEADS = 4
HEAD_DIM = 64
ATT_W = N_Q_HEADS * HEAD_DIM
KV_W = N_KV_HEADS * HEAD_DIM
REP = N_Q_HEADS // N_KV_HEADS
ROT_DIM = HEAD_DIM // 4
ROT_HALF = ROT_DIM // 2
ROPE_THETA = 500000.0
N_IDX_HEADS = 8
IDX_DIM = 64
TOPK_MAX = 256
D_FF = 5632
FFN_CONV = 3
NORM_EPS = 1e-6

LANES = 128
SUBLANES = 8
ONES_ROWS = 16
MXU_TILE = 256
VMEM_LIMIT = 56 * 1024 * 1024
NEG_BIG = -1e30
LOG2E = math.log2(math.e)
DENOM_FLOOR = 2.0 ** -64
INT_MIN = -(2 ** 31)
HIGH_BITS = 15

_ORIG_GATE_OFF = 3 * SC_W + SSM_W + ATT_W + 2 * KV_W + N_IDX_HEADS * IDX_DIM + IDX_DIM + N_IDX_HEADS
OFF_SCX = 3 * 2048
OFF_SCB = OFF_SCX + SC_W
OFF_SCC = OFF_SCB + SC_W
OFF_SSM = OFF_SCC + SC_W
OFF_Q = OFF_SSM + SSM_W
OFF_K = OFF_Q + ATT_W
OFF_V = OFF_K + KV_W
OFF_QI = OFF_V + KV_W
OFF_KI = OFF_QI + N_IDX_HEADS * IDX_DIM
NP_COLS = OFF_KI + LANES


def _cparams(sem, vmem=VMEM_LIMIT):
    return pltpu.CompilerParams(dimension_semantics=sem, vmem_limit_bytes=vmem)


def _pick(n, prefs):
    for p in prefs:
        if n % p == 0:
            return p
    return n


def _inproj_kernel(x_ref, g_ref, w_ref, o_ref, h_ref):
    @pl.when(pl.program_id(1) == 0)
    def _():
        x = x_ref[...]
        ms = jnp.mean(x * x, axis=-1, keepdims=True)
        h_ref[...] = (x * lax.rsqrt(ms + NORM_EPS) * g_ref[...]).astype(BF16)

    o_ref[...] = jnp.dot(h_ref[...], w_ref[...], preferred_element_type=F32).astype(o_ref.dtype)


def _inproj(x2, gain, w):
    t, d = x2.shape
    n = w.shape[1]
    tm = _pick(t, (1024, 512, 256, 128))
    tn = _pick(n, (1152, 384, 128))
    return pl.pallas_call(
        _inproj_kernel,
        grid=(t // tm, n // tn),
        in_specs=[pl.BlockSpec((tm, d), lambda i, j: (i, 0)),
                  pl.BlockSpec((1, d), lambda i, j: (0, 0)),
                  pl.BlockSpec((d, tn), lambda i, j: (0, j))],
        out_specs=pl.BlockSpec((tm, tn), lambda i, j: (i, j)),
        out_shape=jax.ShapeDtypeStruct((t, n), BF16),
        scratch_shapes=[pltpu.VMEM((tm, d), BF16)],
        compiler_params=_cparams(("parallel", "arbitrary")),
        name="inproj",
    )(x2, gain, w)


def _rope_table_kernel(pos_ref, invf_ref, a_ref, bm_ref, bp_ref):
    ang = pos_ref[...].astype(F32) * invf_ref[...]
    c = jnp.cos(ang)
    s = jnp.sin(ang)
    r = lax.broadcasted_iota(I32, ang.shape, 1) % HEAD_DIM
    lo = r < ROT_HALF
    hi = jnp.logical_and(r >= ROT_HALF, r < ROT_DIM)
    a_ref[...] = jnp.where(r < ROT_DIM, c, 1.0)
    bm_ref[...] = jnp.where(lo, -s, 0.0)
    bp_ref[...] = jnp.where(hi, s, 0.0)


def _rope_tables(positions):
    t = positions.size
    posb = jnp.broadcast_to(positions.reshape(t, 1), (t, LANES))
    inv_freq = ROPE_THETA ** (-jnp.arange(0, ROT_DIM, 2, dtype=F32) / ROT_DIM)
    lane = jnp.arange(LANES) % ROT_HALF
    invf = inv_freq[lane].reshape(1, LANES)
    tm = _pick(t, (1024, 512, 256, 128))
    spec = pl.BlockSpec((tm, LANES), lambda i: (i, 0))
    return pl.pallas_call(
        _rope_table_kernel,
        grid=(t // tm,),
        in_specs=[spec, pl.BlockSpec((1, LANES), lambda i: (0, 0))],
        out_specs=[spec, spec, spec],
        out_shape=[jax.ShapeDtypeStruct((t, LANES), F32)] * 3,
        compiler_params=_cparams(("parallel",)),
        name="rope_tables",
    )(posb, invf)


def _rope_kernel(q_ref, k_ref, v_ref, qi_ref, kiw_ref, a_ref, bm_ref, bp_ref,
                 qo_ref, ko_ref, vto_ref, qio_ref, kio_ref, wo_ref, kn_ref):
    a = a_ref[...]
    bm = bm_ref[...]
    bp = bp_ref[...]

    def rope(x):
        w = x.shape[1]
        reps = w // LANES
        xf = x.astype(F32)
        up = pltpu.roll(xf, w - ROT_HALF, 1)
        dn = pltpu.roll(xf, ROT_HALF, 1)
        return (xf * jnp.tile(a, (1, reps)) + up * jnp.tile(bm, (1, reps))
                + dn * jnp.tile(bp, (1, reps)))

    qo_ref[...] = (rope(q_ref[...]) * (HEAD_DIM ** -0.5 * LOG2E)).astype(BF16)
    kr = rope(k_ref[...]).astype(BF16)
    ko_ref[...] = kr
    ksq = kr.astype(F32) * kr.astype(F32)
    lane = lax.broadcasted_iota(I32, (1, LANES), 1)
    kn = jnp.zeros((1, LANES), F32)
    for g in range(N_KV_HEADS):
        n2 = jnp.sum(ksq[:, g * HEAD_DIM:(g + 1) * HEAD_DIM], axis=1, keepdims=True)
        kn = jnp.where(lane == g, jnp.max(n2, axis=0, keepdims=True), kn)
    kn_ref[0] = jnp.broadcast_to(kn, (SUBLANES, LANES))
    vto_ref[0] = v_ref[...].astype(F32).T.astype(BF16)
    qio_ref[...] = (rope(qi_ref[...]) * (IDX_DIM ** -0.5)).astype(BF16)
    kiw = kiw_ref[...]
    kio_ref[...] = rope(kiw).astype(BF16)
    wo_ref[...] = kiw.astype(F32) * (N_IDX_HEADS ** -0.5)


def _rope(p2, tabs, bsz, seq):
    t = p2.shape[0]
    tm = _pick(seq, (512, 256, 128))
    nsb = seq // tm
    a, bm, bp = tabs

    def col(width, off):
        return pl.BlockSpec((tm, width), lambda i, o=off // width: (i, o))

    tab = pl.BlockSpec((tm, LANES), lambda i: (i, 0))
    row = lambda w: pl.BlockSpec((tm, w), lambda i: (i, 0))
    return pl.pallas_call(
        _rope_kernel,
        grid=(t // tm,),
        in_specs=[col(ATT_W, OFF_Q), col(KV_W, OFF_K), col(KV_W, OFF_V),
                  col(N_IDX_HEADS * IDX_DIM, OFF_QI), col(LANES, OFF_KI), tab, tab, tab],
        out_specs=[row(ATT_W), row(KV_W),
                   pl.BlockSpec((1, KV_W, tm), lambda i: (i // nsb, 0, i % nsb)),
                   row(N_IDX_HEADS * IDX_DIM), row(LANES), row(LANES),
                   pl.BlockSpec((1, SUBLANES, LANES), lambda i: (i, 0, 0))],
        out_shape=[jax.ShapeDtypeStruct((t, ATT_W), BF16),
                   jax.ShapeDtypeStruct((t, KV_W), BF16),
                   jax.ShapeDtypeStruct((bsz, KV_W, seq), BF16),
                   jax.ShapeDtypeStruct((t, N_IDX_HEADS * IDX_DIM), BF16),
                   jax.ShapeDtypeStruct((t, LANES), BF16),
                   jax.ShapeDtypeStruct((t, LANES), F32),
                   jax.ShapeDtypeStruct((t // tm, SUBLANES, LANES), F32)],
        compiler_params=_cparams(("parallel",)),
        name="rope",
    )(p2, p2, p2, p2, p2, a, bm, bp)


def _ssm_kernel(u_ref, bbd_ref, cbd_ref, a_ref, d_ref, wg_ref, o_ref, bu_ref, st_ref, *x_refs, bsz, chunk):
    @pl.when(pl.program_id(0) == 0)
    def _():
        st_ref[...] = jnp.zeros_like(st_ref)

    u = u_ref[...].reshape(bsz * chunk, SSM_W)
    gpt = MXU_TILE // SSM_STATE
    kin = gpt * SSM_GROUP
    for part in range(2):
        for n in range(SSM_NS // MXU_TILE):
            cols = slice(part * SSM_NS + n * MXU_TILE, part * SSM_NS + (n + 1) * MXU_TILE)
            bu_ref[:, cols] = jnp.dot(u[:, n * kin:(n + 1) * kin], bbd_ref[n * kin:(n + 1) * kin, cols],
                                      preferred_element_type=F32)
    ar = a_ref[0:1, :]
    ai = a_ref[1:2, :]

    def step(t, carry):
        out = []
        for b in range(bsz):
            xr, xi = carry[b]
            row = pl.ds(b * chunk + t, 1)
            nr = ar * xr - ai * xi + bu_ref[row, 0:SSM_NS]
            ni = ar * xi + ai * xr + bu_ref[row, SSM_NS:2 * SSM_NS]
            x_refs[b][pl.ds(t, 1), 0:SSM_NS] = nr
            x_refs[b][pl.ds(t, 1), SSM_NS:2 * SSM_NS] = ni
            out.append((nr, ni))
        return tuple(out)

    init = tuple((st_ref[2 * b:2 * b + 1, :], st_ref[2 * b + 1:2 * b + 2, :]) for b in range(bsz))
    last = lax.fori_loop(0, chunk, step, init, unroll=8)
    for b in range(bsz):
        st_ref[2 * b:2 * b + 1, :] = last[b][0]
        st_ref[2 * b + 1:2 * b + 2, :] = last[b][1]

    spt = (MXU_TILE // SSM_GROUP) * SSM_STATE
    ys = []
    for m in range(SSM_W // MXU_TILE):
        out = slice(m * MXU_TILE, (m + 1) * MXU_TILE)
        acc = None
        for part in range(2):
            rows = slice(part * SSM_NS + m * spt, part * SSM_NS + (m + 1) * spt)
            xs = jnp.concatenate([x_refs[b][:, rows].astype(BF16) for b in range(bsz)], axis=0)
            d = jnp.dot(xs, cbd_ref[rows, out], preferred_element_type=F32)
            acc = d if acc is None else acc + d
        ys.append(acc)
    y = jnp.concatenate(ys, axis=1) + d_ref[...] * u.astype(F32)
    z = jax.nn.gelu(y)
    gate = jnp.dot(z.astype(BF16), wg_ref[...], preferred_element_type=F32)
    o_ref[...] = (z * jax.nn.sigmoid(gate)).astype(o_ref.dtype).reshape(bsz, chunk, SSM_W)


def _ssm(p3, bbd, cbd, a_bar, d_skip, w_glu):
    bsz, seq, _ = p3.shape
    chunk = _pick(seq, (256, 128))
    const = lambda shape: pl.BlockSpec(shape, lambda c: (0,) * len(shape))
    return pl.pallas_call(
        functools.partial(_ssm_kernel, bsz=bsz, chunk=chunk),
        grid=(seq // chunk,),
        in_specs=[pl.BlockSpec((bsz, chunk, SSM_W), lambda c: (0, c, OFF_SSM // SSM_W)),
                  const((SSM_W, 2 * SSM_NS)), const((2 * SSM_NS, SSM_W)),
                  const((2, SSM_NS)), const((1, SSM_W)), const((SSM_W, SSM_W))],
        out_specs=pl.BlockSpec((bsz, chunk, SSM_W), lambda c: (0, c, 0)),
        out_shape=jax.ShapeDtypeStruct((bsz, seq, SSM_W), BF16),
        scratch_shapes=([pltpu.VMEM((bsz * chunk, 2 * SSM_NS), F32), pltpu.VMEM((2 * bsz, SSM_NS), F32)]
                        + [pltpu.VMEM((chunk, 2 * SSM_NS), F32)] * bsz),
        compiler_params=_cparams(("arbitrary",)),
        name="ssm",
    )(p3, bbd, cbd, a_bar, d_skip, w_glu)


def _dsa_kernel(qi_ref, ws_ref, ki_ref, q_ref, k_ref, vt_ref, kn_ref, o_ref,
                key_ref, hkey_ref, mask_ref, qit_ref, d0_ref, d1_ref, qt_ref, s0_ref, s1_ref, s2_ref, p_ref, acc_ref,
                *, tq, seq, ksel):
    qb = pl.program_id(1)
    gid = pl.program_id(2)
    t0 = qb * tq
    cw = tq
    nc = qb + 1
    ca = 2 * tq
    nca = (qb + 2) // 2
    ks = 128
    vs = 64
    kf = float(ksel)
    jbits = (seq - 1).bit_length()
    qidx = t0 + lax.broadcasted_iota(I32, (1, tq), 1)

    @pl.when(gid == 0)
    def _select():
        qit = qi_ref[...].astype(F32).T
        qit_ref[...] = jnp.concatenate(
            [qit[h * IDX_DIM:(h + 1) * IDX_DIM, :] for h in range(N_IDX_HEADS)], axis=1).astype(BF16)
        wt = ws_ref[...].T
        wrow = jnp.concatenate([wt[IDX_DIM + h:IDX_DIM + h + 1, :] for h in range(N_IDX_HEADS)], axis=1)

        def idx_dots(c, d_ref):
            start = pl.multiple_of(c * cw, cw)
            kib = ki_ref[pl.ds(start, cw), :][:, :IDX_DIM]
            d_ref[...] = jnp.dot(kib, qit_ref[...], preferred_element_type=F32)

        def idx_scores(c, d_ref):
            for i in range(cw // ks):
                start = pl.multiple_of(c * cw, cw) + i * ks
                r = jnp.maximum(d_ref[i * ks:(i + 1) * ks, :], 0.0) * wrow
                isc = r[:, 0:tq]
                for h in range(1, N_IDX_HEADS):
                    isc = isc + r[:, h * tq:(h + 1) * tq]
                kidx = start + lax.broadcasted_iota(I32, (ks, tq), 0)
                isc = jnp.where(isc == 0.0, 0.0, isc)
                isc = jnp.where(kidx <= qidx, isc, -jnp.inf)
                bits = pltpu.bitcast(isc, I32)
                key = bits ^ ((bits >> 31) & 0x7FFFFFFF)
                key_ref[pl.ds(start, ks), :] = key
                hkey_ref[pl.ds(start, ks), :] = (key >> 16).astype(I16)

        idx_dots(0, d0_ref)

        def idx_pair(c2, _):
            c = 2 * c2
            idx_dots(c + 1, d1_ref)
            idx_scores(c, d0_ref)
            idx_dots(jnp.minimum(c + 2, nc - 1), d0_ref)
            idx_scores(c + 1, d1_ref)
            return 0

        lax.fori_loop(0, nc // 2, idx_pair, 0)

        @pl.when(nc % 2 == 1)
        def _():
            idx_scores(nc - 1, d0_ref)

        @pl.when(qb % 2 == 0)
        def _():
            mask_ref[pl.ds(pl.multiple_of((qb + 1) * cw, cw), cw), :] = jnp.zeros((cw, tq), BF16)

        small = qidx < ksel

        def key_rows(c):
            return pl.multiple_of(c * cw, cw) + lax.broadcasted_iota(I32, (cw, tq), 0)

        def count(pred):
            def body(c, cnt):
                blk = key_ref[pl.ds(pl.multiple_of(c * cw, cw), cw), :]
                hit = pred(blk, c)
                return cnt + jnp.sum(hit.reshape(cw // 32, 32, tq), axis=0)
            cnt = lax.fori_loop(0, nc, body, jnp.zeros((32, tq), F32))
            return jnp.sum(cnt, axis=0, keepdims=True)

        def count_ge(cand):
            return count(lambda x, c: jnp.where(x >= cand, 1.0, 0.0))

        def count_half_ge(c16):
            def body(c, cnt):
                blk = hkey_ref[pl.ds(pl.multiple_of(c * cw, cw), cw), :]
                hit = jnp.where(blk >= c16, jnp.ones((), I16), jnp.zeros((), I16))
                for r in range(0, cw, 32):
                    cnt = cnt + hit[r:r + 32, :]
                return cnt
            cnt = lax.fori_loop(0, nc, body, jnp.zeros((32, tq), I16))
            return jnp.sum(cnt.astype(F32), axis=0, keepdims=True)

        def count_high_ge(cand):
            return count_half_ge((cand >> 16).astype(I16))

        def count_low_ge(cand):
            return count_half_ge(((cand & 0xFFFF) - 32768).astype(I16))

        c0 = count_high_ge(jnp.zeros((1, tq), I32))
        thr = jnp.where(c0 >= kf, 0, INT_MIN).astype(I32)
        done = jnp.where(jnp.logical_or(small, c0 == kf), 1.0, 0.0)

        def bit_step(i, thr, done, counter):
            cand = thr + jnp.left_shift(jnp.int32(1), 30 - i)
            cnt = counter(cand)
            return (jnp.where(cnt >= kf, cand, thr),
                    jnp.maximum(done, jnp.where(cnt == kf, 1.0, 0.0)))

        thr, done = lax.fori_loop(0, HIGH_BITS, lambda i, c: bit_step(i, *c, count_high_ge), (thr, done))

        t16 = (thr >> 16).astype(I16)

        def low_body(c, _):
            rows = pl.ds(pl.multiple_of(c * cw, cw), cw)
            h = hkey_ref[rows, :]
            low = ((key_ref[rows, :] & 0xFFFF) - 32768).astype(I16)
            hkey_ref[rows, :] = jnp.where(h == t16, low, jnp.where(h > t16, jnp.full((), 32767, I16),
                                                                    jnp.full((), -32768, I16)))
            return 0

        lax.fori_loop(0, nc, low_body, 0)

        def w_cond(c):
            return jnp.logical_and(c[0] < 31, c[3] < 0.5)

        def w_body(c):
            thr, done = bit_step(c[0], c[1], c[2], count_low_ge)
            thr, done = bit_step(c[0] + 1, thr, done, count_low_ge)
            return c[0] + 2, thr, done, jnp.min(done)

        _, thr, done, settled = lax.while_loop(
            w_cond, w_body, (jnp.int32(HIGH_BITS), thr, done, jnp.min(done)))
        thr = jnp.where(small, INT_MIN, thr)

        def tie_index():
            n_ge = count_ge(thr)
            need = kf - count_ge(thr + 1)
            tie = jnp.logical_and(jnp.logical_not(small), n_ge > kf)

            def count_tie_below(cand):
                return count(lambda x, c: jnp.where(
                    x == thr, jnp.where(key_rows(c) < cand, 1.0, 0.0), 0.0))

            def jbit_body(i, jj):
                cand = jj + jnp.left_shift(jnp.int32(1), jbits - 1 - i)
                return jnp.where(count_tie_below(cand) < need, cand, jj)

            jj = lax.fori_loop(0, jbits, jbit_body, jnp.zeros((1, tq), I32))
            return jnp.where(tie, jj, seq)

        jmax = lax.cond(settled < 0.5, tie_index, lambda: jnp.full((1, tq), seq, I32))

        def mask_body(c, _):
            rows = pl.ds(pl.multiple_of(c * cw, cw), cw)
            x = key_ref[rows, :]
            kidx = key_rows(c)
            b = jnp.where(x > thr, 1.0, jnp.where(x == thr, jnp.where(kidx <= jmax, 1.0, 0.0), 0.0))
            mask_ref[rows, :] = jnp.where(kidx <= qidx, b, 0.0).astype(BF16)
            return 0

        lax.fori_loop(0, nc, mask_body, 0)

    qt = q_ref[...].astype(F32).T.astype(BF16)
    vrow = pl.multiple_of(gid * HEAD_DIM, HEAD_DIM)
    qt_ref[...] = jnp.zeros(qt_ref.shape, BF16)
    for r in range(REP):
        qt_ref[pl.ds(vrow, HEAD_DIM), r * tq:(r + 1) * tq] = qt[r * HEAD_DIM:(r + 1) * HEAD_DIM, :]
    acc_ref[...] = jnp.zeros(acc_ref.shape, F32)
    ones_rows = jnp.ones((ONES_ROWS, ca), BF16)

    qsq = qt.astype(F32) * qt.astype(F32)
    qn2 = jnp.concatenate([jnp.sum(qsq[r * HEAD_DIM:(r + 1) * HEAD_DIM, :], axis=0, keepdims=True)
                           for r in range(REP)], axis=1)
    kn2 = jnp.max(kn_ref[...], axis=(0, 1), keepdims=True)[0]
    kn2 = jnp.max(jnp.where(lax.broadcasted_iota(I32, (1, LANES), 1) == gid, kn2, 0.0), axis=1, keepdims=True)
    shift = jnp.sqrt(qn2) * jnp.sqrt(kn2)

    def scores(c, s_ref):
        start = pl.multiple_of(c * ca, ca)
        s_ref[...] = jnp.dot(k_ref[pl.ds(start, ca), :], qt_ref[...], preferred_element_type=F32)

    def accumulate(c, scale):
        vt = jnp.concatenate(
            [vt_ref[0, pl.ds(vrow, HEAD_DIM), pl.ds(pl.multiple_of(c * ca, ca), ca)], ones_rows], axis=0)
        acc_ref[...] = scale * acc_ref[...] + jnp.dot(vt, p_ref[...], preferred_element_type=F32)

    def mask_rows(c, i):
        return mask_ref[pl.ds(pl.multiple_of(c * ca, ca) + i * vs, vs), :]

    def softmax_chunk(c, s_ref):
        for i in range(ca // vs):
            p = jnp.exp2(s_ref[i * vs:(i + 1) * vs, :] - shift).astype(BF16)
            p_ref[i * vs:(i + 1) * vs, :] = p * jnp.tile(mask_rows(c, i), (1, REP))
        accumulate(c, 1.0)

    scores(0, s0_ref)

    def att_triple(c3, _):
        c = 3 * c3
        scores(c + 1, s1_ref)
        softmax_chunk(c, s0_ref)
        scores(c + 2, s2_ref)
        softmax_chunk(c + 1, s1_ref)
        scores(jnp.minimum(c + 3, nca - 1), s0_ref)
        softmax_chunk(c + 2, s2_ref)
        return 0

    lax.fori_loop(0, nca // 3, att_triple, 0)
    rem = nca % 3
    base = nca - rem

    @pl.when(rem >= 1)
    def _():
        scores(jnp.minimum(base + 1, nca - 1), s1_ref)
        softmax_chunk(base, s0_ref)

    @pl.when(rem == 2)
    def _():
        softmax_chunk(base + 1, s1_ref)

    @pl.when(jnp.min(acc_ref[HEAD_DIM:HEAD_DIM + 1, :]) < DENOM_FLOOR)
    def _():
        acc_ref[...] = jnp.zeros(acc_ref.shape, F32)

        def exact_chunk(c, m_prev):
            scores(c, s0_ref)
            m_new = m_prev
            for i in range(ca // vs):
                b = jnp.where(jnp.tile(mask_rows(c, i), (1, REP)) > 0, s0_ref[i * vs:(i + 1) * vs, :], NEG_BIG)
                s1_ref[i * vs:(i + 1) * vs, :] = b
                m_new = jnp.maximum(m_new, jnp.max(b, axis=0, keepdims=True))
            for i in range(ca // vs):
                p_ref[i * vs:(i + 1) * vs, :] = jnp.exp2(s1_ref[i * vs:(i + 1) * vs, :] - m_new).astype(BF16)
            accumulate(c, jnp.exp2(m_prev - m_new))
            return m_new

        lax.fori_loop(0, nca, exact_chunk, jnp.full((1, REP * tq), NEG_BIG, F32))

    acc = acc_ref[...]
    out_t = acc[:HEAD_DIM] / acc[HEAD_DIM:HEAD_DIM + 1]
    out_t = jnp.concatenate([out_t[:, r * tq:(r + 1) * tq] for r in range(REP)], axis=0)
    o_ref[...] = out_t.T.astype(o_ref.dtype)


def _dsa(qi_r, w_s, ki_r, q_r, k_r, v_t, k_n, bsz, seq):
    t = q_r.shape[0]
    tq = 256
    assert seq % (2 * tq) == 0
    ksel = min(TOPK_MAX, seq // 4)
    assert ksel <= tq
    nqb = seq // tq
    rowblk = lambda w: pl.BlockSpec((tq, w), lambda b, i, g: (b * nqb + i, 0))
    return pl.pallas_call(
        functools.partial(_dsa_kernel, tq=tq, seq=seq, ksel=ksel),
        grid=(bsz, nqb, N_KV_HEADS),
        in_specs=[rowblk(N_IDX_HEADS * IDX_DIM), rowblk(LANES),
                  pl.BlockSpec((seq, LANES), lambda b, i, g: (b, 0)),
                  pl.BlockSpec((tq, REP * HEAD_DIM), lambda b, i, g: (b * nqb + i, g)),
                  pl.BlockSpec((seq, KV_W), lambda b, i, g: (b, 0)),
                  pl.BlockSpec((1, KV_W, seq), lambda b, i, g: (b, 0, 0)),
                  pl.BlockSpec((k_n.shape[0] // bsz, SUBLANES, LANES), lambda b, i, g: (b, 0, 0))],
        out_specs=pl.BlockSpec((tq, REP * HEAD_DIM), lambda b, i, g: (b * nqb + i, g)),
        out_shape=jax.ShapeDtypeStruct((t, ATT_W), BF16),
        scratch_shapes=[pltpu.VMEM((seq, tq), I32),
                        pltpu.VMEM((seq, tq), I16),
                        pltpu.VMEM((seq, tq), BF16),
                        pltpu.VMEM((IDX_DIM, N_IDX_HEADS * tq), BF16),
                        pltpu.VMEM((tq, N_IDX_HEADS * tq), F32),
                        pltpu.VMEM((tq, N_IDX_HEADS * tq), F32),
                        pltpu.VMEM((KV_W, REP * tq), BF16),
                        pltpu.VMEM((2 * tq, REP * tq), F32),
                        pltpu.VMEM((2 * tq, REP * tq), F32),
                        pltpu.VMEM((2 * tq, REP * tq), F32),
                        pltpu.VMEM((2 * tq, REP * tq), BF16),
                        pltpu.VMEM((HEAD_DIM + ONES_ROWS, REP * tq), F32)],
        compiler_params=_cparams(("parallel", "arbitrary", "arbitrary")),
        name="dsa",
    )(qi_r, w_s, ki_r, q_r, k_r, v_t, k_n)


def _causal_conv3(cur, halo, w):
    tm = cur.shape[0]
    ext = jnp.concatenate([halo, cur], axis=0)
    return (w[2:3, :] * cur + w[1:2, :] * ext[SUBLANES - 1:SUBLANES - 1 + tm]
            + w[0:1, :] * ext[SUBLANES - 2:SUBLANES - 2 + tm])


def _merge_kernel(x_ref, ga_ref, gb_ref, gc_ref, scx_ref, scb_ref, scc_ref, hx_ref, hc_ref,
                  yb_ref, yc_ref, cw_ref, wa_ref, wb_ref, wc_ref, wo_ref, o_ref, *, tm, seq):
    first = (pl.program_id(0) * tm) % seq == 0
    cx = scc_ref[...].astype(F32) * scx_ref[...].astype(F32)
    halo = hc_ref[...].astype(F32) * hx_ref[...].astype(F32)
    halo = jnp.where(first, 0.0, halo)
    ya = scb_ref[...].astype(F32) * _causal_conv3(cx, halo, cw_ref[...])
    dot = lambda a, w: jnp.dot(a, w[...], preferred_element_type=F32)
    m = jax.nn.sigmoid(ga_ref[...].astype(F32)) * dot(ya.astype(BF16), wa_ref)
    m = m + jax.nn.sigmoid(gb_ref[...].astype(F32)) * dot(yb_ref[...], wb_ref)
    m = m + jax.nn.sigmoid(gc_ref[...].astype(F32)) * dot(yc_ref[...], wc_ref)
    o_ref[...] = x_ref[...] + dot(m.astype(BF16), wo_ref)


def _merge(x2, p2, y_b, y_c, conv_w, w_a, w_b, w_c, w_o, seq):
    t, d = x2.shape
    tm = _pick(seq, (256, 128))
    hb = tm // SUBLANES

    def col(width, off):
        return pl.BlockSpec((tm, width), lambda i, o=off // width: (i, o))

    def halo(off):
        return pl.BlockSpec((SUBLANES, SC_W), lambda i, o=off // SC_W: (jnp.maximum(i * hb - 1, 0), o))

    def const(shape):
        return pl.BlockSpec(shape, lambda i: (0, 0), pipeline_mode=pl.Buffered(1))

    row = lambda w: pl.BlockSpec((tm, w), lambda i: (i, 0))
    return pl.pallas_call(
        functools.partial(_merge_kernel, tm=tm, seq=seq),
        grid=(t // tm,),
        in_specs=[row(d), col(d, 0), col(d, d), col(d, 2 * d),
                  col(SC_W, OFF_SCX), col(SC_W, OFF_SCB), col(SC_W, OFF_SCC),
                  halo(OFF_SCX), halo(OFF_SCC), row(SSM_W), row(ATT_W),
                  const((SC_CONV, SC_W)), const((SC_W, d)), const((SSM_W, d)),
                  const((ATT_W, d)), const((d, d))],
        out_specs=row(d),
        out_shape=jax.ShapeDtypeStruct((t, d), F32),
        compiler_params=_cparams(("parallel",)),
        name="merge",
    )(x2, p2, p2, p2, p2, p2, p2, p2, p2, y_b, y_c, conv_w, w_a, w_b, w_c, w_o)


def _ffn_kernel(x_ref, hx_ref, g_ref, wg_ref, wu_ref, cg_ref, cu_ref, wd_ref, og_ref, o_ref,
                h_ref, *, tm, seq, out_norm):
    j = pl.program_id(1)

    @pl.when(j == 0)
    def _():
        first = (pl.program_id(0) * tm) % seq == 0
        x = jnp.concatenate([hx_ref[...], x_ref[...]], axis=0)
        ms = jnp.mean(x * x, axis=-1, keepdims=True)
        h = x * lax.rsqrt(ms + NORM_EPS) * g_ref[...]
        rows = lax.broadcasted_iota(I32, (tm + SUBLANES, 1), 0)
        h = jnp.where(jnp.logical_and(first, rows < SUBLANES), 0.0, h)
        h_ref[...] = h.astype(BF16)
        o_ref[...] = jnp.zeros_like(o_ref)

    h = h_ref[...]
    gt = jnp.dot(h, wg_ref[...], preferred_element_type=F32)
    ut = jnp.dot(h, wu_ref[...], preferred_element_type=F32)
    gc = _causal_conv3(gt[SUBLANES:], gt[:SUBLANES], cg_ref[...])
    uc = _causal_conv3(ut[SUBLANES:], ut[:SUBLANES], cu_ref[...])
    act = (jax.nn.silu(gc) * uc).astype(BF16)
    o_ref[...] += jnp.dot(act, wd_ref[...], preferred_element_type=F32)

    @pl.when(j == pl.num_programs(1) - 1)
    def _():
        y = x_ref[...] + o_ref[...]
        if out_norm:
            ms = jnp.mean(y * y, axis=-1, keepdims=True)
            y = y * lax.rsqrt(ms + NORM_EPS) * og_ref[...]
        o_ref[...] = y


def _ffn(x2, gain, w_up, conv_w, w_down, out_gain, seq, out_norm):
    t, d = x2.shape
    dff = w_down.shape[0]
    tm = _pick(seq, (512, 256, 128))
    tf = _pick(dff, (512, 256, 128))
    nf = dff // tf
    hb = tm // SUBLANES
    return pl.pallas_call(
        functools.partial(_ffn_kernel, tm=tm, seq=seq, out_norm=out_norm),
        grid=(t // tm, nf),
        in_specs=[pl.BlockSpec((tm, d), lambda i, j: (i, 0)),
                  pl.BlockSpec((SUBLANES, d), lambda i, j: (jnp.maximum(i * hb - 1, 0), 0)),
                  pl.BlockSpec((1, d), lambda i, j: (0, 0)),
                  pl.BlockSpec((d, tf), lambda i, j: (0, j)),
                  pl.BlockSpec((d, tf), lambda i, j: (0, j + nf)),
                  pl.BlockSpec((FFN_CONV, tf), lambda i, j: (0, j)),
                  pl.BlockSpec((FFN_CONV, tf), lambda i, j: (0, j + nf)),
                  pl.BlockSpec((tf, d), lambda i, j: (j, 0)),
                  pl.BlockSpec((1, d), lambda i, j: (0, 0))],
        out_specs=pl.BlockSpec((tm, d), lambda i, j: (i, 0)),
        out_shape=jax.ShapeDtypeStruct((t, d), F32),
        scratch_shapes=[pltpu.VMEM((tm + SUBLANES, d), BF16)],
        compiler_params=_cparams(("parallel", "arbitrary")),
        name="ffn",
    )(x2, x2, gain, w_up, w_up, conv_w, conv_w, w_down, out_gain)


def _prep_w_in(w):
    d = w.shape[0]
    pad = jnp.zeros((d, NP_COLS - w.shape[1]), BF16)
    return jnp.concatenate([w[:, _ORIG_GATE_OFF:].astype(BF16), w[:, :_ORIG_GATE_OFF].astype(BF16), pad], axis=1)


def _prep_ssm(a_re, a_im, log_dt, b_re, b_im, c_re, c_im):
    ar, ai = a_re.astype(F32), a_im.astype(F32)
    dt = jnp.exp(log_dt.astype(F32))[:, None]
    mag = jnp.exp(dt * ar)
    abr, abi = mag * jnp.cos(dt * ai), mag * jnp.sin(dt * ai)
    den = ar * ar + ai * ai
    cr = ((abr - 1.0) * ar + abi * ai) / den
    ci = (abi * ar - (abr - 1.0) * ai) / den
    br, bi = b_re.astype(F32), b_im.astype(F32)
    bbr = cr[..., None] * br - ci[..., None] * bi
    bbi = cr[..., None] * bi + ci[..., None] * br
    eye = jnp.eye(SSM_GROUPS, dtype=F32)

    def bdiag_in(m):
        return (jnp.transpose(m, (0, 2, 1))[:, :, None, :] * eye[:, None, :, None]).reshape(
            SSM_W, SSM_NS)

    def bdiag_out(m):
        return (jnp.transpose(m, (0, 2, 1))[:, :, None, :] * eye[:, None, :, None]).reshape(
            SSM_NS, SSM_W)

    bbd = jnp.concatenate([bdiag_in(bbr), bdiag_in(bbi)], axis=1).astype(BF16)
    cbd = jnp.concatenate([bdiag_out(c_re.astype(F32)), bdiag_out(-c_im.astype(F32))], axis=0).astype(BF16)
    a_rows = jnp.stack([abr.reshape(-1), abi.reshape(-1)], axis=0)
    return bbd, cbd, a_rows


def kernel(x, positions, norm_mix, w_in, sc_conv, ssm_a_re, ssm_a_im, ssm_log_dt, ssm_b_re, ssm_b_im,
           ssm_c_re, ssm_c_im, ssm_d, ssm_glu, w_branch_a, w_branch_b, w_branch_c, w_out, norm_ffn,
           w_up, ffn_conv, w_down, norm_final):
    bsz, seq, d = x.shape
    depth = w_in.shape[0]
    t = bsz * seq
    x2 = x.reshape(t, d).astype(F32)
    tabs = _rope_tables(positions.astype(I32))
    for l in range(depth):
        p2 = _inproj(x2, norm_mix[l].reshape(1, d).astype(F32), _prep_w_in(w_in[l]))
        q_r, k_r, v_t, qi_r, ki_r, w_s, k_n = _rope(p2, tabs, bsz, seq)
        bbd, cbd, a_rows = _prep_ssm(ssm_a_re[l], ssm_a_im[l], ssm_log_dt[l], ssm_b_re[l], ssm_b_im[l],
                                     ssm_c_re[l], ssm_c_im[l])
        y_b = _ssm(p2.reshape(bsz, seq, NP_COLS), bbd, cbd, a_rows,
                   ssm_d[l].reshape(1, SSM_W).astype(F32), ssm_glu[l].astype(BF16))
        y_c = _dsa(qi_r, w_s, ki_r, q_r, k_r, v_t, k_n, bsz, seq)
        x2 = _merge(x2, p2, y_b.reshape(t, SSM_W), y_c, sc_conv[l].astype(F32),
                    w_branch_a[l].astype(BF16), w_branch_b[l].astype(BF16), w_branch_c[l].astype(BF16),
                    w_out[l].astype(BF16), seq)
        x2 = _ffn(x2, norm_ffn[l].reshape(1, d).astype(F32), w_up[l].astype(BF16),
                  ffn_conv[l].astype(F32), w_down[l].astype(BF16),
                  norm_final.reshape(1, d).astype(F32), seq, out_norm=(l == depth - 1))
    return x2.reshape(bsz, seq, d).astype(x.dtype)
```

```python
import functools
import math

import jax
import jax.numpy as jnp
from jax import lax
from jax.experimental import pallas as pl
from jax.experimental.pallas import tpu as pltpu

F32 = jnp.float32
BF16 = jnp.bfloat16
I32 = jnp.int32
I16 = jnp.int16

SC_W = 512
SC_CONV = 3
SSM_W = 512
SSM_GROUP = 16
SSM_GROUPS = SSM_W // SSM_GROUP
SSM_STATE = 64
SSM_NS = SSM_GROUPS * SSM_STATE
N_Q_HEADS = 16
N_KV_HEADS = 4
HEAD_DIM = 64
ATT_W = N_Q_HEADS * HEAD_DIM
KV_W = N_KV_HEADS * HEAD_DIM
REP = N_Q_HEADS // N_KV_HEADS
ROT_DIM = HEAD_DIM // 4
ROT_HALF = ROT_DIM // 2
ROPE_THETA = 500000.0
N_IDX_HEADS = 8
IDX_DIM = 64
TOPK_MAX = 256
D_FF = 5632
FFN_CONV = 3
NORM_EPS = 1e-6

LANES = 128
SUBLANES = 8
ONES_ROWS = 16
MXU_TILE = 256
VMEM_LIMIT = 56 * 1024 * 1024
NEG_BIG = -1e30
LOG2E = math.log2(math.e)
DENOM_FLOOR = 2.0 ** -64
INT_MIN = -(2 ** 31)
HIGH_BITS = 15

_ORIG_GATE_OFF = 3 * SC_W + SSM_W + ATT_W + 2 * KV_W + N_IDX_HEADS * IDX_DIM + IDX_DIM + N_IDX_HEADS
OFF_SCX = 3 * 2048
OFF_SCB = OFF_SCX + SC_W
OFF_SCC = OFF_SCB + SC_W
OFF_SSM = OFF_SCC + SC_W
OFF_Q = OFF_SSM + SSM_W
OFF_K = OFF_Q + ATT_W
OFF_V = OFF_K + KV_W
OFF_QI = OFF_V + KV_W
OFF_KI = OFF_QI + N_IDX_HEADS * IDX_DIM
NP_COLS = OFF_KI + LANES


def _cparams(sem, vmem=VMEM_LIMIT):
    return pltpu.CompilerParams(dimension_semantics=sem, vmem_limit_bytes=vmem)


def _pick(n, prefs):
    for p in prefs:
        if n % p == 0:
            return p
    return n


def _inproj_kernel(x_ref, g_ref, w_ref, o_ref, h_ref):
    @pl.when(pl.program_id(1) == 0)
    def _():
        x = x_ref[...]
        ms = jnp.mean(x * x, axis=-1, keepdims=True)
        h_ref[...] = (x * lax.rsqrt(ms + NORM_EPS) * g_ref[...]).astype(BF16)

    o_ref[...] = jnp.dot(h_ref[...], w_ref[...], preferred_element_type=F32).astype(o_ref.dtype)


def _inproj(x2, gain, w):
    t, d = x2.shape
    n = w.shape[1]
    tm = _pick(t, (1024, 512, 256, 128))
    tn = _pick(n, (1152, 384, 128))
    return pl.pallas_call(
        _inproj_kernel,
        grid=(t // tm, n // tn),
        in_specs=[pl.BlockSpec((tm, d), lambda i, j: (i, 0)),
                  pl.BlockSpec((1, d), lambda i, j: (0, 0)),
                  pl.BlockSpec((d, tn), lambda i, j: (0, j))],
        out_specs=pl.BlockSpec((tm, tn), lambda i, j: (i, j)),
        out_shape=jax.ShapeDtypeStruct((t, n), BF16),
        scratch_shapes=[pltpu.VMEM((tm, d), BF16)],
        compiler_params=_cparams(("parallel", "arbitrary")),
        name="inproj",
    )(x2, gain, w)


def _rope_table_kernel(pos_ref, invf_ref, a_ref, bm_ref, bp_ref):
    ang = pos_ref[...].astype(F32) * invf_ref[...]
    c = jnp.cos(ang)
    s = jnp.sin(ang)
    r = lax.broadcasted_iota(I32, ang.shape, 1) % HEAD_DIM
    lo = r < ROT_HALF
    hi = jnp.logical_and(r >= ROT_HALF, r < ROT_DIM)
    a_ref[...] = jnp.where(r < ROT_DIM, c, 1.0)
    bm_ref[...] = jnp.where(lo, -s, 0.0)
    bp_ref[...] = jnp.where(hi, s, 0.0)


def _rope_tables(positions):
    t = positions.size
    posb = jnp.broadcast_to(positions.reshape(t, 1), (t, LANES))
    inv_freq = ROPE_THETA ** (-jnp.arange(0, ROT_DIM, 2, dtype=F32) / ROT_DIM)
    lane = jnp.arange(LANES) % ROT_HALF
    invf = inv_freq[lane].reshape(1, LANES)
    tm = _pick(t, (1024, 512, 256, 128))
    spec = pl.BlockSpec((tm, LANES), lambda i: (i, 0))
    return pl.pallas_call(
        _rope_table_kernel,
        grid=(t // tm,),
        in_specs=[spec, pl.BlockSpec((1, LANES), lambda i: (0, 0))],
        out_specs=[spec, spec, spec],
        out_shape=[jax.ShapeDtypeStruct((t, LANES), F32)] * 3,
        compiler_params=_cparams(("parallel",)),
        name="rope_tables",
    )(posb, invf)


def _rope_kernel(q_ref, k_ref, v_ref, qi_ref, kiw_ref, a_ref, bm_ref, bp_ref,
                 qo_ref, ko_ref, vto_ref, qio_ref, kio_ref, wo_ref, kn_ref):
    a = a_ref[...]
    bm = bm_ref[...]
    bp = bp_ref[...]

    def rope(x):
        w = x.shape[1]
        reps = w // LANES
        xf = x.astype(F32)
        up = pltpu.roll(xf, w - ROT_HALF, 1)
        dn = pltpu.roll(xf, ROT_HALF, 1)
        return (xf * jnp.tile(a, (1, reps)) + up * jnp.tile(bm, (1, reps))
                + dn * jnp.tile(bp, (1, reps)))

    qo_ref[...] = (rope(q_ref[...]) * (HEAD_DIM ** -0.5 * LOG2E)).astype(BF16)
    kr = rope(k_ref[...]).astype(BF16)
    ko_ref[...] = kr
    ksq = kr.astype(F32) * kr.astype(F32)
    lane = lax.broadcasted_iota(I32, (1, LANES), 1)
    kn = jnp.zeros((1, LANES), F32)
    for g in range(N_KV_HEADS):
        n2 = jnp.sum(ksq[:, g * HEAD_DIM:(g + 1) * HEAD_DIM], axis=1, keepdims=True)
        kn = jnp.where(lane == g, jnp.max(n2, axis=0, keepdims=True), kn)
    kn_ref[0] = jnp.broadcast_to(kn, (SUBLANES, LANES))
    vto_ref[0] = v_ref[...].astype(F32).T.astype(BF16)
    qio_ref[...] = (rope(qi_ref[...]) * (IDX_DIM ** -0.5)).astype(BF16)
    kiw = kiw_ref[...]
    kio_ref[...] = rope(kiw).astype(BF16)
    wo_ref[...] = kiw.astype(F32) * (N_IDX_HEADS ** -0.5)


def _rope(p2, tabs, bsz, seq):
    t = p2.shape[0]
    tm = _pick(seq, (512, 256, 128))
    nsb = seq // tm
    a, bm, bp = tabs

    def col(width, off):
        return pl.BlockSpec((tm, width), lambda i, o=off // width: (i, o))

    tab = pl.BlockSpec((tm, LANES), lambda i: (i, 0))
    row = lambda w: pl.BlockSpec((tm, w), lambda i: (i, 0))
    return pl.pallas_call(
        _rope_kernel,
        grid=(t // tm,),
        in_specs=[col(ATT_W, OFF_Q), col(KV_W, OFF_K), col(KV_W, OFF_V),
                  col(N_IDX_HEADS * IDX_DIM, OFF_QI), col(LANES, OFF_KI), tab, tab, tab],
        out_specs=[row(ATT_W), row(KV_W),
                   pl.BlockSpec((1, KV_W, tm), lambda i: (i // nsb, 0, i % nsb)),
                   row(N_IDX_HEADS * IDX_DIM), row(LANES), row(LANES),
                   pl.BlockSpec((1, SUBLANES, LANES), lambda i: (i, 0, 0))],
        out_shape=[jax.ShapeDtypeStruct((t, ATT_W), BF16),
                   jax.ShapeDtypeStruct((t, KV_W), BF16),
                   jax.ShapeDtypeStruct((bsz, KV_W, seq), BF16),
                   jax.ShapeDtypeStruct((t, N_IDX_HEADS * IDX_DIM), BF16),
                   jax.ShapeDtypeStruct((t, LANES), BF16),
                   jax.ShapeDtypeStruct((t, LANES), F32),
                   jax.ShapeDtypeStruct((t // tm, SUBLANES, LANES), F32)],
        compiler_params=_cparams(("parallel",)),
        name="rope",
    )(p2, p2, p2, p2, p2, a, bm, bp)


def _ssm_kernel(u_ref, bbd_ref, cbd_ref, a_ref, d_ref, wg_ref, o_ref, bu_ref, st_ref, *x_refs, bsz, chunk):
    @pl.when(pl.program_id(0) == 0)
    def _():
        st_ref[...] = jnp.zeros_like(st_ref)

    u = u_ref[...].reshape(bsz * chunk, SSM_W)
    gpt = MXU_TILE // SSM_STATE
    kin = gpt * SSM_GROUP
    for part in range(2):
        for n in range(SSM_NS // MXU_TILE):
            cols = slice(part * SSM_NS + n * MXU_TILE, part * SSM_NS + (n + 1) * MXU_TILE)
            bu_ref[:, cols] = jnp.dot(u[:, n * kin:(n + 1) * kin], bbd_ref[n * kin:(n + 1) * kin, cols],
                                      preferred_element_type=F32)
    ar = a_ref[0:1, :]
    ai = a_ref[1:2, :]

    def step(t, carry):
        out = []
        for b in range(bsz):
            xr, xi = carry[b]
            row = pl.ds(b * chunk + t, 1)
            nr = ar * xr - ai * xi + bu_ref[row, 0:SSM_NS]
            ni = ar * xi + ai * xr + bu_ref[row, SSM_NS:2 * SSM_NS]
            x_refs[b][pl.ds(t, 1), 0:SSM_NS] = nr
            x_refs[b][pl.ds(t, 1), SSM_NS:2 * SSM_NS] = ni
            out.append((nr, ni))
        return tuple(out)

    init = tuple((st_ref[2 * b:2 * b + 1, :], st_ref[2 * b + 1:2 * b + 2, :]) for b in range(bsz))
    last = lax.fori_loop(0, chunk, step, init, unroll=8)
    for b in range(bsz):
        st_ref[2 * b:2 * b + 1, :] = last[b][0]
        st_ref[2 * b + 1:2 * b + 2, :] = last[b][1]

    spt = (MXU_TILE // SSM_GROUP) * SSM_STATE
    ys = []
    for m in range(SSM_W // MXU_TILE):
        out = slice(m * MXU_TILE, (m + 1) * MXU_TILE)
        acc = None
        for part in range(2):
            rows = slice(part * SSM_NS + m * spt, part * SSM_NS + (m + 1) * spt)
            xs = jnp.concatenate([x_refs[b][:, rows].astype(BF16) for b in range(bsz)], axis=0)
            d = jnp.dot(xs, cbd_ref[rows, out], preferred_element_type=F32)
            acc = d if acc is None else acc + d
        ys.append(acc)
    y = jnp.concatenate(ys, axis=1) + d_ref[...] * u.astype(F32)
    z = jax.nn.gelu(y)
    gate = jnp.dot(z.astype(BF16), wg_ref[...], preferred_element_type=F32)
    o_ref[...] = (z * jax.nn.sigmoid(gate)).astype(o_ref.dtype).reshape(bsz, chunk, SSM_W)


def _ssm(p3, bbd, cbd, a_bar, d_skip, w_glu):
    bsz, seq, _ = p3.shape
    chunk = _pick(seq, (256, 128))
    const = lambda shape: pl.BlockSpec(shape, lambda c: (0,) * len(shape))
    return pl.pallas_call(
        functools.partial(_ssm_kernel, bsz=bsz, chunk=chunk),
        grid=(seq // chunk,),
        in_specs=[pl.BlockSpec((bsz, chunk, SSM_W), lambda c: (0, c, OFF_SSM // SSM_W)),
                  const((SSM_W, 2 * SSM_NS)), const((2 * SSM_NS, SSM_W)),
                  const((2, SSM_NS)), const((1, SSM_W)), const((SSM_W, SSM_W))],
        out_specs=pl.BlockSpec((bsz, chunk, SSM_W), lambda c: (0, c, 0)),
        out_shape=jax.ShapeDtypeStruct((bsz, seq, SSM_W), BF16),
        scratch_shapes=([pltpu.VMEM((bsz * chunk, 2 * SSM_NS), F32), pltpu.VMEM((2 * bsz, SSM_NS), F32)]
                        + [pltpu.VMEM((chunk, 2 * SSM_NS), F32)] * bsz),
        compiler_params=_cparams(("arbitrary",)),
        name="ssm",
    )(p3, bbd, cbd, a_bar, d_skip, w_glu)


def _dsa_kernel(qi_ref, ws_ref, ki_ref, q_ref, k_ref, vt_ref, kn_ref, o_ref,
                key_ref, hkey_ref, mask_ref, qit_ref, d0_ref, d1_ref, qt_ref, s0_ref, s1_ref, s2_ref,
                p0_ref, p1_ref, p2_ref, acc_ref, shift_ref,
                *, tq, seq, ksel):
    qb = pl.program_id(1)
    t0 = qb * tq
    cw = tq
    nc = qb + 1
    ca = 2 * tq
    nca = (qb + 2) // 2
    ks = 128
    vs = 64
    kf = float(ksel)
    jbits = (seq - 1).bit_length()
    qidx = t0 + lax.broadcasted_iota(I32, (1, tq), 1)

    def _select():
        qit = qi_ref[...].astype(F32).T
        qit_ref[...] = jnp.concatenate(
            [qit[h * IDX_DIM:(h + 1) * IDX_DIM, :] for h in range(N_IDX_HEADS)], axis=1).astype(BF16)
        wt = ws_ref[...].T
        wrow = jnp.concatenate([wt[IDX_DIM + h:IDX_DIM + h + 1, :] for h in range(N_IDX_HEADS)], axis=1)

        def idx_dots(c, d_ref):
            start = pl.multiple_of(c * cw, cw)
            kib = ki_ref[pl.ds(start, cw), :][:, :IDX_DIM]
            d_ref[...] = jnp.dot(kib, qit_ref[...], preferred_element_type=F32)

        def idx_scores(c, d_ref):
            for i in range(cw // ks):
                start = pl.multiple_of(c * cw, cw) + i * ks
                r = jnp.maximum(d_ref[i * ks:(i + 1) * ks, :], 0.0) * wrow
                isc = r[:, 0:tq]
                for h in range(1, N_IDX_HEADS):
                    isc = isc + r[:, h * tq:(h + 1) * tq]
                kidx = start + lax.broadcasted_iota(I32, (ks, tq), 0)
                isc = jnp.where(isc == 0.0, 0.0, isc)
                isc = jnp.where(kidx <= qidx, isc, -jnp.inf)
                bits = pltpu.bitcast(isc, I32)
                key = bits ^ ((bits >> 31) & 0x7FFFFFFF)
                key_ref[pl.ds(start, ks), :] = key
                hkey_ref[pl.ds(start, ks), :] = (key >> 16).astype(I16)

        idx_dots(0, d0_ref)

        def idx_pair(c2, _):
            c = 2 * c2
            idx_dots(c + 1, d1_ref)
            idx_scores(c, d0_ref)
            idx_dots(jnp.minimum(c + 2, nc - 1), d0_ref)
            idx_scores(c + 1, d1_ref)
            return 0

        lax.fori_loop(0, nc // 2, idx_pair, 0)

        @pl.when(nc % 2 == 1)
        def _():
            idx_scores(nc - 1, d0_ref)

        @pl.when(qb % 2 == 0)
        def _():
            mask_ref[pl.ds(pl.multiple_of((qb + 1) * cw, cw), cw), :] = jnp.zeros((cw, tq), BF16)

        small = qidx < ksel

        def key_rows(c):
            return pl.multiple_of(c * cw, cw) + lax.broadcasted_iota(I32, (cw, tq), 0)

        def count(pred):
            def body(c, cnt):
                blk = key_ref[pl.ds(pl.multiple_of(c * cw, cw), cw), :]
                hit = pred(blk, c)
                return cnt + jnp.sum(hit.reshape(cw // 32, 32, tq), axis=0)
            cnt = lax.fori_loop(0, nc, body, jnp.zeros((32, tq), F32))
            return jnp.sum(cnt, axis=0, keepdims=True)

        def count_ge(cand):
            return count(lambda x, c: jnp.where(x >= cand, 1.0, 0.0))

        def count_half_ge(c16):
            def body(c, cnt):
                blk = hkey_ref[pl.ds(pl.multiple_of(c * cw, cw), cw), :]
                hit = jnp.where(blk >= c16, jnp.ones((), I16), jnp.zeros((), I16))
                for r in range(0, cw, 32):
                    cnt = cnt + hit[r:r + 32, :]
                return cnt
            cnt = lax.fori_loop(0, nc, body, jnp.zeros((32, tq), I16))
            return jnp.sum(cnt.astype(F32), axis=0, keepdims=True)

        def count_high_ge(cand):
            return count_half_ge((cand >> 16).astype(I16))

        def count_low_ge(cand):
            return count_half_ge(((cand & 0xFFFF) - 32768).astype(I16))

        c0 = count_high_ge(jnp.zeros((1, tq), I32))
        thr = jnp.where(c0 >= kf, 0, INT_MIN).astype(I32)
        done = jnp.where(jnp.logical_or(small, c0 == kf), 1.0, 0.0)

        def bit_step(i, thr, done, counter):
            cand = thr + jnp.left_shift(jnp.int32(1), 30 - i)
            cnt = counter(cand)
            return (jnp.where(cnt >= kf, cand, thr),
                    jnp.maximum(done, jnp.where(cnt == kf, 1.0, 0.0)))

        thr, done = lax.fori_loop(0, HIGH_BITS, lambda i, c: bit_step(i, *c, count_high_ge), (thr, done))

        t16 = (thr >> 16).astype(I16)

        def low_body(c, _):
            rows = pl.ds(pl.multiple_of(c * cw, cw), cw)
            h = hkey_ref[rows, :]
            low = ((key_ref[rows, :] & 0xFFFF) - 32768).astype(I16)
            hkey_ref[rows, :] = jnp.where(h == t16, low, jnp.where(h > t16, jnp.full((), 32767, I16),
                                                                    jnp.full((), -32768, I16)))
            return 0

        lax.fori_loop(0, nc, low_body, 0)

        def w_cond(c):
            return jnp.logical_and(c[0] < 31, c[3] < 0.5)

        def w_body(c):
            thr, done = bit_step(c[0], c[1], c[2], count_low_ge)
            thr, done = bit_step(c[0] + 1, thr, done, count_low_ge)
            return c[0] + 2, thr, done, jnp.min(done)

        _, thr, done, settled = lax.while_loop(
            w_cond, w_body, (jnp.int32(HIGH_BITS), thr, done, jnp.min(done)))
        thr = jnp.where(small, INT_MIN, thr)

        def tie_index():
            n_ge = count_ge(thr)
            need = kf - count_ge(thr + 1)
            tie = jnp.logical_and(jnp.logical_not(small), n_ge > kf)

            def count_tie_below(cand):
                return count(lambda x, c: jnp.where(
                    x == thr, jnp.where(key_rows(c) < cand, 1.0, 0.0), 0.0))

            def jbit_body(i, jj):
                cand = jj + jnp.left_shift(jnp.int32(1), jbits - 1 - i)
                return jnp.where(count_tie_below(cand) < need, cand, jj)

            jj = lax.fori_loop(0, jbits, jbit_body, jnp.zeros((1, tq), I32))
            return jnp.where(tie, jj, seq)

        jmax = lax.cond(settled < 0.5, tie_index, lambda: jnp.full((1, tq), seq, I32))

        def mask_body(c, _):
            rows = pl.ds(pl.multiple_of(c * cw, cw), cw)
            x = key_ref[rows, :]
            kidx = key_rows(c)
            b = jnp.where(x > thr, 1.0, jnp.where(x == thr, jnp.where(kidx <= jmax, 1.0, 0.0), 0.0))
            mask_ref[rows, :] = jnp.where(kidx <= qidx, b, 0.0).astype(BF16)
            return 0

        lax.fori_loop(0, nc, mask_body, 0)

    _select()

    kn_all = jnp.max(kn_ref[...], axis=(0, 1), keepdims=True)[0]
    lane = lax.broadcasted_iota(I32, (1, LANES), 1)
    qt_ref[...] = jnp.zeros(qt_ref.shape, BF16)
    for g in range(N_KV_HEADS):
        qt = q_ref[:, g * REP * HEAD_DIM:(g + 1) * REP * HEAD_DIM].astype(F32).T.astype(BF16)
        for r in range(REP):
            qt_ref[g, g * HEAD_DIM:(g + 1) * HEAD_DIM, r * tq:(r + 1) * tq] = qt[r * HEAD_DIM:(r + 1) * HEAD_DIM, :]
        qsq = qt.astype(F32) * qt.astype(F32)
        qn2 = jnp.concatenate([jnp.sum(qsq[r * HEAD_DIM:(r + 1) * HEAD_DIM, :], axis=0, keepdims=True)
                               for r in range(REP)], axis=1)
        kn2 = jnp.max(jnp.where(lane == g, kn_all, 0.0), axis=1, keepdims=True)
        shift_ref[g] = jnp.sqrt(qn2) * jnp.sqrt(kn2)
    acc_ref[...] = jnp.zeros(acc_ref.shape, F32)
    ones_rows = jnp.ones((ONES_ROWS, ca), BF16)

    n_items = N_KV_HEADS * nca

    def head_chunk(w):
        g = jnp.asarray(w, I32) // nca
        return g, w - g * nca

    def scores(w, s_ref):
        g, c = head_chunk(w)
        start = pl.multiple_of(c * ca, ca)
        s_ref[...] = jnp.dot(k_ref[pl.ds(start, ca), :], qt_ref[g], preferred_element_type=F32)

    def accumulate(g, c, scale, p_ref):
        vt = jnp.concatenate(
            [vt_ref[0, pl.ds(pl.multiple_of(g * HEAD_DIM, HEAD_DIM), HEAD_DIM),
                    pl.ds(pl.multiple_of(c * ca, ca), ca)], ones_rows], axis=0)
        acc_ref[g] = scale * acc_ref[g] + jnp.dot(vt, p_ref[...], preferred_element_type=F32)

    def mask_rows(c, i):
        return mask_ref[pl.ds(pl.multiple_of(c * ca, ca) + i * vs, vs), :]

    def softmax_chunk(w, s_ref, p_ref):
        g, c = head_chunk(w)
        shift = shift_ref[g]
        for i in range(ca // vs):
            p = jnp.exp2(s_ref[i * vs:(i + 1) * vs, :] - shift).astype(BF16)
            p_ref[i * vs:(i + 1) * vs, :] = p * jnp.tile(mask_rows(c, i), (1, REP))
        accumulate(g, c, 1.0, p_ref)

    scores(0, s0_ref)

    def att_triple(w3, _):
        w = 3 * w3
        scores(w + 1, s1_ref)
        softmax_chunk(w, s0_ref, p0_ref)
        scores(w + 2, s2_ref)
        softmax_chunk(w + 1, s1_ref, p1_ref)
        scores(jnp.minimum(w + 3, n_items - 1), s0_ref)
        softmax_chunk(w + 2, s2_ref, p2_ref)
        return 0

    lax.fori_loop(0, n_items // 3, att_triple, 0)
    rem = n_items % 3
    base = n_items - rem

    @pl.when(rem >= 1)
    def _():
        scores(jnp.minimum(base + 1, n_items - 1), s1_ref)
        softmax_chunk(base, s0_ref, p0_ref)

    @pl.when(rem == 2)
    def _():
        softmax_chunk(base + 1, s1_ref, p1_ref)

    def finish_head(g, _):
        @pl.when(jnp.min(acc_ref[g][HEAD_DIM:HEAD_DIM + 1, :]) < DENOM_FLOOR)
        def _():
            acc_ref[g] = jnp.zeros(acc_ref.shape[1:], F32)

            def exact_chunk(c, m_prev):
                scores(g * nca + c, s0_ref)
                m_new = m_prev
                for i in range(ca // vs):
                    b = jnp.where(jnp.tile(mask_rows(c, i), (1, REP)) > 0, s0_ref[i * vs:(i + 1) * vs, :], NEG_BIG)
                    s1_ref[i * vs:(i + 1) * vs, :] = b
                    m_new = jnp.maximum(m_new, jnp.max(b, axis=0, keepdims=True))
                for i in range(ca // vs):
                    p0_ref[i * vs:(i + 1) * vs, :] = jnp.exp2(s1_ref[i * vs:(i + 1) * vs, :] - m_new).astype(BF16)
                accumulate(g, c, jnp.exp2(m_prev - m_new), p0_ref)
                return m_new

            lax.fori_loop(0, nca, exact_chunk, jnp.full((1, REP * tq), NEG_BIG, F32))

        acc = acc_ref[g]
        out_t = acc[:HEAD_DIM] / acc[HEAD_DIM:HEAD_DIM + 1]
        out_t = jnp.concatenate([out_t[:, r * tq:(r + 1) * tq] for r in range(REP)], axis=0)
        o_ref[:, pl.ds(pl.multiple_of(g * REP * HEAD_DIM, REP * HEAD_DIM), REP * HEAD_DIM)] = (
            out_t.T.astype(o_ref.dtype))
        return 0

    lax.fori_loop(0, N_KV_HEADS, finish_head, 0)


def _dsa(qi_r, w_s, ki_r, q_r, k_r, v_t, k_n, bsz, seq):
    t = q_r.shape[0]
    tq = 256
    assert seq % (2 * tq) == 0
    ksel = min(TOPK_MAX, seq // 4)
    assert ksel <= tq
    nqb = seq // tq
    rowblk = lambda w: pl.BlockSpec((tq, w), lambda b, i: (b * nqb + i, 0))
    return pl.pallas_call(
        functools.partial(_dsa_kernel, tq=tq, seq=seq, ksel=ksel),
        grid=(bsz, nqb),
        in_specs=[rowblk(N_IDX_HEADS * IDX_DIM), rowblk(LANES),
                  pl.BlockSpec((seq, LANES), lambda b, i: (b, 0)),
                  rowblk(ATT_W),
                  pl.BlockSpec((seq, KV_W), lambda b, i: (b, 0)),
                  pl.BlockSpec((1, KV_W, seq), lambda b, i: (b, 0, 0)),
                  pl.BlockSpec((k_n.shape[0] // bsz, SUBLANES, LANES), lambda b, i: (b, 0, 0))],
        out_specs=rowblk(ATT_W),
        out_shape=jax.ShapeDtypeStruct((t, ATT_W), BF16),
        scratch_shapes=[pltpu.VMEM((seq, tq), I32),
                        pltpu.VMEM((seq, tq), I16),
                        pltpu.VMEM((seq, tq), BF16),
                        pltpu.VMEM((IDX_DIM, N_IDX_HEADS * tq), BF16),
                        pltpu.VMEM((tq, N_IDX_HEADS * tq), F32),
                        pltpu.VMEM((tq, N_IDX_HEADS * tq), F32),
                        pltpu.VMEM((N_KV_HEADS, KV_W, REP * tq), BF16),
                        pltpu.VMEM((2 * tq, REP * tq), F32),
                        pltpu.VMEM((2 * tq, REP * tq), F32),
                        pltpu.VMEM((2 * tq, REP * tq), F32),
                        pltpu.VMEM((2 * tq, REP * tq), BF16),
                        pltpu.VMEM((2 * tq, REP * tq), BF16),
                        pltpu.VMEM((2 * tq, REP * tq), BF16),
                        pltpu.VMEM((N_KV_HEADS, HEAD_DIM + ONES_ROWS, REP * tq), F32),
                        pltpu.VMEM((N_KV_HEADS, 1, REP * tq), F32)],
        compiler_params=_cparams(("parallel", "arbitrary")),
        name="dsa",
    )(qi_r, w_s, ki_r, q_r, k_r, v_t, k_n)


def _causal_conv3(cur, halo, w):
    tm = cur.shape[0]
    ext = jnp.concatenate([halo, cur], axis=0)
    return (w[2:3, :] * cur + w[1:2, :] * ext[SUBLANES - 1:SUBLANES - 1 + tm]
            + w[0:1, :] * ext[SUBLANES - 2:SUBLANES - 2 + tm])


def _merge_kernel(x_ref, ga_ref, gb_ref, gc_ref, scx_ref, scb_ref, scc_ref, hx_ref, hc_ref,
                  yb_ref, yc_ref, cw_ref, wa_ref, wb_ref, wc_ref, wo_ref, o_ref, *, tm, seq):
    first = (pl.program_id(0) * tm) % seq == 0
    cx = scc_ref[...].astype(F32) * scx_ref[...].astype(F32)
    halo = hc_ref[...].astype(F32) * hx_ref[...].astype(F32)
    halo = jnp.where(first, 0.0, halo)
    ya = scb_ref[...].astype(F32) * _causal_conv3(cx, halo, cw_ref[...])
    dot = lambda a, w: jnp.dot(a, w[...], preferred_element_type=F32)
    m = jax.nn.sigmoid(ga_ref[...].astype(F32)) * dot(ya.astype(BF16), wa_ref)
    m = m + jax.nn.sigmoid(gb_ref[...].astype(F32)) * dot(yb_ref[...], wb_ref)
    m = m + jax.nn.sigmoid(gc_ref[...].astype(F32)) * dot(yc_ref[...], wc_ref)
    o_ref[...] = x_ref[...] + dot(m.astype(BF16), wo_ref)


def _merge(x2, p2, y_b, y_c, conv_w, w_a, w_b, w_c, w_o, seq):
    t, d = x2.shape
    tm = _pick(seq, (256, 128))
    hb = tm // SUBLANES

    def col(width, off):
        return pl.BlockSpec((tm, width), lambda i, o=off // width: (i, o))

    def halo(off):
        return pl.BlockSpec((SUBLANES, SC_W), lambda i, o=off // SC_W: (jnp.maximum(i * hb - 1, 0), o))

    def const(shape):
        return pl.BlockSpec(shape, lambda i: (0, 0), pipeline_mode=pl.Buffered(1))

    row = lambda w: pl.BlockSpec((tm, w), lambda i: (i, 0))
    return pl.pallas_call(
        functools.partial(_merge_kernel, tm=tm, seq=seq),
        grid=(t // tm,),
        in_specs=[row(d), col(d, 0), col(d, d), col(d, 2 * d),
                  col(SC_W, OFF_SCX), col(SC_W, OFF_SCB), col(SC_W, OFF_SCC),
                  halo(OFF_SCX), halo(OFF_SCC), row(SSM_W), row(ATT_W),
                  const((SC_CONV, SC_W)), const((SC_W, d)), const((SSM_W, d)),
                  const((ATT_W, d)), const((d, d))],
        out_specs=row(d),
        out_shape=jax.ShapeDtypeStruct((t, d), F32),
        compiler_params=_cparams(("parallel",)),
        name="merge",
    )(x2, p2, p2, p2, p2, p2, p2, p2, p2, y_b, y_c, conv_w, w_a, w_b, w_c, w_o)


def _ffn_kernel(x_ref, hx_ref, g_ref, wg_ref, wu_ref, cg_ref, cu_ref, wd_ref, og_ref, o_ref,
                h_ref, *, tm, seq, out_norm):
    j = pl.program_id(1)

    @pl.when(j == 0)
    def _():
        first = (pl.program_id(0) * tm) % seq == 0
        x = jnp.concatenate([hx_ref[...], x_ref[...]], axis=0)
        ms = jnp.mean(x * x, axis=-1, keepdims=True)
        h = x * lax.rsqrt(ms + NORM_EPS) * g_ref[...]
        rows = lax.broadcasted_iota(I32, (tm + SUBLANES, 1), 0)
        h = jnp.where(jnp.logical_and(first, rows < SUBLANES), 0.0, h)
        h_ref[...] = h.astype(BF16)
        o_ref[...] = jnp.zeros_like(o_ref)

    h = h_ref[...]
    gt = jnp.dot(h, wg_ref[...], preferred_element_type=F32)
    ut = jnp.dot(h, wu_ref[...], preferred_element_type=F32)
    gc = _causal_conv3(gt[SUBLANES:], gt[:SUBLANES], cg_ref[...])
    uc = _causal_conv3(ut[SUBLANES:], ut[:SUBLANES], cu_ref[...])
    act = (jax.nn.silu(gc) * uc).astype(BF16)
    o_ref[...] += jnp.dot(act, wd_ref[...], preferred_element_type=F32)

    @pl.when(j == pl.num_programs(1) - 1)
    def _():
        y = x_ref[...] + o_ref[...]
        if out_norm:
            ms = jnp.mean(y * y, axis=-1, keepdims=True)
            y = y * lax.rsqrt(ms + NORM_EPS) * og_ref[...]
        o_ref[...] = y


def _ffn(x2, gain, w_up, conv_w, w_down, out_gain, seq, out_norm):
    t, d = x2.shape
    dff = w_down.shape[0]
    tm = _pick(seq, (512, 256, 128))
    tf = _pick(dff, (512, 256, 128))
    nf = dff // tf
    hb = tm // SUBLANES
    return pl.pallas_call(
        functools.partial(_ffn_kernel, tm=tm, seq=seq, out_norm=out_norm),
        grid=(t // tm, nf),
        in_specs=[pl.BlockSpec((tm, d), lambda i, j: (i, 0)),
                  pl.BlockSpec((SUBLANES, d), lambda i, j: (jnp.maximum(i * hb - 1, 0), 0)),
                  pl.BlockSpec((1, d), lambda i, j: (0, 0)),
                  pl.BlockSpec((d, tf), lambda i, j: (0, j)),
                  pl.BlockSpec((d, tf), lambda i, j: (0, j + nf)),
                  pl.BlockSpec((FFN_CONV, tf), lambda i, j: (0, j)),
                  pl.BlockSpec((FFN_CONV, tf), lambda i, j: (0, j + nf)),
                  pl.BlockSpec((tf, d), lambda i, j: (j, 0)),
                  pl.BlockSpec((1, d), lambda i, j: (0, 0))],
        out_specs=pl.BlockSpec((tm, d), lambda i, j: (i, 0)),
        out_shape=jax.ShapeDtypeStruct((t, d), F32),
        scratch_shapes=[pltpu.VMEM((tm + SUBLANES, d), BF16)],
        compiler_params=_cparams(("parallel", "arbitrary")),
        name="ffn",
    )(x2, x2, gain, w_up, w_up, conv_w, conv_w, w_down, out_gain)


def _prep_w_in(w):
    d = w.shape[0]
    pad = jnp.zeros((d, NP_COLS - w.shape[1]), BF16)
    return jnp.concatenate([w[:, _ORIG_GATE_OFF:].astype(BF16), w[:, :_ORIG_GATE_OFF].astype(BF16), pad], axis=1)


def _prep_ssm(a_re, a_im, log_dt, b_re, b_im, c_re, c_im):
    ar, ai = a_re.astype(F32), a_im.astype(F32)
    dt = jnp.exp(log_dt.astype(F32))[:, None]
    mag = jnp.exp(dt * ar)
    abr, abi = mag * jnp.cos(dt * ai), mag * jnp.sin(dt * ai)
    den = ar * ar + ai * ai
    cr = ((abr - 1.0) * ar + abi * ai) / den
    ci = (abi * ar - (abr - 1.0) * ai) / den
    br, bi = b_re.astype(F32), b_im.astype(F32)
    bbr = cr[..., None] * br - ci[..., None] * bi
    bbi = cr[..., None] * bi + ci[..., None] * br
    eye = jnp.eye(SSM_GROUPS, dtype=F32)

    def bdiag_in(m):
        return (jnp.transpose(m, (0, 2, 1))[:, :, None, :] * eye[:, None, :, None]).reshape(
            SSM_W, SSM_NS)

    def bdiag_out(m):
        return (jnp.transpose(m, (0, 2, 1))[:, :, None, :] * eye[:, None, :, None]).reshape(
            SSM_NS, SSM_W)

    bbd = jnp.concatenate([bdiag_in(bbr), bdiag_in(bbi)], axis=1).astype(BF16)
    cbd = jnp.concatenate([bdiag_out(c_re.astype(F32)), bdiag_out(-c_im.astype(F32))], axis=0).astype(BF16)
    a_rows = jnp.stack([abr.reshape(-1), abi.reshape(-1)], axis=0)
    return bbd, cbd, a_rows


def kernel(x, positions, norm_mix, w_in, sc_conv, ssm_a_re, ssm_a_im, ssm_log_dt, ssm_b_re, ssm_b_im,
           ssm_c_re, ssm_c_im, ssm_d, ssm_glu, w_branch_a, w_branch_b, w_branch_c, w_out, norm_ffn,
           w_up, ffn_conv, w_down, norm_final):
    bsz, seq, d = x.shape
    depth = w_in.shape[0]
    t = bsz * seq
    x2 = x.reshape(t, d).astype(F32)
    tabs = _rope_tables(positions.astype(I32))
    for l in range(depth):
        p2 = _inproj(x2, norm_mix[l].reshape(1, d).astype(F32), _prep_w_in(w_in[l]))
        q_r, k_r, v_t, qi_r, ki_r, w_s, k_n = _rope(p2, tabs, bsz, seq)
        bbd, cbd, a_rows = _prep_ssm(ssm_a_re[l], ssm_a_im[l], ssm_log_dt[l], ssm_b_re[l], ssm_b_im[l],
                                     ssm_c_re[l], ssm_c_im[l])
        y_b = _ssm(p2.reshape(bsz, seq, NP_COLS), bbd, cbd, a_rows,
                   ssm_d[l].reshape(1, SSM_W).astype(F32), ssm_glu[l].astype(BF16))
        y_c = _dsa(qi_r, w_s, ki_r, q_r, k_r, v_t, k_n, bsz, seq)
        x2 = _merge(x2, p2, y_b.reshape(t, SSM_W), y_c, sc_conv[l].astype(F32),
                    w_branch_a[l].astype(BF16), w_branch_b[l].astype(BF16), w_branch_c[l].astype(BF16),
                    w_out[l].astype(BF16), seq)
        x2 = _ffn(x2, norm_ffn[l].reshape(1, d).astype(F32), w_up[l].astype(BF16),
                  ffn_conv[l].astype(F32), w_down[l].astype(BF16),
                  norm_final.reshape(1, d).astype(F32), seq, out_norm=(l == depth - 1))
    return x2.reshape(bsz, seq, d).astype(x.dtype)
```

```python
import functools
import math

import jax
import jax.numpy as jnp
from jax import lax
from jax.experimental import pallas as pl
from jax.experimental.pallas import tpu as pltpu

F32 = jnp.float32
BF16 = jnp.bfloat16
I32 = jnp.int32
I16 = jnp.int16

SC_W = 512
SC_CONV = 3
SSM_W = 512
SSM_GROUP = 16
SSM_GROUPS = SSM_W // SSM_GROUP
SSM_STATE = 64
SSM_NS = SSM_GROUPS * SSM_STATE
N_Q_HEADS = 16
N_KV_HEADS = 4
HEAD_DIM = 64
ATT_W = N_Q_HEADS * HEAD_DIM
KV_W = N_KV_HEADS * HEAD_DIM
REP = N_Q_HEADS // N_KV_HEADS
ROT_DIM = HEAD_DIM // 4
ROT_HALF = ROT_DIM // 2
ROPE_THETA = 500000.0
N_IDX_HEADS = 8
IDX_DIM = 64
TOPK_MAX = 256
D_FF = 5632
FFN_CONV = 3
NORM_EPS = 1e-6

LANES = 128
SUBLANES = 8
ONES_ROWS = 16
MXU_TILE = 256
VMEM_LIMIT = 56 * 1024 * 1024
NEG_BIG = -1e30
LOG2E = math.log2(math.e)
DENOM_FLOOR = 2.0 ** -64
INT_MIN = -(2 ** 31)
HIGH_BITS = 15

_ORIG_GATE_OFF = 3 * SC_W + SSM_W + ATT_W + 2 * KV_W + N_IDX_HEADS * IDX_DIM + IDX_DIM + N_IDX_HEADS
OFF_SCX = 3 * 2048
OFF_SCB = OFF_SCX + SC_W
OFF_SCC = OFF_SCB + SC_W
OFF_SSM = OFF_SCC + SC_W
OFF_Q = OFF_SSM + SSM_W
OFF_K = OFF_Q + ATT_W
OFF_V = OFF_K + KV_W
OFF_QI = OFF_V + KV_W
OFF_KI = OFF_QI + N_IDX_HEADS * IDX_DIM
INPROJ_TN = 7 * 256
NP_COLS = -(-(OFF_KI + LANES) // INPROJ_TN) * INPROJ_TN


def _cparams(sem, vmem=VMEM_LIMIT):
    return pltpu.CompilerParams(dimension_semantics=sem, vmem_limit_bytes=vmem)


def _pick(n, prefs):
    for p in prefs:
        if n % p == 0:
            return p
    return n


def _inproj_kernel(x_ref, g_ref, w_ref, o_ref, h_ref):
    @pl.when(pl.program_id(1) == 0)
    def _():
        x = x_ref[...]
        ms = jnp.mean(x * x, axis=-1, keepdims=True)
        h_ref[...] = (x * lax.rsqrt(ms + NORM_EPS) * g_ref[...]).astype(BF16)

    o_ref[...] = jnp.dot(h_ref[...], w_ref[...], preferred_element_type=F32).astype(o_ref.dtype)


def _inproj(x2, gain, w):
    t, d = x2.shape
    n = w.shape[1]
    tm = _pick(t, (1024, 512, 256, 128))
    tn = _pick(n, (INPROJ_TN, 384, 128))
    return pl.pallas_call(
        _inproj_kernel,
        grid=(t // tm, n // tn),
        in_specs=[pl.BlockSpec((tm, d), lambda i, j: (i, 0)),
                  pl.BlockSpec((1, d), lambda i, j: (0, 0)),
                  pl.BlockSpec((d, tn), lambda i, j: (0, j))],
        out_specs=pl.BlockSpec((tm, tn), lambda i, j: (i, j)),
        out_shape=jax.ShapeDtypeStruct((t, n), BF16),
        scratch_shapes=[pltpu.VMEM((tm, d), BF16)],
        compiler_params=_cparams(("parallel", "arbitrary")),
        name="inproj",
    )(x2, gain, w)


def _rope_table_kernel(pos_ref, invf_ref, a_ref, bm_ref, bp_ref):
    ang = pos_ref[...].astype(F32) * invf_ref[...]
    c = jnp.cos(ang)
    s = jnp.sin(ang)
    r = lax.broadcasted_iota(I32, ang.shape, 1) % HEAD_DIM
    lo = r < ROT_HALF
    hi = jnp.logical_and(r >= ROT_HALF, r < ROT_DIM)
    a_ref[...] = jnp.where(r < ROT_DIM, c, 1.0)
    bm_ref[...] = jnp.where(lo, -s, 0.0)
    bp_ref[...] = jnp.where(hi, s, 0.0)


def _rope_tables(positions):
    t = positions.size
    posb = jnp.broadcast_to(positions.reshape(t, 1), (t, LANES))
    inv_freq = ROPE_THETA ** (-jnp.arange(0, ROT_DIM, 2, dtype=F32) / ROT_DIM)
    lane = jnp.arange(LANES) % ROT_HALF
    invf = inv_freq[lane].reshape(1, LANES)
    tm = _pick(t, (1024, 512, 256, 128))
    spec = pl.BlockSpec((tm, LANES), lambda i: (i, 0))
    return pl.pallas_call(
        _rope_table_kernel,
        grid=(t // tm,),
        in_specs=[spec, pl.BlockSpec((1, LANES), lambda i: (0, 0))],
        out_specs=[spec, spec, spec],
        out_shape=[jax.ShapeDtypeStruct((t, LANES), F32)] * 3,
        compiler_params=_cparams(("parallel",)),
        name="rope_tables",
    )(posb, invf)


def _rope_kernel(q_ref, k_ref, v_ref, qi_ref, kiw_ref, a_ref, bm_ref, bp_ref,
                 qo_ref, ko_ref, vto_ref, qio_ref, kio_ref, wo_ref, kn_ref):
    a = a_ref[...]
    bm = bm_ref[...]
    bp = bp_ref[...]

    def rope(x):
        w = x.shape[1]
        reps = w // LANES
        xf = x.astype(F32)
        up = pltpu.roll(xf, w - ROT_HALF, 1)
        dn = pltpu.roll(xf, ROT_HALF, 1)
        return (xf * jnp.tile(a, (1, reps)) + up * jnp.tile(bm, (1, reps))
                + dn * jnp.tile(bp, (1, reps)))

    qo_ref[...] = (rope(q_ref[...]) * (HEAD_DIM ** -0.5 * LOG2E)).astype(BF16)
    kr = rope(k_ref[...]).astype(BF16)
    ko_ref[...] = kr
    ksq = kr.astype(F32) * kr.astype(F32)
    lane = lax.broadcasted_iota(I32, (1, LANES), 1)
    kn = jnp.zeros((1, LANES), F32)
    for g in range(N_KV_HEADS):
        n2 = jnp.sum(ksq[:, g * HEAD_DIM:(g + 1) * HEAD_DIM], axis=1, keepdims=True)
        kn = jnp.where(lane == g, jnp.max(n2, axis=0, keepdims=True), kn)
    kn_ref[0] = jnp.broadcast_to(kn, (SUBLANES, LANES))
    vto_ref[0] = v_ref[...].astype(F32).T.astype(BF16)
    qio_ref[...] = (rope(qi_ref[...]) * (IDX_DIM ** -0.5)).astype(BF16)
    kiw = kiw_ref[...]
    kio_ref[...] = rope(kiw).astype(BF16)
    wo_ref[...] = kiw.astype(F32) * (N_IDX_HEADS ** -0.5)


def _rope(p2, tabs, bsz, seq):
    t = p2.shape[0]
    tm = _pick(seq, (512, 256, 128))
    nsb = seq // tm
    a, bm, bp = tabs

    def col(width, off):
        return pl.BlockSpec((tm, width), lambda i, o=off // width: (i, o))

    tab = pl.BlockSpec((tm, LANES), lambda i: (i, 0))
    row = lambda w: pl.BlockSpec((tm, w), lambda i: (i, 0))
    return pl.pallas_call(
        _rope_kernel,
        grid=(t // tm,),
        in_specs=[col(ATT_W, OFF_Q), col(KV_W, OFF_K), col(KV_W, OFF_V),
                  col(N_IDX_HEADS * IDX_DIM, OFF_QI), col(LANES, OFF_KI), tab, tab, tab],
        out_specs=[row(ATT_W), row(KV_W),
                   pl.BlockSpec((1, KV_W, tm), lambda i: (i // nsb, 0, i % nsb)),
                   row(N_IDX_HEADS * IDX_DIM), row(LANES), row(LANES),
                   pl.BlockSpec((1, SUBLANES, LANES), lambda i: (i, 0, 0))],
        out_shape=[jax.ShapeDtypeStruct((t, ATT_W), BF16),
                   jax.ShapeDtypeStruct((t, KV_W), BF16),
                   jax.ShapeDtypeStruct((bsz, KV_W, seq), BF16),
                   jax.ShapeDtypeStruct((t, N_IDX_HEADS * IDX_DIM), BF16),
                   jax.ShapeDtypeStruct((t, LANES), BF16),
                   jax.ShapeDtypeStruct((t, LANES), F32),
                   jax.ShapeDtypeStruct((t // tm, SUBLANES, LANES), F32)],
        compiler_params=_cparams(("parallel",)),
        name="rope",
    )(p2, p2, p2, p2, p2, a, bm, bp)


def _ssm_kernel(u_ref, bbd_ref, cbd_ref, a_ref, d_ref, wg_ref, o_ref, bu_ref, st_ref, *x_refs, bsz, chunk):
    @pl.when(pl.program_id(0) == 0)
    def _():
        st_ref[...] = jnp.zeros_like(st_ref)

    u = u_ref[...].reshape(bsz * chunk, SSM_W)
    gpt = MXU_TILE // SSM_STATE
    kin = gpt * SSM_GROUP
    for part in range(2):
        for n in range(SSM_NS // MXU_TILE):
            cols = slice(part * SSM_NS + n * MXU_TILE, part * SSM_NS + (n + 1) * MXU_TILE)
            bu_ref[:, cols] = jnp.dot(u[:, n * kin:(n + 1) * kin], bbd_ref[n * kin:(n + 1) * kin, cols],
                                      preferred_element_type=F32)
    ar = a_ref[0:1, :]
    ai = a_ref[1:2, :]

    def step(t, carry):
        out = []
        for b in range(bsz):
            xr, xi = carry[b]
            row = pl.ds(b * chunk + t, 1)
            nr = ar * xr - ai * xi + bu_ref[row, 0:SSM_NS]
            ni = ar * xi + ai * xr + bu_ref[row, SSM_NS:2 * SSM_NS]
            x_refs[b][pl.ds(t, 1), 0:SSM_NS] = nr
            x_refs[b][pl.ds(t, 1), SSM_NS:2 * SSM_NS] = ni
            out.append((nr, ni))
        return tuple(out)

    init = tuple((st_ref[2 * b:2 * b + 1, :], st_ref[2 * b + 1:2 * b + 2, :]) for b in range(bsz))
    last = lax.fori_loop(0, chunk, step, init, unroll=8)
    for b in range(bsz):
        st_ref[2 * b:2 * b + 1, :] = last[b][0]
        st_ref[2 * b + 1:2 * b + 2, :] = last[b][1]

    spt = (MXU_TILE // SSM_GROUP) * SSM_STATE
    ys = []
    for m in range(SSM_W // MXU_TILE):
        out = slice(m * MXU_TILE, (m + 1) * MXU_TILE)
        acc = None
        for part in range(2):
            rows = slice(part * SSM_NS + m * spt, part * SSM_NS + (m + 1) * spt)
            xs = jnp.concatenate([x_refs[b][:, rows].astype(BF16) for b in range(bsz)], axis=0)
            d = jnp.dot(xs, cbd_ref[rows, out], preferred_element_type=F32)
            acc = d if acc is None else acc + d
        ys.append(acc)
    y = jnp.concatenate(ys, axis=1) + d_ref[...] * u.astype(F32)
    z = jax.nn.gelu(y)
    gate = jnp.dot(z.astype(BF16), wg_ref[...], preferred_element_type=F32)
    o_ref[...] = (z * jax.nn.sigmoid(gate)).astype(o_ref.dtype).reshape(bsz, chunk, SSM_W)


def _ssm(p3, bbd, cbd, a_bar, d_skip, w_glu):
    bsz, seq, _ = p3.shape
    chunk = _pick(seq, (256, 128))
    const = lambda shape: pl.BlockSpec(shape, lambda c: (0,) * len(shape))
    return pl.pallas_call(
        functools.partial(_ssm_kernel, bsz=bsz, chunk=chunk),
        grid=(seq // chunk,),
        in_specs=[pl.BlockSpec((bsz, chunk, SSM_W), lambda c: (0, c, OFF_SSM // SSM_W)),
                  const((SSM_W, 2 * SSM_NS)), const((2 * SSM_NS, SSM_W)),
                  const((2, SSM_NS)), const((1, SSM_W)), const((SSM_W, SSM_W))],
        out_specs=pl.BlockSpec((bsz, chunk, SSM_W), lambda c: (0, c, 0)),
        out_shape=jax.ShapeDtypeStruct((bsz, seq, SSM_W), BF16),
        scratch_shapes=([pltpu.VMEM((bsz * chunk, 2 * SSM_NS), F32), pltpu.VMEM((2 * bsz, SSM_NS), F32)]
                        + [pltpu.VMEM((chunk, 2 * SSM_NS), F32)] * bsz),
        compiler_params=_cparams(("arbitrary",)),
        name="ssm",
    )(p3, bbd, cbd, a_bar, d_skip, w_glu)


def _dsa_kernel(qi_ref, ws_ref, ki_ref, q_ref, k_ref, vt_ref, kn_ref, o_ref,
                key_ref, hkey_ref, mask_ref, qit_ref, d0_ref, d1_ref, qt_ref, s0_ref, s1_ref, s2_ref,
                p0_ref, p1_ref, p2_ref, acc_ref, shift_ref,
                *, tq, seq, ksel):
    qb = pl.program_id(1)
    t0 = qb * tq
    cw = tq
    nc = qb + 1
    ca = 2 * tq
    nca = (qb + 2) // 2
    ks = 128
    vs = 64
    kf = float(ksel)
    jbits = (seq - 1).bit_length()
    qidx = t0 + lax.broadcasted_iota(I32, (1, tq), 1)

    def _select():
        qit = qi_ref[...].astype(F32).T
        qit_ref[...] = jnp.concatenate(
            [qit[h * IDX_DIM:(h + 1) * IDX_DIM, :] for h in range(N_IDX_HEADS)], axis=1).astype(BF16)
        wt = ws_ref[...].T
        wrow = jnp.concatenate([wt[IDX_DIM + h:IDX_DIM + h + 1, :] for h in range(N_IDX_HEADS)], axis=1)

        def idx_dots(c, d_ref):
            start = pl.multiple_of(c * cw, cw)
            kib = ki_ref[pl.ds(start, cw), :][:, :IDX_DIM]
            d_ref[...] = jnp.dot(kib, qit_ref[...], preferred_element_type=F32)

        def idx_scores(c, d_ref):
            for i in range(cw // ks):
                start = pl.multiple_of(c * cw, cw) + i * ks
                r = jnp.maximum(d_ref[i * ks:(i + 1) * ks, :], 0.0) * wrow
                isc = r[:, 0:tq]
                for h in range(1, N_IDX_HEADS):
                    isc = isc + r[:, h * tq:(h + 1) * tq]
                kidx = start + lax.broadcasted_iota(I32, (ks, tq), 0)
                isc = jnp.where(isc == 0.0, 0.0, isc)
                isc = jnp.where(kidx <= qidx, isc, -jnp.inf)
                bits = pltpu.bitcast(isc, I32)
                key = bits ^ ((bits >> 31) & 0x7FFFFFFF)
                key_ref[pl.ds(start, ks), :] = key
                hkey_ref[pl.ds(start, ks), :] = (key >> 16).astype(I16)

        idx_dots(0, d0_ref)

        def idx_pair(c2, _):
            c = 2 * c2
            idx_dots(c + 1, d1_ref)
            idx_scores(c, d0_ref)
            idx_dots(jnp.minimum(c + 2, nc - 1), d0_ref)
            idx_scores(c + 1, d1_ref)
            return 0

        lax.fori_loop(0, nc // 2, idx_pair, 0)

        @pl.when(nc % 2 == 1)
        def _():
            idx_scores(nc - 1, d0_ref)

        @pl.when(qb % 2 == 0)
        def _():
            mask_ref[pl.ds(pl.multiple_of((qb + 1) * cw, cw), cw), :] = jnp.zeros((cw, tq), BF16)

        small = qidx < ksel

        def key_rows(c):
            return pl.multiple_of(c * cw, cw) + lax.broadcasted_iota(I32, (cw, tq), 0)

        def count(pred):
            def body(c, cnt):
                blk = key_ref[pl.ds(pl.multiple_of(c * cw, cw), cw), :]
                hit = pred(blk, c)
                return cnt + jnp.sum(hit.reshape(cw // 32, 32, tq), axis=0)
            cnt = lax.fori_loop(0, nc, body, jnp.zeros((32, tq), F32))
            return jnp.sum(cnt, axis=0, keepdims=True)

        def count_ge(cand):
            return count(lambda x, c: jnp.where(x >= cand, 1.0, 0.0))

        def count_half_ge(c16):
            def body(c, cnt):
                blk = hkey_ref[pl.ds(pl.multiple_of(c * cw, cw), cw), :]
                hit = jnp.where(blk >= c16, jnp.ones((), I16), jnp.zeros((), I16))
                for r in range(0, cw, 32):
                    cnt = cnt + hit[r:r + 32, :]
                return cnt
            cnt = lax.fori_loop(0, nc, body, jnp.zeros((32, tq), I16))
            return jnp.sum(cnt.astype(F32), axis=0, keepdims=True)

        def count_high_ge(cand):
            return count_half_ge((cand >> 16).astype(I16))

        def count_low_ge(cand):
            return count_half_ge(((cand & 0xFFFF) - 32768).astype(I16))

        c0 = count_high_ge(jnp.zeros((1, tq), I32))
        thr = jnp.where(c0 >= kf, 0, INT_MIN).astype(I32)
        done = jnp.where(jnp.logical_or(small, c0 == kf), 1.0, 0.0)

        def bit_step(i, thr, done, counter):
            cand = thr + jnp.left_shift(jnp.int32(1), 30 - i)
            cnt = counter(cand)
            return (jnp.where(cnt >= kf, cand, thr),
                    jnp.maximum(done, jnp.where(cnt == kf, 1.0, 0.0)))

        thr, done = lax.fori_loop(0, HIGH_BITS, lambda i, c: bit_step(i, *c, count_high_ge), (thr, done))

        t16 = (thr >> 16).astype(I16)

        def low_body(c, _):
            rows = pl.ds(pl.multiple_of(c * cw, cw), cw)
            h = hkey_ref[rows, :]
            low = ((key_ref[rows, :] & 0xFFFF) - 32768).astype(I16)
            hkey_ref[rows, :] = jnp.where(h == t16, low, jnp.where(h > t16, jnp.full((), 32767, I16),
                                                                    jnp.full((), -32768, I16)))
            return 0

        lax.fori_loop(0, nc, low_body, 0)

        def w_cond(c):
            return jnp.logical_and(c[0] < 31, c[3] < 0.5)

        def w_body(c):
            thr, done = bit_step(c[0], c[1], c[2], count_low_ge)
            thr, done = bit_step(c[0] + 1, thr, done, count_low_ge)
            return c[0] + 2, thr, done, jnp.min(done)

        _, thr, done, settled = lax.while_loop(
            w_cond, w_body, (jnp.int32(HIGH_BITS), thr, done, jnp.min(done)))
        thr = jnp.where(small, INT_MIN, thr)

        def tie_index():
            n_ge = count_ge(thr)
            need = kf - count_ge(thr + 1)
            tie = jnp.logical_and(jnp.logical_not(small), n_ge > kf)

            def count_tie_below(cand):
                return count(lambda x, c: jnp.where(
                    x == thr, jnp.where(key_rows(c) < cand, 1.0, 0.0), 0.0))

            def jbit_body(i, jj):
                cand = jj + jnp.left_shift(jnp.int32(1), jbits - 1 - i)
                return jnp.where(count_tie_below(cand) < need, cand, jj)

            jj = lax.fori_loop(0, jbits, jbit_body, jnp.zeros((1, tq), I32))
            return jnp.where(tie, jj, seq)

        jmax = lax.cond(settled < 0.5, tie_index, lambda: jnp.full((1, tq), seq, I32))

        def mask_body(c, _):
            rows = pl.ds(pl.multiple_of(c * cw, cw), cw)
            x = key_ref[rows, :]
            kidx = key_rows(c)
            b = jnp.where(x > thr, 1.0, jnp.where(x == thr, jnp.where(kidx <= jmax, 1.0, 0.0), 0.0))
            mask_ref[rows, :] = jnp.where(kidx <= qidx, b, 0.0).astype(BF16)
            return 0

        lax.fori_loop(0, nc, mask_body, 0)

    _select()

    kn_all = jnp.max(kn_ref[...], axis=(0, 1), keepdims=True)[0]
    lane = lax.broadcasted_iota(I32, (1, LANES), 1)
    qt_ref[...] = jnp.zeros(qt_ref.shape, BF16)
    for g in range(N_KV_HEADS):
        qt = q_ref[:, g * REP * HEAD_DIM:(g + 1) * REP * HEAD_DIM].astype(F32).T.astype(BF16)
        for r in range(REP):
            qt_ref[g, g * HEAD_DIM:(g + 1) * HEAD_DIM, r * tq:(r + 1) * tq] = qt[r * HEAD_DIM:(r + 1) * HEAD_DIM, :]
        qsq = qt.astype(F32) * qt.astype(F32)
        qn2 = jnp.concatenate([jnp.sum(qsq[r * HEAD_DIM:(r + 1) * HEAD_DIM, :], axis=0, keepdims=True)
                               for r in range(REP)], axis=1)
        kn2 = jnp.max(jnp.where(lane == g, kn_all, 0.0), axis=1, keepdims=True)
        shift_ref[g] = jnp.sqrt(qn2) * jnp.sqrt(kn2)
    acc_ref[...] = jnp.zeros(acc_ref.shape, F32)
    ones_rows = jnp.ones((ONES_ROWS, ca), BF16)

    n_items = N_KV_HEADS * nca

    def head_chunk(w):
        g = jnp.asarray(w, I32) // nca
        return g, w - g * nca

    def scores(w, s_ref):
        g, c = head_chunk(w)
        start = pl.multiple_of(c * ca, ca)
        s_ref[...] = jnp.dot(k_ref[pl.ds(start, ca), :], qt_ref[g], preferred_element_type=F32)

    def accumulate(g, c, scale, p_ref):
        vt = jnp.concatenate(
            [vt_ref[0, pl.ds(pl.multiple_of(g * HEAD_DIM, HEAD_DIM), HEAD_DIM),
                    pl.ds(pl.multiple_of(c * ca, ca), ca)], ones_rows], axis=0)
        acc_ref[g] = scale * acc_ref[g] + jnp.dot(vt, p_ref[...], preferred_element_type=F32)

    def mask_rows(c, i):
        return mask_ref[pl.ds(pl.multiple_of(c * ca, ca) + i * vs, vs), :]

    def softmax_chunk(w, s_ref, p_ref):
        g, c = head_chunk(w)
        shift = shift_ref[g]
        for i in range(ca // vs):
            p = jnp.exp2(s_ref[i * vs:(i + 1) * vs, :] - shift).astype(BF16)
            p_ref[i * vs:(i + 1) * vs, :] = p * jnp.tile(mask_rows(c, i), (1, REP))
        accumulate(g, c, 1.0, p_ref)

    scores(0, s0_ref)

    def att_triple(w3, _):
        w = 3 * w3
        scores(w + 1, s1_ref)
        softmax_chunk(w, s0_ref, p0_ref)
        scores(w + 2, s2_ref)
        softmax_chunk(w + 1, s1_ref, p1_ref)
        scores(jnp.minimum(w + 3, n_items - 1), s0_ref)
        softmax_chunk(w + 2, s2_ref, p2_ref)
        return 0

    lax.fori_loop(0, n_items // 3, att_triple, 0)
    rem = n_items % 3
    base = n_items - rem

    @pl.when(rem >= 1)
    def _():
        scores(jnp.minimum(base + 1, n_items - 1), s1_ref)
        softmax_chunk(base, s0_ref, p0_ref)

    @pl.when(rem == 2)
    def _():
        softmax_chunk(base + 1, s1_ref, p1_ref)

    def finish_head(g, _):
        @pl.when(jnp.min(acc_ref[g][HEAD_DIM:HEAD_DIM + 1, :]) < DENOM_FLOOR)
        def _():
            acc_ref[g] = jnp.zeros(acc_ref.shape[1:], F32)

            def exact_chunk(c, m_prev):
                scores(g * nca + c, s0_ref)
                m_new = m_prev
                for i in range(ca // vs):
                    b = jnp.where(jnp.tile(mask_rows(c, i), (1, REP)) > 0, s0_ref[i * vs:(i + 1) * vs, :], NEG_BIG)
                    s1_ref[i * vs:(i + 1) * vs, :] = b
                    m_new = jnp.maximum(m_new, jnp.max(b, axis=0, keepdims=True))
                for i in range(ca // vs):
                    p0_ref[i * vs:(i + 1) * vs, :] = jnp.exp2(s1_ref[i * vs:(i + 1) * vs, :] - m_new).astype(BF16)
                accumulate(g, c, jnp.exp2(m_prev - m_new), p0_ref)
                return m_new

            lax.fori_loop(0, nca, exact_chunk, jnp.full((1, REP * tq), NEG_BIG, F32))

        acc = acc_ref[g]
        out_t = acc[:HEAD_DIM] / acc[HEAD_DIM:HEAD_DIM + 1]
        out_t = jnp.concatenate([out_t[:, r * tq:(r + 1) * tq] for r in range(REP)], axis=0)
        o_ref[:, pl.ds(pl.multiple_of(g * REP * HEAD_DIM, REP * HEAD_DIM), REP * HEAD_DIM)] = (
            out_t.T.astype(o_ref.dtype))
        return 0

    lax.fori_loop(0, N_KV_HEADS, finish_head, 0)


def _dsa(qi_r, w_s, ki_r, q_r, k_r, v_t, k_n, bsz, seq):
    t = q_r.shape[0]
    tq = 256
    assert seq % (2 * tq) == 0
    ksel = min(TOPK_MAX, seq // 4)
    assert ksel <= tq
    nqb = seq // tq
    rowblk = lambda w: pl.BlockSpec((tq, w), lambda b, i: (b * nqb + i, 0))
    return pl.pallas_call(
        functools.partial(_dsa_kernel, tq=tq, seq=seq, ksel=ksel),
        grid=(bsz, nqb),
        in_specs=[rowblk(N_IDX_HEADS * IDX_DIM), rowblk(LANES),
                  pl.BlockSpec((seq, LANES), lambda b, i: (b, 0)),
                  rowblk(ATT_W),
                  pl.BlockSpec((seq, KV_W), lambda b, i: (b, 0)),
                  pl.BlockSpec((1, KV_W, seq), lambda b, i: (b, 0, 0)),
                  pl.BlockSpec((k_n.shape[0] // bsz, SUBLANES, LANES), lambda b, i: (b, 0, 0))],
        out_specs=rowblk(ATT_W),
        out_shape=jax.ShapeDtypeStruct((t, ATT_W), BF16),
        scratch_shapes=[pltpu.VMEM((seq, tq), I32),
                        pltpu.VMEM((seq, tq), I16),
                        pltpu.VMEM((seq, tq), BF16),
                        pltpu.VMEM((IDX_DIM, N_IDX_HEADS * tq), BF16),
                        pltpu.VMEM((tq, N_IDX_HEADS * tq), F32),
                        pltpu.VMEM((tq, N_IDX_HEADS * tq), F32),
                        pltpu.VMEM((N_KV_HEADS, KV_W, REP * tq), BF16),
                        pltpu.VMEM((2 * tq, REP * tq), F32),
                        pltpu.VMEM((2 * tq, REP * tq), F32),
                        pltpu.VMEM((2 * tq, REP * tq), F32),
                        pltpu.VMEM((2 * tq, REP * tq), BF16),
                        pltpu.VMEM((2 * tq, REP * tq), BF16),
                        pltpu.VMEM((2 * tq, REP * tq), BF16),
                        pltpu.VMEM((N_KV_HEADS, HEAD_DIM + ONES_ROWS, REP * tq), F32),
                        pltpu.VMEM((N_KV_HEADS, 1, REP * tq), F32)],
        compiler_params=_cparams(("parallel", "arbitrary")),
        name="dsa",
    )(qi_r, w_s, ki_r, q_r, k_r, v_t, k_n)


def _causal_conv3(cur, halo, w):
    tm = cur.shape[0]
    ext = jnp.concatenate([halo, cur], axis=0)
    return (w[2:3, :] * cur + w[1:2, :] * ext[SUBLANES - 1:SUBLANES - 1 + tm]
            + w[0:1, :] * ext[SUBLANES - 2:SUBLANES - 2 + tm])


def _merge_kernel(x_ref, ga_ref, gb_ref, gc_ref, scx_ref, scb_ref, scc_ref, hx_ref, hc_ref,
                  yb_ref, yc_ref, cw_ref, wa_ref, wb_ref, wc_ref, wo_ref, o_ref, *, tm, seq):
    first = (pl.program_id(0) * tm) % seq == 0
    cx = scc_ref[...].astype(F32) * scx_ref[...].astype(F32)
    halo = hc_ref[...].astype(F32) * hx_ref[...].astype(F32)
    halo = jnp.where(first, 0.0, halo)
    ya = scb_ref[...].astype(F32) * _causal_conv3(cx, halo, cw_ref[...])
    dot = lambda a, w: jnp.dot(a, w[...], preferred_element_type=F32)
    m = jax.nn.sigmoid(ga_ref[...].astype(F32)) * dot(ya.astype(BF16), wa_ref)
    m = m + jax.nn.sigmoid(gb_ref[...].astype(F32)) * dot(yb_ref[...], wb_ref)
    m = m + jax.nn.sigmoid(gc_ref[...].astype(F32)) * dot(yc_ref[...], wc_ref)
    o_ref[...] = x_ref[...] + dot(m.astype(BF16), wo_ref)


def _merge(x2, p2, y_b, y_c, conv_w, w_a, w_b, w_c, w_o, seq):
    t, d = x2.shape
    tm = _pick(seq, (256, 128))
    hb = tm // SUBLANES

    def col(width, off):
        return pl.BlockSpec((tm, width), lambda i, o=off // width: (i, o))

    def halo(off):
        return pl.BlockSpec((SUBLANES, SC_W), lambda i, o=off // SC_W: (jnp.maximum(i * hb - 1, 0), o))

    def const(shape):
        return pl.BlockSpec(shape, lambda i: (0, 0), pipeline_mode=pl.Buffered(1))

    row = lambda w: pl.BlockSpec((tm, w), lambda i: (i, 0))
    return pl.pallas_call(
        functools.partial(_merge_kernel, tm=tm, seq=seq),
        grid=(t // tm,),
        in_specs=[row(d), col(d, 0), col(d, d), col(d, 2 * d),
                  col(SC_W, OFF_SCX), col(SC_W, OFF_SCB), col(SC_W, OFF_SCC),
                  halo(OFF_SCX), halo(OFF_SCC), row(SSM_W), row(ATT_W),
                  const((SC_CONV, SC_W)), const((SC_W, d)), const((SSM_W, d)),
                  const((ATT_W, d)), const((d, d))],
        out_specs=row(d),
        out_shape=jax.ShapeDtypeStruct((t, d), F32),
        compiler_params=_cparams(("parallel",)),
        name="merge",
    )(x2, p2, p2, p2, p2, p2, p2, p2, p2, y_b, y_c, conv_w, w_a, w_b, w_c, w_o)


def _ffn_kernel(x_ref, hx_ref, g_ref, wg_ref, wu_ref, cg_ref, cu_ref, wd_ref, og_ref, o_ref,
                h_ref, *, tm, seq, out_norm):
    j = pl.program_id(1)

    @pl.when(j == 0)
    def _():
        first = (pl.program_id(0) * tm) % seq == 0
        x = jnp.concatenate([hx_ref[...], x_ref[...]], axis=0)
        ms = jnp.mean(x * x, axis=-1, keepdims=True)
        h = x * lax.rsqrt(ms + NORM_EPS) * g_ref[...]
        rows = lax.broadcasted_iota(I32, (tm + SUBLANES, 1), 0)
        h = jnp.where(jnp.logical_and(first, rows < SUBLANES), 0.0, h)
        h_ref[...] = h.astype(BF16)
        o_ref[...] = jnp.zeros_like(o_ref)

    h = h_ref[...]
    gt = jnp.dot(h, wg_ref[...], preferred_element_type=F32)
    ut = jnp.dot(h, wu_ref[...], preferred_element_type=F32)
    gc = _causal_conv3(gt[SUBLANES:], gt[:SUBLANES], cg_ref[...])
    uc = _causal_conv3(ut[SUBLANES:], ut[:SUBLANES], cu_ref[...])
    act = (jax.nn.silu(gc) * uc).astype(BF16)
    o_ref[...] += jnp.dot(act, wd_ref[...], preferred_element_type=F32)

    @pl.when(j == pl.num_programs(1) - 1)
    def _():
        y = x_ref[...] + o_ref[...]
        if out_norm:
            ms = jnp.mean(y * y, axis=-1, keepdims=True)
            y = y * lax.rsqrt(ms + NORM_EPS) * og_ref[...]
        o_ref[...] = y


def _ffn(x2, gain, w_up, conv_w, w_down, out_gain, seq, out_norm):
    t, d = x2.shape
    dff = w_down.shape[0]
    tm = _pick(seq, (512, 256, 128))
    tf = _pick(dff, (512, 256, 128))
    nf = dff // tf
    hb = tm // SUBLANES
    return pl.pallas_call(
        functools.partial(_ffn_kernel, tm=tm, seq=seq, out_norm=out_norm),
        grid=(t // tm, nf),
        in_specs=[pl.BlockSpec((tm, d), lambda i, j: (i, 0)),
                  pl.BlockSpec((SUBLANES, d), lambda i, j: (jnp.maximum(i * hb - 1, 0), 0)),
                  pl.BlockSpec((1, d), lambda i, j: (0, 0)),
                  pl.BlockSpec((d, tf), lambda i, j: (0, j)),
                  pl.BlockSpec((d, tf), lambda i, j: (0, j + nf)),
                  pl.BlockSpec((FFN_CONV, tf), lambda i, j: (0, j)),
                  pl.BlockSpec((FFN_CONV, tf), lambda i, j: (0, j + nf)),
                  pl.BlockSpec((tf, d), lambda i, j: (j, 0)),
                  pl.BlockSpec((1, d), lambda i, j: (0, 0))],
        out_specs=pl.BlockSpec((tm, d), lambda i, j: (i, 0)),
        out_shape=jax.ShapeDtypeStruct((t, d), F32),
        scratch_shapes=[pltpu.VMEM((tm + SUBLANES, d), BF16)],
        compiler_params=_cparams(("parallel", "arbitrary")),
        name="ffn",
    )(x2, x2, gain, w_up, w_up, conv_w, conv_w, w_down, out_gain)


def _prep_w_in(w):
    d = w.shape[0]
    pad = jnp.zeros((d, NP_COLS - w.shape[1]), BF16)
    return jnp.concatenate([w[:, _ORIG_GATE_OFF:].astype(BF16), w[:, :_ORIG_GATE_OFF].astype(BF16), pad], axis=1)


def _prep_ssm(a_re, a_im, log_dt, b_re, b_im, c_re, c_im):
    ar, ai = a_re.astype(F32), a_im.astype(F32)
    dt = jnp.exp(log_dt.astype(F32))[:, None]
    mag = jnp.exp(dt * ar)
    abr, abi = mag * jnp.cos(dt * ai), mag * jnp.sin(dt * ai)
    den = ar * ar + ai * ai
    cr = ((abr - 1.0) * ar + abi * ai) / den
    ci = (abi * ar - (abr - 1.0) * ai) / den
    br, bi = b_re.astype(F32), b_im.astype(F32)
    bbr = cr[..., None] * br - ci[..., None] * bi
    bbi = cr[..., None] * bi + ci[..., None] * br
    eye = jnp.eye(SSM_GROUPS, dtype=F32)

    def bdiag_in(m):
        return (jnp.transpose(m, (0, 2, 1))[:, :, None, :] * eye[:, None, :, None]).reshape(
            SSM_W, SSM_NS)

    def bdiag_out(m):
        return (jnp.transpose(m, (0, 2, 1))[:, :, None, :] * eye[:, None, :, None]).reshape(
            SSM_NS, SSM_W)

    bbd = jnp.concatenate([bdiag_in(bbr), bdiag_in(bbi)], axis=1).astype(BF16)
    cbd = jnp.concatenate([bdiag_out(c_re.astype(F32)), bdiag_out(-c_im.astype(F32))], axis=0).astype(BF16)
    a_rows = jnp.stack([abr.reshape(-1), abi.reshape(-1)], axis=0)
    return bbd, cbd, a_rows


def kernel(x, positions, norm_mix, w_in, sc_conv, ssm_a_re, ssm_a_im, ssm_log_dt, ssm_b_re, ssm_b_im,
           ssm_c_re, ssm_c_im, ssm_d, ssm_glu, w_branch_a, w_branch_b, w_branch_c, w_out, norm_ffn,
           w_up, ffn_conv, w_down, norm_final):
    bsz, seq, d = x.shape
    depth = w_in.shape[0]
    t = bsz * seq
    x2 = x.reshape(t, d).astype(F32)
    tabs = _rope_tables(positions.astype(I32))
    for l in range(depth):
        p2 = _inproj(x2, norm_mix[l].reshape(1, d).astype(F32), _prep_w_in(w_in[l]))
        q_r, k_r, v_t, qi_r, ki_r, w_s, k_n = _rope(p2, tabs, bsz, seq)
        bbd, cbd, a_rows = _prep_ssm(ssm_a_re[l], ssm_a_im[l], ssm_log_dt[l], ssm_b_re[l], ssm_b_im[l],
                                     ssm_c_re[l], ssm_c_im[l])
        y_b = _ssm(p2.reshape(bsz, seq, NP_COLS), bbd, cbd, a_rows,
                   ssm_d[l].reshape(1, SSM_W).astype(F32), ssm_glu[l].astype(BF16))
        y_c = _dsa(qi_r, w_s, ki_r, q_r, k_r, v_t, k_n, bsz, seq)
        x2 = _merge(x2, p2, y_b.reshape(t, SSM_W), y_c, sc_conv[l].astype(F32),
                    w_branch_a[l].astype(BF16), w_branch_b[l].astype(BF16), w_branch_c[l].astype(BF16),
                    w_out[l].astype(BF16), seq)
        x2 = _ffn(x2, norm_ffn[l].reshape(1, d).astype(F32), w_up[l].astype(BF16),
                  ffn_conv[l].astype(F32), w_down[l].astype(BF16),
                  norm_final.reshape(1, d).astype(F32), seq, out_norm=(l == depth - 1))
    return x2.reshape(bsz, seq, d).astype(x.dtype)
```

```python
import functools
import math

import jax
import jax.numpy as jnp
from jax import lax
from jax.experimental import pallas as pl
from jax.experimental.pallas import tpu as pltpu

F32 = jnp.float32
BF16 = jnp.bfloat16
I32 = jnp.int32
I16 = jnp.int16

SC_W = 512
SC_CONV = 3
SSM_W = 512
SSM_GROUP = 16
SSM_GROUPS = SSM_W // SSM_GROUP
SSM_STATE = 64
SSM_NS = SSM_GROUPS * SSM_STATE
N_Q_HEADS = 16
N_KV_HEADS = 4
HEAD_DIM = 64
ATT_W = N_Q_HEADS * HEAD_DIM
KV_W = N_KV_HEADS * HEAD_DIM
REP = N_Q_HEADS // N_KV_HEADS
ROT_DIM = HEAD_DIM // 4
ROT_HALF = ROT_DIM // 2
ROPE_THETA = 500000.0
N_IDX_HEADS = 8
IDX_DIM = 64
TOPK_MAX = 256
D_FF = 5632
FFN_CONV = 3
NORM_EPS = 1e-6

LANES = 128
SUBLANES = 8
ONES_ROWS = 16
MXU_TILE = 256
VMEM_LIMIT = 56 * 1024 * 1024
NEG_BIG = -1e30
LOG2E = math.log2(math.e)
DENOM_FLOOR = 2.0 ** -64
INT_MIN = -(2 ** 31)
HIGH_BITS = 15

_ORIG_GATE_OFF = 3 * SC_W + SSM_W + ATT_W + 2 * KV_W + N_IDX_HEADS * IDX_DIM + IDX_DIM + N_IDX_HEADS
OFF_SCX = 3 * 2048
OFF_SCB = OFF_SCX + SC_W
OFF_SCC = OFF_SCB + SC_W
OFF_SSM = OFF_SCC + SC_W
OFF_Q = OFF_SSM + SSM_W
OFF_K = OFF_Q + ATT_W
OFF_V = OFF_K + KV_W
OFF_QI = OFF_V + KV_W
OFF_KI = OFF_QI + N_IDX_HEADS * IDX_DIM
INPROJ_TN = 7 * 256
NP_COLS = -(-(OFF_KI + LANES) // INPROJ_TN) * INPROJ_TN


def _cparams(sem, vmem=VMEM_LIMIT):
    return pltpu.CompilerParams(dimension_semantics=sem, vmem_limit_bytes=vmem)


def _pick(n, prefs):
    for p in prefs:
        if n % p == 0:
            return p
    return n


def _inproj_kernel(x_ref, g_ref, w_ref, o_ref, h_ref):
    @pl.when(pl.program_id(1) == 0)
    def _():
        x = x_ref[...]
        ms = jnp.mean(x * x, axis=-1, keepdims=True)
        h_ref[...] = (x * lax.rsqrt(ms + NORM_EPS) * g_ref[...]).astype(BF16)

    o_ref[...] = jnp.dot(h_ref[...], w_ref[...], preferred_element_type=F32).astype(o_ref.dtype)


def _inproj(x2, gain, w):
    t, d = x2.shape
    n = w.shape[1]
    tm = _pick(t, (1024, 512, 256, 128))
    tn = _pick(n, (INPROJ_TN, 384, 128))
    return pl.pallas_call(
        _inproj_kernel,
        grid=(t // tm, n // tn),
        in_specs=[pl.BlockSpec((tm, d), lambda i, j: (i, 0)),
                  pl.BlockSpec((1, d), lambda i, j: (0, 0)),
                  pl.BlockSpec((d, tn), lambda i, j: (0, j))],
        out_specs=pl.BlockSpec((tm, tn), lambda i, j: (i, j)),
        out_shape=jax.ShapeDtypeStruct((t, n), BF16),
        scratch_shapes=[pltpu.VMEM((tm, d), BF16)],
        compiler_params=_cparams(("parallel", "arbitrary")),
        name="inproj",
    )(x2, gain, w)


def _rope_table_kernel(pos_ref, invf_ref, a_ref, bm_ref, bp_ref):
    ang = pos_ref[...].astype(F32) * invf_ref[...]
    c = jnp.cos(ang)
    s = jnp.sin(ang)
    r = lax.broadcasted_iota(I32, ang.shape, 1) % HEAD_DIM
    lo = r < ROT_HALF
    hi = jnp.logical_and(r >= ROT_HALF, r < ROT_DIM)
    a_ref[...] = jnp.where(r < ROT_DIM, c, 1.0)
    bm_ref[...] = jnp.where(lo, -s, 0.0)
    bp_ref[...] = jnp.where(hi, s, 0.0)


def _rope_tables(positions):
    t = positions.size
    posb = jnp.broadcast_to(positions.reshape(t, 1), (t, LANES))
    inv_freq = ROPE_THETA ** (-jnp.arange(0, ROT_DIM, 2, dtype=F32) / ROT_DIM)
    lane = jnp.arange(LANES) % ROT_HALF
    invf = inv_freq[lane].reshape(1, LANES)
    tm = _pick(t, (1024, 512, 256, 128))
    spec = pl.BlockSpec((tm, LANES), lambda i: (i, 0))
    return pl.pallas_call(
        _rope_table_kernel,
        grid=(t // tm,),
        in_specs=[spec, pl.BlockSpec((1, LANES), lambda i: (0, 0))],
        out_specs=[spec, spec, spec],
        out_shape=[jax.ShapeDtypeStruct((t, LANES), F32)] * 3,
        compiler_params=_cparams(("parallel",)),
        name="rope_tables",
    )(posb, invf)


def _rope_kernel(q_ref, k_ref, v_ref, qi_ref, kiw_ref, a_ref, bm_ref, bp_ref,
                 qo_ref, ko_ref, vto_ref, qio_ref, kio_ref, wo_ref, kn_ref):
    a = a_ref[...]
    bm = bm_ref[...]
    bp = bp_ref[...]

    def rope(x):
        w = x.shape[1]
        reps = w // LANES
        xf = x.astype(F32)
        up = pltpu.roll(xf, w - ROT_HALF, 1)
        dn = pltpu.roll(xf, ROT_HALF, 1)
        return (xf * jnp.tile(a, (1, reps)) + up * jnp.tile(bm, (1, reps))
                + dn * jnp.tile(bp, (1, reps)))

    qo_ref[...] = (rope(q_ref[...]) * (HEAD_DIM ** -0.5 * LOG2E)).astype(BF16)
    kr = rope(k_ref[...]).astype(BF16)
    ko_ref[...] = kr
    ksq = kr.astype(F32) * kr.astype(F32)
    lane = lax.broadcasted_iota(I32, (1, LANES), 1)
    kn = jnp.zeros((1, LANES), F32)
    for g in range(N_KV_HEADS):
        n2 = jnp.sum(ksq[:, g * HEAD_DIM:(g + 1) * HEAD_DIM], axis=1, keepdims=True)
        kn = jnp.where(lane == g, jnp.max(n2, axis=0, keepdims=True), kn)
    kn_ref[0] = jnp.broadcast_to(kn, (SUBLANES, LANES))
    vto_ref[0] = v_ref[...].astype(F32).T.astype(BF16)
    qio_ref[...] = (rope(qi_ref[...]) * (IDX_DIM ** -0.5)).astype(BF16)
    kiw = kiw_ref[...]
    kio_ref[...] = rope(kiw).astype(BF16)
    wo_ref[...] = kiw.astype(F32) * (N_IDX_HEADS ** -0.5)


def _rope(p2, tabs, bsz, seq):
    t = p2.shape[0]
    tm = _pick(seq, (512, 256, 128))
    nsb = seq // tm
    a, bm, bp = tabs

    def col(width, off):
        return pl.BlockSpec((tm, width), lambda i, o=off // width: (i, o))

    tab = pl.BlockSpec((tm, LANES), lambda i: (i, 0))
    row = lambda w: pl.BlockSpec((tm, w), lambda i: (i, 0))
    return pl.pallas_call(
        _rope_kernel,
        grid=(t // tm,),
        in_specs=[col(ATT_W, OFF_Q), col(KV_W, OFF_K), col(KV_W, OFF_V),
                  col(N_IDX_HEADS * IDX_DIM, OFF_QI), col(LANES, OFF_KI), tab, tab, tab],
        out_specs=[row(ATT_W), row(KV_W),
                   pl.BlockSpec((1, KV_W, tm), lambda i: (i // nsb, 0, i % nsb)),
                   row(N_IDX_HEADS * IDX_DIM), row(LANES), row(LANES),
                   pl.BlockSpec((1, SUBLANES, LANES), lambda i: (i, 0, 0))],
        out_shape=[jax.ShapeDtypeStruct((t, ATT_W), BF16),
                   jax.ShapeDtypeStruct((t, KV_W), BF16),
                   jax.ShapeDtypeStruct((bsz, KV_W, seq), BF16),
                   jax.ShapeDtypeStruct((t, N_IDX_HEADS * IDX_DIM), BF16),
                   jax.ShapeDtypeStruct((t, LANES), BF16),
                   jax.ShapeDtypeStruct((t, LANES), F32),
                   jax.ShapeDtypeStruct((t // tm, SUBLANES, LANES), F32)],
        compiler_params=_cparams(("parallel",)),
        name="rope",
    )(p2, p2, p2, p2, p2, a, bm, bp)


def _ssm_kernel(u_ref, bbd_ref, cbd_ref, a_ref, d_ref, wg_ref, o_ref, bu_ref, st_ref, *x_refs, bsz, chunk):
    @pl.when(pl.program_id(0) == 0)
    def _():
        st_ref[...] = jnp.zeros_like(st_ref)

    u = u_ref[...].reshape(bsz * chunk, SSM_W)
    gpt = MXU_TILE // SSM_STATE
    kin = gpt * SSM_GROUP
    for part in range(2):
        for n in range(SSM_NS // MXU_TILE):
            cols = slice(part * SSM_NS + n * MXU_TILE, part * SSM_NS + (n + 1) * MXU_TILE)
            bu_ref[:, cols] = jnp.dot(u[:, n * kin:(n + 1) * kin], bbd_ref[n * kin:(n + 1) * kin, cols],
                                      preferred_element_type=F32)
    ar = a_ref[0:1, :]
    ai = a_ref[1:2, :]

    def step(t, carry):
        out = []
        for b in range(bsz):
            xr, xi = carry[b]
            row = pl.ds(b * chunk + t, 1)
            nr = ar * xr - ai * xi + bu_ref[row, 0:SSM_NS]
            ni = ar * xi + ai * xr + bu_ref[row, SSM_NS:2 * SSM_NS]
            x_refs[b][pl.ds(t, 1), 0:SSM_NS] = nr
            x_refs[b][pl.ds(t, 1), SSM_NS:2 * SSM_NS] = ni
            out.append((nr, ni))
        return tuple(out)

    init = tuple((st_ref[2 * b:2 * b + 1, :], st_ref[2 * b + 1:2 * b + 2, :]) for b in range(bsz))
    last = lax.fori_loop(0, chunk, step, init, unroll=8)
    for b in range(bsz):
        st_ref[2 * b:2 * b + 1, :] = last[b][0]
        st_ref[2 * b + 1:2 * b + 2, :] = last[b][1]

    spt = (MXU_TILE // SSM_GROUP) * SSM_STATE
    ys = []
    for m in range(SSM_W // MXU_TILE):
        out = slice(m * MXU_TILE, (m + 1) * MXU_TILE)
        acc = None
        for part in range(2):
            rows = slice(part * SSM_NS + m * spt, part * SSM_NS + (m + 1) * spt)
            xs = jnp.concatenate([x_refs[b][:, rows].astype(BF16) for b in range(bsz)], axis=0)
            d = jnp.dot(xs, cbd_ref[rows, out], preferred_element_type=F32)
            acc = d if acc is None else acc + d
        ys.append(acc)
    y = jnp.concatenate(ys, axis=1) + d_ref[...] * u.astype(F32)
    z = jax.nn.gelu(y)
    gate = jnp.dot(z.astype(BF16), wg_ref[...], preferred_element_type=F32)
    o_ref[...] = (z * jax.nn.sigmoid(gate)).astype(o_ref.dtype).reshape(bsz, chunk, SSM_W)


def _ssm(p3, bbd, cbd, a_bar, d_skip, w_glu):
    bsz, seq, _ = p3.shape
    chunk = _pick(seq, (256, 128))
    const = lambda shape: pl.BlockSpec(shape, lambda c: (0,) * len(shape))
    return pl.pallas_call(
        functools.partial(_ssm_kernel, bsz=bsz, chunk=chunk),
        grid=(seq // chunk,),
        in_specs=[pl.BlockSpec((bsz, chunk, SSM_W), lambda c: (0, c, OFF_SSM // SSM_W)),
                  const((SSM_W, 2 * SSM_NS)), const((2 * SSM_NS, SSM_W)),
                  const((2, SSM_NS)), const((1, SSM_W)), const((SSM_W, SSM_W))],
        out_specs=pl.BlockSpec((bsz, chunk, SSM_W), lambda c: (0, c, 0)),
        out_shape=jax.ShapeDtypeStruct((bsz, seq, SSM_W), BF16),
        scratch_shapes=([pltpu.VMEM((bsz * chunk, 2 * SSM_NS), F32), pltpu.VMEM((2 * bsz, SSM_NS), F32)]
                        + [pltpu.VMEM((chunk, 2 * SSM_NS), F32)] * bsz),
        compiler_params=_cparams(("arbitrary",)),
        name="ssm",
    )(p3, bbd, cbd, a_bar, d_skip, w_glu)


def _dsa_kernel(qi_ref, ws_ref, ki_ref, q_ref, k_ref, vt_ref, kn_ref, o_ref,
                key_ref, hkey_ref, mask_ref, qit_ref, d0_ref, d1_ref, qt_ref, s0_ref, s1_ref, s2_ref,
                p0_ref, p1_ref, p2_ref, acc_ref, shift_ref,
                *, tq, seq, ksel):
    qb = pl.program_id(1)
    t0 = qb * tq
    cw = tq
    nc = qb + 1
    ca = 2 * tq
    nca = (qb + 2) // 2
    ks = 128
    vs = 64
    kf = float(ksel)
    jbits = (seq - 1).bit_length()
    qidx = t0 + lax.broadcasted_iota(I32, (1, tq), 1)

    def _select():
        qit = qi_ref[...].astype(F32).T
        qit_ref[...] = jnp.concatenate(
            [qit[h * IDX_DIM:(h + 1) * IDX_DIM, :] for h in range(N_IDX_HEADS)], axis=1).astype(BF16)
        wt = ws_ref[...].T
        wrow = jnp.concatenate([wt[IDX_DIM + h:IDX_DIM + h + 1, :] for h in range(N_IDX_HEADS)], axis=1)

        def idx_dots(c, d_ref):
            start = pl.multiple_of(c * cw, cw)
            kib = ki_ref[pl.ds(start, cw), :][:, :IDX_DIM]
            d_ref[...] = jnp.dot(kib, qit_ref[...], preferred_element_type=F32)

        def idx_scores(c, d_ref):
            for i in range(cw // ks):
                start = pl.multiple_of(c * cw, cw) + i * ks
                r = jnp.maximum(d_ref[i * ks:(i + 1) * ks, :], 0.0) * wrow
                isc = r[:, 0:tq]
                for h in range(1, N_IDX_HEADS):
                    isc = isc + r[:, h * tq:(h + 1) * tq]
                kidx = start + lax.broadcasted_iota(I32, (ks, tq), 0)
                isc = jnp.where(isc == 0.0, 0.0, isc)
                isc = jnp.where(kidx <= qidx, isc, -jnp.inf)
                bits = pltpu.bitcast(isc, I32)
                key = bits ^ ((bits >> 31) & 0x7FFFFFFF)
                key_ref[pl.ds(start, ks), :] = key
                hkey_ref[pl.ds(start, ks), :] = (key >> 16).astype(I16)

        idx_dots(0, d0_ref)

        def idx_pair(c2, _):
            c = 2 * c2
            idx_dots(c + 1, d1_ref)
            idx_scores(c, d0_ref)
            idx_dots(jnp.minimum(c + 2, nc - 1), d0_ref)
            idx_scores(c + 1, d1_ref)
            return 0

        lax.fori_loop(0, nc // 2, idx_pair, 0)

        @pl.when(nc % 2 == 1)
        def _():
            idx_scores(nc - 1, d0_ref)

        @pl.when(qb % 2 == 0)
        def _():
            mask_ref[pl.ds(pl.multiple_of((qb + 1) * cw, cw), cw), :] = jnp.zeros((cw, tq), BF16)

        small = qidx < ksel

        def key_rows(c):
            return pl.multiple_of(c * cw, cw) + lax.broadcasted_iota(I32, (cw, tq), 0)

        def count(pred):
            def body(c, cnt):
                blk = key_ref[pl.ds(pl.multiple_of(c * cw, cw), cw), :]
                hit = pred(blk, c)
                return cnt + jnp.sum(hit.reshape(cw // 32, 32, tq), axis=0)
            cnt = lax.fori_loop(0, nc, body, jnp.zeros((32, tq), F32))
            return jnp.sum(cnt, axis=0, keepdims=True)

        def count_ge(cand):
            return count(lambda x, c: jnp.where(x >= cand, 1.0, 0.0))

        def count_half_ge(c16):
            def body(c, cnt):
                blk = hkey_ref[pl.ds(pl.multiple_of(c * cw, cw), cw), :]
                hit = jnp.where(blk >= c16, jnp.ones((), I16), jnp.zeros((), I16))
                for r in range(0, cw, 32):
                    cnt = cnt + hit[r:r + 32, :]
                return cnt
            cnt = lax.fori_loop(0, nc, body, jnp.zeros((32, tq), I16))
            return jnp.sum(cnt.astype(F32), axis=0, keepdims=True)

        def count_high_ge(cand):
            return count_half_ge((cand >> 16).astype(I16))

        def count_low_ge(cand):
            return count_half_ge(((cand & 0xFFFF) - 32768).astype(I16))

        c0 = count_high_ge(jnp.zeros((1, tq), I32))
        thr = jnp.where(c0 >= kf, 0, INT_MIN).astype(I32)
        done = jnp.where(jnp.logical_or(small, c0 == kf), 1.0, 0.0)

        def bit_step(i, thr, done, counter):
            cand = thr + jnp.left_shift(jnp.int32(1), 30 - i)
            cnt = counter(cand)
            return (jnp.where(cnt >= kf, cand, thr),
                    jnp.maximum(done, jnp.where(cnt == kf, 1.0, 0.0)))

        thr, done = lax.fori_loop(0, HIGH_BITS, lambda i, c: bit_step(i, *c, count_high_ge), (thr, done))

        t16 = (thr >> 16).astype(I16)

        def low_body(c, _):
            rows = pl.ds(pl.multiple_of(c * cw, cw), cw)
            h = hkey_ref[rows, :]
            low = ((key_ref[rows, :] & 0xFFFF) - 32768).astype(I16)
            hkey_ref[rows, :] = jnp.where(h == t16, low, jnp.where(h > t16, jnp.full((), 32767, I16),
                                                                    jnp.full((), -32768, I16)))
            return 0

        lax.fori_loop(0, nc, low_body, 0)

        def w_cond(c):
            return jnp.logical_and(c[0] < 31, c[3] < 0.5)

        def w_body(c):
            thr, done = bit_step(c[0], c[1], c[2], count_low_ge)
            thr, done = bit_step(c[0] + 1, thr, done, count_low_ge)
            return c[0] + 2, thr, done, jnp.min(done)

        _, thr, done, settled = lax.while_loop(
            w_cond, w_body, (jnp.int32(HIGH_BITS), thr, done, jnp.min(done)))
        thr = jnp.where(small, INT_MIN, thr)

        def tie_index():
            n_ge = count_ge(thr)
            need = kf - count_ge(thr + 1)
            tie = jnp.logical_and(jnp.logical_not(small), n_ge > kf)

            def count_tie_below(cand):
                return count(lambda x, c: jnp.where(
                    x == thr, jnp.where(key_rows(c) < cand, 1.0, 0.0), 0.0))

            def jbit_body(i, jj):
                cand = jj + jnp.left_shift(jnp.int32(1), jbits - 1 - i)
                return jnp.where(count_tie_below(cand) < need, cand, jj)

            jj = lax.fori_loop(0, jbits, jbit_body, jnp.zeros((1, tq), I32))
            return jnp.where(tie, jj, seq)

        jmax = lax.cond(settled < 0.5, tie_index, lambda: jnp.full((1, tq), seq, I32))

        def mask_body(c, _):
            rows = pl.ds(pl.multiple_of(c * cw, cw), cw)
            x = key_ref[rows, :]
            kidx = key_rows(c)
            b = jnp.where(x > thr, 1.0, jnp.where(x == thr, jnp.where(kidx <= jmax, 1.0, 0.0), 0.0))
            mask_ref[rows, :] = jnp.where(kidx <= qidx, b, 0.0).astype(BF16)
            return 0

        lax.fori_loop(0, nc, mask_body, 0)

    _select()

    kn_all = jnp.max(kn_ref[...], axis=(0, 1), keepdims=True)[0]
    lane = lax.broadcasted_iota(I32, (1, LANES), 1)
    qt_ref[...] = jnp.zeros(qt_ref.shape, BF16)
    for g in range(N_KV_HEADS):
        qt = q_ref[:, g * REP * HEAD_DIM:(g + 1) * REP * HEAD_DIM].astype(F32).T.astype(BF16)
        for r in range(REP):
            qt_ref[g, g * HEAD_DIM:(g + 1) * HEAD_DIM, r * tq:(r + 1) * tq] = qt[r * HEAD_DIM:(r + 1) * HEAD_DIM, :]
        qsq = qt.astype(F32) * qt.astype(F32)
        qn2 = jnp.concatenate([jnp.sum(qsq[r * HEAD_DIM:(r + 1) * HEAD_DIM, :], axis=0, keepdims=True)
                               for r in range(REP)], axis=1)
        kn2 = jnp.max(jnp.where(lane == g, kn_all, 0.0), axis=1, keepdims=True)
        shift_ref[g] = jnp.sqrt(qn2) * jnp.sqrt(kn2)
    acc_ref[...] = jnp.zeros(acc_ref.shape, F32)
    ones_rows = jnp.ones((ONES_ROWS, ca), BF16)

    n_items = N_KV_HEADS * nca

    def head_chunk(w):
        g = jnp.asarray(w, I32) // nca
        return g, w - g * nca

    def scores(w, s_ref):
        g, c = head_chunk(w)
        start = pl.multiple_of(c * ca, ca)
        s_ref[...] = jnp.dot(k_ref[pl.ds(start, ca), :], qt_ref[g], preferred_element_type=F32)

    def accumulate(g, c, scale, p_ref):
        vt = jnp.concatenate(
            [vt_ref[0, pl.ds(pl.multiple_of(g * HEAD_DIM, HEAD_DIM), HEAD_DIM),
                    pl.ds(pl.multiple_of(c * ca, ca), ca)], ones_rows], axis=0)
        acc_ref[g] = scale * acc_ref[g] + jnp.dot(vt, p_ref[...], preferred_element_type=F32)

    def mask_rows(c, i):
        return mask_ref[pl.ds(pl.multiple_of(c * ca, ca) + i * vs, vs), :]

    def softmax_chunk(w, s_ref, p_ref):
        g, c = head_chunk(w)
        shift = shift_ref[g]
        for i in range(ca // vs):
            p = jnp.exp2(s_ref[i * vs:(i + 1) * vs, :] - shift).astype(BF16)
            p_ref[i * vs:(i + 1) * vs, :] = p * jnp.tile(mask_rows(c, i), (1, REP))
        accumulate(g, c, 1.0, p_ref)

    scores(0, s0_ref)

    def att_triple(w3, _):
        w = 3 * w3
        scores(w + 1, s1_ref)
        softmax_chunk(w, s0_ref, p0_ref)
        scores(w + 2, s2_ref)
        softmax_chunk(w + 1, s1_ref, p1_ref)
        scores(jnp.minimum(w + 3, n_items - 1), s0_ref)
        softmax_chunk(w + 2, s2_ref, p2_ref)
        return 0

    lax.fori_loop(0, n_items // 3, att_triple, 0)
    rem = n_items % 3
    base = n_items - rem

    @pl.when(rem >= 1)
    def _():
        scores(jnp.minimum(base + 1, n_items - 1), s1_ref)
        softmax_chunk(base, s0_ref, p0_ref)

    @pl.when(rem == 2)
    def _():
        softmax_chunk(base + 1, s1_ref, p1_ref)

    def finish_head(g, _):
        @pl.when(jnp.min(acc_ref[g][HEAD_DIM:HEAD_DIM + 1, :]) < DENOM_FLOOR)
        def _():
            acc_ref[g] = jnp.zeros(acc_ref.shape[1:], F32)

            def exact_chunk(c, m_prev):
                scores(g * nca + c, s0_ref)
                m_new = m_prev
                for i in range(ca // vs):
                    b = jnp.where(jnp.tile(mask_rows(c, i), (1, REP)) > 0, s0_ref[i * vs:(i + 1) * vs, :], NEG_BIG)
                    s1_ref[i * vs:(i + 1) * vs, :] = b
                    m_new = jnp.maximum(m_new, jnp.max(b, axis=0, keepdims=True))
                for i in range(ca // vs):
                    p0_ref[i * vs:(i + 1) * vs, :] = jnp.exp2(s1_ref[i * vs:(i + 1) * vs, :] - m_new).astype(BF16)
                accumulate(g, c, jnp.exp2(m_prev - m_new), p0_ref)
                return m_new

            lax.fori_loop(0, nca, exact_chunk, jnp.full((1, REP * tq), NEG_BIG, F32))

        acc = acc_ref[g]
        out_t = acc[:HEAD_DIM] / acc[HEAD_DIM:HEAD_DIM + 1]
        out_t = jnp.concatenate([out_t[:, r * tq:(r + 1) * tq] for r in range(REP)], axis=0)
        o_ref[:, pl.ds(pl.multiple_of(g * REP * HEAD_DIM, REP * HEAD_DIM), REP * HEAD_DIM)] = (
            out_t.T.astype(o_ref.dtype))
        return 0

    lax.fori_loop(0, N_KV_HEADS, finish_head, 0)


def _dsa(qi_r, w_s, ki_r, q_r, k_r, v_t, k_n, bsz, seq):
    t = q_r.shape[0]
    tq = 256
    assert seq % (2 * tq) == 0
    ksel = min(TOPK_MAX, seq // 4)
    assert ksel <= tq
    nqb = seq // tq
    rowblk = lambda w: pl.BlockSpec((tq, w), lambda b, i: (b * nqb + i, 0))
    return pl.pallas_call(
        functools.partial(_dsa_kernel, tq=tq, seq=seq, ksel=ksel),
        grid=(bsz, nqb),
        in_specs=[rowblk(N_IDX_HEADS * IDX_DIM), rowblk(LANES),
                  pl.BlockSpec((seq, LANES), lambda b, i: (b, 0)),
                  rowblk(ATT_W),
                  pl.BlockSpec((seq, KV_W), lambda b, i: (b, 0)),
                  pl.BlockSpec((1, KV_W, seq), lambda b, i: (b, 0, 0)),
                  pl.BlockSpec((k_n.shape[0] // bsz, SUBLANES, LANES), lambda b, i: (b, 0, 0))],
        out_specs=rowblk(ATT_W),
        out_shape=jax.ShapeDtypeStruct((t, ATT_W), BF16),
        scratch_shapes=[pltpu.VMEM((seq, tq), I32),
                        pltpu.VMEM((seq, tq), I16),
                        pltpu.VMEM((seq, tq), BF16),
                        pltpu.VMEM((IDX_DIM, N_IDX_HEADS * tq), BF16),
                        pltpu.VMEM((tq, N_IDX_HEADS * tq), F32),
                        pltpu.VMEM((tq, N_IDX_HEADS * tq), F32),
                        pltpu.VMEM((N_KV_HEADS, KV_W, REP * tq), BF16),
                        pltpu.VMEM((2 * tq, REP * tq), F32),
                        pltpu.VMEM((2 * tq, REP * tq), F32),
                        pltpu.VMEM((2 * tq, REP * tq), F32),
                        pltpu.VMEM((2 * tq, REP * tq), BF16),
                        pltpu.VMEM((2 * tq, REP * tq), BF16),
                        pltpu.VMEM((2 * tq, REP * tq), BF16),
                        pltpu.VMEM((N_KV_HEADS, HEAD_DIM + ONES_ROWS, REP * tq), F32),
                        pltpu.VMEM((N_KV_HEADS, 1, REP * tq), F32)],
        compiler_params=_cparams(("parallel", "arbitrary")),
        name="dsa",
    )(qi_r, w_s, ki_r, q_r, k_r, v_t, k_n)


def _causal_conv3(cur, halo, w):
    tm = cur.shape[0]
    ext = jnp.concatenate([halo, cur], axis=0)
    return (w[2:3, :] * cur + w[1:2, :] * ext[SUBLANES - 1:SUBLANES - 1 + tm]
            + w[0:1, :] * ext[SUBLANES - 2:SUBLANES - 2 + tm])


def _merge_kernel(x_ref, ga_ref, gb_ref, gc_ref, scx_ref, scb_ref, scc_ref, hx_ref, hc_ref,
                  yb_ref, yc_ref, cw_ref, wa_ref, wb_ref, wc_ref, wo_ref, o_ref, *, tm, seq):
    first = (pl.program_id(0) * tm) % seq == 0
    cx = scc_ref[...].astype(F32) * scx_ref[...].astype(F32)
    halo = hc_ref[...].astype(F32) * hx_ref[...].astype(F32)
    halo = jnp.where(first, 0.0, halo)
    ya = scb_ref[...].astype(F32) * _causal_conv3(cx, halo, cw_ref[...])
    dot = lambda a, w: jnp.dot(a, w[...], preferred_element_type=F32)
    m = jax.nn.sigmoid(ga_ref[...].astype(F32)) * dot(ya.astype(BF16), wa_ref)
    m = m + jax.nn.sigmoid(gb_ref[...].astype(F32)) * dot(yb_ref[...], wb_ref)
    m = m + jax.nn.sigmoid(gc_ref[...].astype(F32)) * dot(yc_ref[...], wc_ref)
    o_ref[...] = x_ref[...] + dot(m.astype(BF16), wo_ref)


def _merge(x2, p2, y_b, y_c, conv_w, w_a, w_b, w_c, w_o, seq):
    t, d = x2.shape
    tm = _pick(seq, (256, 128))
    hb = tm // SUBLANES

    def col(width, off):
        return pl.BlockSpec((tm, width), lambda i, o=off // width: (i, o))

    def halo(off):
        return pl.BlockSpec((SUBLANES, SC_W), lambda i, o=off // SC_W: (jnp.maximum(i * hb - 1, 0), o))

    def const(shape):
        return pl.BlockSpec(shape, lambda i: (0, 0), pipeline_mode=pl.Buffered(1))

    row = lambda w: pl.BlockSpec((tm, w), lambda i: (i, 0))
    return pl.pallas_call(
        functools.partial(_merge_kernel, tm=tm, seq=seq),
        grid=(t // tm,),
        in_specs=[row(d), col(d, 0), col(d, d), col(d, 2 * d),
                  col(SC_W, OFF_SCX), col(SC_W, OFF_SCB), col(SC_W, OFF_SCC),
                  halo(OFF_SCX), halo(OFF_SCC), row(SSM_W), row(ATT_W),
                  const((SC_CONV, SC_W)), const((SC_W, d)), const((SSM_W, d)),
                  const((ATT_W, d)), const((d, d))],
        out_specs=row(d),
        out_shape=jax.ShapeDtypeStruct((t, d), F32),
        compiler_params=_cparams(("parallel",)),
        name="merge",
    )(x2, p2, p2, p2, p2, p2, p2, p2, p2, y_b, y_c, conv_w, w_a, w_b, w_c, w_o)


def _ffn_kernel(x_ref, g_ref, wg_ref, wu_ref, cg_ref, cu_ref, wd_ref, og_ref, o_ref,
                h_ref, halo_ref, *, tm, seq, out_norm):
    j = pl.program_id(1)

    @pl.when(j == 0)
    def _():
        x = x_ref[...]
        ms = jnp.mean(x * x, axis=-1, keepdims=True)
        h_ref[...] = (x * lax.rsqrt(ms + NORM_EPS) * g_ref[...]).astype(BF16)
        o_ref[...] = jnp.zeros_like(o_ref)

    first = (pl.program_id(0) * tm) % seq == 0
    h = h_ref[...]
    gt = jnp.dot(h, wg_ref[...], preferred_element_type=F32)
    ut = jnp.dot(h, wu_ref[...], preferred_element_type=F32)
    gh = jnp.where(first, 0.0, halo_ref[j, 0])
    uh = jnp.where(first, 0.0, halo_ref[j, 1])
    halo_ref[j, 0] = gt[tm - SUBLANES:]
    halo_ref[j, 1] = ut[tm - SUBLANES:]
    gc = _causal_conv3(gt, gh, cg_ref[...])
    uc = _causal_conv3(ut, uh, cu_ref[...])
    act = (jax.nn.silu(gc) * uc).astype(BF16)
    o_ref[...] += jnp.dot(act, wd_ref[...], preferred_element_type=F32)

    @pl.when(j == pl.num_programs(1) - 1)
    def _():
        y = x_ref[...] + o_ref[...]
        if out_norm:
            ms = jnp.mean(y * y, axis=-1, keepdims=True)
            y = y * lax.rsqrt(ms + NORM_EPS) * og_ref[...]
        o_ref[...] = y


def _ffn(x2, gain, w_up, conv_w, w_down, out_gain, seq, out_norm):
    t, d = x2.shape
    dff = w_down.shape[0]
    tm = _pick(seq, (512, 256, 128))
    tf = _pick(dff, (512, 256, 128))
    nf = dff // tf
    return pl.pallas_call(
        functools.partial(_ffn_kernel, tm=tm, seq=seq, out_norm=out_norm),
        grid=(t // tm, nf),
        in_specs=[pl.BlockSpec((tm, d), lambda i, j: (i, 0)),
                  pl.BlockSpec((1, d), lambda i, j: (0, 0)),
                  pl.BlockSpec((d, tf), lambda i, j: (0, j)),
                  pl.BlockSpec((d, tf), lambda i, j: (0, j + nf)),
                  pl.BlockSpec((FFN_CONV, tf), lambda i, j: (0, j)),
                  pl.BlockSpec((FFN_CONV, tf), lambda i, j: (0, j + nf)),
                  pl.BlockSpec((tf, d), lambda i, j: (j, 0)),
                  pl.BlockSpec((1, d), lambda i, j: (0, 0))],
        out_specs=pl.BlockSpec((tm, d), lambda i, j: (i, 0)),
        out_shape=jax.ShapeDtypeStruct((t, d), F32),
        scratch_shapes=[pltpu.VMEM((tm, d), BF16), pltpu.VMEM((nf, 2, SUBLANES, tf), F32)],
        compiler_params=_cparams(("arbitrary", "arbitrary")),
        name="ffn",
    )(x2, gain, w_up, w_up, conv_w, conv_w, w_down, out_gain)


def _prep_w_in(w):
    d = w.shape[0]
    pad = jnp.zeros((d, NP_COLS - w.shape[1]), BF16)
    return jnp.concatenate([w[:, _ORIG_GATE_OFF:].astype(BF16), w[:, :_ORIG_GATE_OFF].astype(BF16), pad], axis=1)


def _prep_ssm(a_re, a_im, log_dt, b_re, b_im, c_re, c_im):
    ar, ai = a_re.astype(F32), a_im.astype(F32)
    dt = jnp.exp(log_dt.astype(F32))[:, None]
    mag = jnp.exp(dt * ar)
    abr, abi = mag * jnp.cos(dt * ai), mag * jnp.sin(dt * ai)
    den = ar * ar + ai * ai
    cr = ((abr - 1.0) * ar + abi * ai) / den
    ci = (abi * ar - (abr - 1.0) * ai) / den
    br, bi = b_re.astype(F32), b_im.astype(F32)
    bbr = cr[..., None] * br - ci[..., None] * bi
    bbi = cr[..., None] * bi + ci[..., None] * br
    eye = jnp.eye(SSM_GROUPS, dtype=F32)

    def bdiag_in(m):
        return (jnp.transpose(m, (0, 2, 1))[:, :, None, :] * eye[:, None, :, None]).reshape(
            SSM_W, SSM_NS)

    def bdiag_out(m):
        return (jnp.transpose(m, (0, 2, 1))[:, :, None, :] * eye[:, None, :, None]).reshape(
            SSM_NS, SSM_W)

    bbd = jnp.concatenate([bdiag_in(bbr), bdiag_in(bbi)], axis=1).astype(BF16)
    cbd = jnp.concatenate([bdiag_out(c_re.astype(F32)), bdiag_out(-c_im.astype(F32))], axis=0).astype(BF16)
    a_rows = jnp.stack([abr.reshape(-1), abi.reshape(-1)], axis=0)
    return bbd, cbd, a_rows


def kernel(x, positions, norm_mix, w_in, sc_conv, ssm_a_re, ssm_a_im, ssm_log_dt, ssm_b_re, ssm_b_im,
           ssm_c_re, ssm_c_im, ssm_d, ssm_glu, w_branch_a, w_branch_b, w_branch_c, w_out, norm_ffn,
           w_up, ffn_conv, w_down, norm_final):
    bsz, seq, d = x.shape
    depth = w_in.shape[0]
    t = bsz * seq
    x2 = x.reshape(t, d).astype(F32)
    tabs = _rope_tables(positions.astype(I32))
    for l in range(depth):
        p2 = _inproj(x2, norm_mix[l].reshape(1, d).astype(F32), _prep_w_in(w_in[l]))
        q_r, k_r, v_t, qi_r, ki_r, w_s, k_n = _rope(p2, tabs, bsz, seq)
        bbd, cbd, a_rows = _prep_ssm(ssm_a_re[l], ssm_a_im[l], ssm_log_dt[l], ssm_b_re[l], ssm_b_im[l],
                                     ssm_c_re[l], ssm_c_im[l])
        y_b = _ssm(p2.reshape(bsz, seq, NP_COLS), bbd, cbd, a_rows,
                   ssm_d[l].reshape(1, SSM_W).astype(F32), ssm_glu[l].astype(BF16))
        y_c = _dsa(qi_r, w_s, ki_r, q_r, k_r, v_t, k_n, bsz, seq)
        x2 = _merge(x2, p2, y_b.reshape(t, SSM_W), y_c, sc_conv[l].astype(F32),
                    w_branch_a[l].astype(BF16), w_branch_b[l].astype(BF16), w_branch_c[l].astype(BF16),
                    w_out[l].astype(BF16), seq)
        x2 = _ffn(x2, norm_ffn[l].reshape(1, d).astype(F32), w_up[l].astype(BF16),
                  ffn_conv[l].astype(F32), w_down[l].astype(BF16),
                  norm_final.reshape(1, d).astype(F32), seq, out_norm=(l == depth - 1))
    return x2.reshape(bsz, seq, d).astype(x.dtype)
```

```python
import functools
import math

import jax
import jax.numpy as jnp
from jax import lax
from jax.experimental import pallas as pl
from jax.experimental.pallas import tpu as pltpu

F32 = jnp.float32
BF16 = jnp.bfloat16
I32 = jnp.int32
I16 = jnp.int16

SC_W = 512
SC_CONV = 3
SSM_W = 512
SSM_GROUP = 16
SSM_GROUPS = SSM_W // SSM_GROUP
SSM_STATE = 64
SSM_NS = SSM_GROUPS * SSM_STATE
N_Q_HEADS = 16
N_KV_HEADS = 4
HEAD_DIM = 64
ATT_W = N_Q_HEADS * HEAD_DIM
KV_W = N_KV_HEADS * HEAD_DIM
REP = N_Q_HEADS // N_KV_HEADS
ROT_DIM = HEAD_DIM // 4
ROT_HALF = ROT_DIM // 2
ROPE_THETA = 500000.0
N_IDX_HEADS = 8
IDX_DIM = 64
TOPK_MAX = 256
D_FF = 5632
FFN_CONV = 3
NORM_EPS = 1e-6

LANES = 128
SUBLANES = 8
ONES_ROWS = 16
MXU_TILE = 256
VMEM_LIMIT = 56 * 1024 * 1024
NEG_BIG = -1e30
LOG2E = math.log2(math.e)
DENOM_FLOOR = 2.0 ** -64
INT_MIN = -(2 ** 31)
HIGH_BITS = 15

_ORIG_GATE_OFF = 3 * SC_W + SSM_W + ATT_W + 2 * KV_W + N_IDX_HEADS * IDX_DIM + IDX_DIM + N_IDX_HEADS
OFF_SCX = 3 * 2048
OFF_SCB = OFF_SCX + SC_W
OFF_SCC = OFF_SCB + SC_W
OFF_SSM = OFF_SCC + SC_W
OFF_Q = OFF_SSM + SSM_W
OFF_K = OFF_Q + ATT_W
OFF_V = OFF_K + KV_W
OFF_QI = OFF_V + KV_W
OFF_KI = OFF_QI + N_IDX_HEADS * IDX_DIM
INPROJ_TN = 7 * 256
NP_COLS = -(-(OFF_KI + LANES) // INPROJ_TN) * INPROJ_TN


def _cparams(sem, vmem=VMEM_LIMIT):
    return pltpu.CompilerParams(dimension_semantics=sem, vmem_limit_bytes=vmem)


def _pick(n, prefs):
    for p in prefs:
        if n % p == 0:
            return p
    return n


def _inproj_kernel(x_ref, g_ref, w_ref, o_ref, h_ref):
    @pl.when(pl.program_id(1) == 0)
    def _():
        x = x_ref[...]
        ms = jnp.mean(x * x, axis=-1, keepdims=True)
        h_ref[...] = (x * lax.rsqrt(ms + NORM_EPS) * g_ref[...]).astype(BF16)

    o_ref[...] = jnp.dot(h_ref[...], w_ref[...], preferred_element_type=F32).astype(o_ref.dtype)


def _inproj(x2, gain, w):
    t, d = x2.shape
    n = w.shape[1]
    tm = _pick(t, (1024, 512, 256, 128))
    tn = _pick(n, (INPROJ_TN, 384, 128))
    return pl.pallas_call(
        _inproj_kernel,
        grid=(t // tm, n // tn),
        in_specs=[pl.BlockSpec((tm, d), lambda i, j: (i, 0)),
                  pl.BlockSpec((1, d), lambda i, j: (0, 0)),
                  pl.BlockSpec((d, tn), lambda i, j: (0, j))],
        out_specs=pl.BlockSpec((tm, tn), lambda i, j: (i, j)),
        out_shape=jax.ShapeDtypeStruct((t, n), BF16),
        scratch_shapes=[pltpu.VMEM((tm, d), BF16)],
        compiler_params=_cparams(("parallel", "arbitrary")),
        name="inproj",
    )(x2, gain, w)


def _rope_table_kernel(pos_ref, invf_ref, a_ref, bm_ref, bp_ref):
    ang = pos_ref[...].astype(F32) * invf_ref[...]
    c = jnp.cos(ang)
    s = jnp.sin(ang)
    r = lax.broadcasted_iota(I32, ang.shape, 1) % HEAD_DIM
    lo = r < ROT_HALF
    hi = jnp.logical_and(r >= ROT_HALF, r < ROT_DIM)
    a_ref[...] = jnp.where(r < ROT_DIM, c, 1.0)
    bm_ref[...] = jnp.where(lo, -s, 0.0)
    bp_ref[...] = jnp.where(hi, s, 0.0)


def _rope_tables(positions):
    t = positions.size
    posb = jnp.broadcast_to(positions.reshape(t, 1), (t, LANES))
    inv_freq = ROPE_THETA ** (-jnp.arange(0, ROT_DIM, 2, dtype=F32) / ROT_DIM)
    lane = jnp.arange(LANES) % ROT_HALF
    invf = inv_freq[lane].reshape(1, LANES)
    tm = _pick(t, (1024, 512, 256, 128))
    spec = pl.BlockSpec((tm, LANES), lambda i: (i, 0))
    return pl.pallas_call(
        _rope_table_kernel,
        grid=(t // tm,),
        in_specs=[spec, pl.BlockSpec((1, LANES), lambda i: (0, 0))],
        out_specs=[spec, spec, spec],
        out_shape=[jax.ShapeDtypeStruct((t, LANES), F32)] * 3,
        compiler_params=_cparams(("parallel",)),
        name="rope_tables",
    )(posb, invf)


def _rope_kernel(q_ref, k_ref, v_ref, qi_ref, kiw_ref, a_ref, bm_ref, bp_ref,
                 qo_ref, ko_ref, vto_ref, qio_ref, kio_ref, wo_ref, kn_ref):
    a = a_ref[...]
    bm = bm_ref[...]
    bp = bp_ref[...]

    def rope(x):
        w = x.shape[1]
        reps = w // LANES
        xf = x.astype(F32)
        up = pltpu.roll(xf, w - ROT_HALF, 1)
        dn = pltpu.roll(xf, ROT_HALF, 1)
        return (xf * jnp.tile(a, (1, reps)) + up * jnp.tile(bm, (1, reps))
                + dn * jnp.tile(bp, (1, reps)))

    qo_ref[...] = (rope(q_ref[...]) * (HEAD_DIM ** -0.5 * LOG2E)).astype(BF16)
    kr = rope(k_ref[...]).astype(BF16)
    ko_ref[...] = kr
    ksq = kr.astype(F32) * kr.astype(F32)
    lane = lax.broadcasted_iota(I32, (1, LANES), 1)
    kn = jnp.zeros((1, LANES), F32)
    for g in range(N_KV_HEADS):
        n2 = jnp.sum(ksq[:, g * HEAD_DIM:(g + 1) * HEAD_DIM], axis=1, keepdims=True)
        kn = jnp.where(lane == g, jnp.max(n2, axis=0, keepdims=True), kn)
    kn_ref[0] = jnp.broadcast_to(kn, (SUBLANES, LANES))
    vto_ref[0] = v_ref[...].astype(F32).T.astype(BF16)
    qio_ref[...] = (rope(qi_ref[...]) * (IDX_DIM ** -0.5)).astype(BF16)
    kiw = kiw_ref[...]
    kio_ref[...] = rope(kiw).astype(BF16)
    wo_ref[...] = kiw.astype(F32) * (N_IDX_HEADS ** -0.5)


def _rope(p2, tabs, bsz, seq):
    t = p2.shape[0]
    tm = _pick(seq, (512, 256, 128))
    nsb = seq // tm
    a, bm, bp = tabs

    def col(width, off):
        return pl.BlockSpec((tm, width), lambda i, o=off // width: (i, o))

    tab = pl.BlockSpec((tm, LANES), lambda i: (i, 0))
    row = lambda w: pl.BlockSpec((tm, w), lambda i: (i, 0))
    return pl.pallas_call(
        _rope_kernel,
        grid=(t // tm,),
        in_specs=[col(ATT_W, OFF_Q), col(KV_W, OFF_K), col(KV_W, OFF_V),
                  col(N_IDX_HEADS * IDX_DIM, OFF_QI), col(LANES, OFF_KI), tab, tab, tab],
        out_specs=[row(ATT_W), row(KV_W),
                   pl.BlockSpec((1, KV_W, tm), lambda i: (i // nsb, 0, i % nsb)),
                   row(N_IDX_HEADS * IDX_DIM), row(LANES), row(LANES),
                   pl.BlockSpec((1, SUBLANES, LANES), lambda i: (i, 0, 0))],
        out_shape=[jax.ShapeDtypeStruct((t, ATT_W), BF16),
                   jax.ShapeDtypeStruct((t, KV_W), BF16),
                   jax.ShapeDtypeStruct((bsz, KV_W, seq), BF16),
                   jax.ShapeDtypeStruct((t, N_IDX_HEADS * IDX_DIM), BF16),
                   jax.ShapeDtypeStruct((t, LANES), BF16),
                   jax.ShapeDtypeStruct((t, LANES), F32),
                   jax.ShapeDtypeStruct((t // tm, SUBLANES, LANES), F32)],
        compiler_params=_cparams(("parallel",)),
        name="rope",
    )(p2, p2, p2, p2, p2, a, bm, bp)


def _ssm_kernel(u_ref, bbd_ref, cbd_ref, a_ref, d_ref, wg_ref, o_ref, bu_ref, st_ref, *x_refs, bsz, chunk):
    @pl.when(pl.program_id(0) == 0)
    def _():
        st_ref[...] = jnp.zeros_like(st_ref)

    u = u_ref[...].reshape(bsz * chunk, SSM_W)
    gpt = MXU_TILE // SSM_STATE
    kin = gpt * SSM_GROUP
    for part in range(2):
        for n in range(SSM_NS // MXU_TILE):
            cols = slice(part * SSM_NS + n * MXU_TILE, part * SSM_NS + (n + 1) * MXU_TILE)
            bu_ref[:, cols] = jnp.dot(u[:, n * kin:(n + 1) * kin], bbd_ref[n * kin:(n + 1) * kin, cols],
                                      preferred_element_type=F32)
    ar = a_ref[0:1, :]
    ai = a_ref[1:2, :]

    def step(t, carry):
        out = []
        for b in range(bsz):
            xr, xi = carry[b]
            row = pl.ds(b * chunk + t, 1)
            nr = ar * xr - ai * xi + bu_ref[row, 0:SSM_NS]
            ni = ar * xi + ai * xr + bu_ref[row, SSM_NS:2 * SSM_NS]
            x_refs[b][pl.ds(t, 1), 0:SSM_NS] = nr
            x_refs[b][pl.ds(t, 1), SSM_NS:2 * SSM_NS] = ni
            out.append((nr, ni))
        return tuple(out)

    init = tuple((st_ref[2 * b:2 * b + 1, :], st_ref[2 * b + 1:2 * b + 2, :]) for b in range(bsz))
    last = lax.fori_loop(0, chunk, step, init, unroll=8)
    for b in range(bsz):
        st_ref[2 * b:2 * b + 1, :] = last[b][0]
        st_ref[2 * b + 1:2 * b + 2, :] = last[b][1]

    spt = (MXU_TILE // SSM_GROUP) * SSM_STATE
    ys = []
    for m in range(SSM_W // MXU_TILE):
        out = slice(m * MXU_TILE, (m + 1) * MXU_TILE)
        acc = None
        for part in range(2):
            rows = slice(part * SSM_NS + m * spt, part * SSM_NS + (m + 1) * spt)
            xs = jnp.concatenate([x_refs[b][:, rows].astype(BF16) for b in range(bsz)], axis=0)
            d = jnp.dot(xs, cbd_ref[rows, out], preferred_element_type=F32)
            acc = d if acc is None else acc + d
        ys.append(acc)
    y = jnp.concatenate(ys, axis=1) + d_ref[...] * u.astype(F32)
    z = jax.nn.gelu(y)
    gate = jnp.dot(z.astype(BF16), wg_ref[...], preferred_element_type=F32)
    o_ref[...] = (z * jax.nn.sigmoid(gate)).astype(o_ref.dtype).reshape(bsz, chunk, SSM_W)


def _ssm(p3, bbd, cbd, a_bar, d_skip, w_glu):
    bsz, seq, _ = p3.shape
    chunk = _pick(seq, (256, 128))
    const = lambda shape: pl.BlockSpec(shape, lambda c: (0,) * len(shape))
    return pl.pallas_call(
        functools.partial(_ssm_kernel, bsz=bsz, chunk=chunk),
        grid=(seq // chunk,),
        in_specs=[pl.BlockSpec((bsz, chunk, SSM_W), lambda c: (0, c, OFF_SSM // SSM_W)),
                  const((SSM_W, 2 * SSM_NS)), const((2 * SSM_NS, SSM_W)),
                  const((2, SSM_NS)), const((1, SSM_W)), const((SSM_W, SSM_W))],
        out_specs=pl.BlockSpec((bsz, chunk, SSM_W), lambda c: (0, c, 0)),
        out_shape=jax.ShapeDtypeStruct((bsz, seq, SSM_W), BF16),
        scratch_shapes=([pltpu.VMEM((bsz * chunk, 2 * SSM_NS), F32), pltpu.VMEM((2 * bsz, SSM_NS), F32)]
                        + [pltpu.VMEM((chunk, 2 * SSM_NS), F32)] * bsz),
        compiler_params=_cparams(("arbitrary",)),
        name="ssm",
    )(p3, bbd, cbd, a_bar, d_skip, w_glu)


def _dsa_kernel(qi_ref, ws_ref, ki_ref, q_ref, k_ref, vt_ref, kn_ref, o_ref,
                key_ref, hkey_ref, mask_ref, qit_ref, d0_ref, d1_ref, qt_ref, s0_ref, s1_ref, s2_ref,
                p0_ref, p1_ref, p2_ref, acc_ref, shift_ref,
                *, tq, seq, ksel):
    qb = pl.program_id(1)
    t0 = qb * tq
    cw = tq
    nc = qb + 1
    ca = 2 * tq
    nca = (qb + 2) // 2
    ks = 128
    vs = 64
    kf = float(ksel)
    jbits = (seq - 1).bit_length()
    qidx = t0 + lax.broadcasted_iota(I32, (1, tq), 1)

    def _select():
        qit = qi_ref[...].astype(F32).T
        qit_ref[...] = jnp.concatenate(
            [qit[h * IDX_DIM:(h + 1) * IDX_DIM, :] for h in range(N_IDX_HEADS)], axis=1).astype(BF16)
        wt = ws_ref[...].T
        wrow = jnp.concatenate([wt[IDX_DIM + h:IDX_DIM + h + 1, :] for h in range(N_IDX_HEADS)], axis=1)

        def idx_dots(c, d_ref):
            start = pl.multiple_of(c * cw, cw)
            kib = ki_ref[pl.ds(start, cw), :][:, :IDX_DIM]
            d_ref[...] = jnp.dot(kib, qit_ref[...], preferred_element_type=F32)

        def idx_scores(c, d_ref):
            for i in range(cw // ks):
                start = pl.multiple_of(c * cw, cw) + i * ks
                r = jnp.maximum(d_ref[i * ks:(i + 1) * ks, :], 0.0) * wrow
                isc = r[:, 0:tq]
                for h in range(1, N_IDX_HEADS):
                    isc = isc + r[:, h * tq:(h + 1) * tq]
                kidx = start + lax.broadcasted_iota(I32, (ks, tq), 0)
                isc = jnp.where(isc == 0.0, 0.0, isc)
                isc = jnp.where(kidx <= qidx, isc, -jnp.inf)
                bits = pltpu.bitcast(isc, I32)
                key = bits ^ ((bits >> 31) & 0x7FFFFFFF)
                key_ref[pl.ds(start, ks), :] = key
                hkey_ref[pl.ds(start, ks), :] = (key >> 16).astype(I16)

        idx_dots(0, d0_ref)

        def idx_pair(c2, _):
            c = 2 * c2
            idx_dots(c + 1, d1_ref)
            idx_scores(c, d0_ref)
            idx_dots(jnp.minimum(c + 2, nc - 1), d0_ref)
            idx_scores(c + 1, d1_ref)
            return 0

        lax.fori_loop(0, nc // 2, idx_pair, 0)

        @pl.when(nc % 2 == 1)
        def _():
            idx_scores(nc - 1, d0_ref)

        @pl.when(qb % 2 == 0)
        def _():
            mask_ref[pl.ds(pl.multiple_of((qb + 1) * cw, cw), cw), :] = jnp.zeros((cw, tq), BF16)

        small = qidx < ksel

        def key_rows(c):
            return pl.multiple_of(c * cw, cw) + lax.broadcasted_iota(I32, (cw, tq), 0)

        def count(pred):
            def body(c, cnt):
                blk = key_ref[pl.ds(pl.multiple_of(c * cw, cw), cw), :]
                hit = pred(blk, c)
                return cnt + jnp.sum(hit.reshape(cw // 32, 32, tq), axis=0)
            cnt = lax.fori_loop(0, nc, body, jnp.zeros((32, tq), F32))
            return jnp.sum(cnt, axis=0, keepdims=True)

        def count_ge(cand):
            return count(lambda x, c: jnp.where(x >= cand, 1.0, 0.0))

        def count_half_ge(c16):
            def body(c, cnt):
                blk = hkey_ref[pl.ds(pl.multiple_of(c * cw, cw), cw), :]
                hit = jnp.where(blk >= c16, jnp.ones((), I16), jnp.zeros((), I16))
                for r in range(0, cw, 32):
                    cnt = cnt + hit[r:r + 32, :]
                return cnt
            cnt = lax.fori_loop(0, nc, body, jnp.zeros((32, tq), I16))
            return jnp.sum(cnt.astype(F32), axis=0, keepdims=True)

        def count_high_ge(cand):
            return count_half_ge((cand >> 16).astype(I16))

        def count_low_ge(cand):
            return count_half_ge(((cand & 0xFFFF) - 32768).astype(I16))

        c0 = count_high_ge(jnp.zeros((1, tq), I32))
        thr = jnp.where(c0 >= kf, 0, INT_MIN).astype(I32)
        done = jnp.where(jnp.logical_or(small, c0 == kf), 1.0, 0.0)

        def bit_step(i, thr, done, counter):
            cand = thr + jnp.left_shift(jnp.int32(1), 30 - i)
            cnt = counter(cand)
            return (jnp.where(cnt >= kf, cand, thr),
                    jnp.maximum(done, jnp.where(cnt == kf, 1.0, 0.0)))

        thr, done = lax.fori_loop(0, HIGH_BITS, lambda i, c: bit_step(i, *c, count_high_ge), (thr, done))

        t16 = (thr >> 16).astype(I16)

        def low_body(c, _):
            rows = pl.ds(pl.multiple_of(c * cw, cw), cw)
            h = hkey_ref[rows, :]
            low = ((key_ref[rows, :] & 0xFFFF) - 32768).astype(I16)
            hkey_ref[rows, :] = jnp.where(h == t16, low, jnp.where(h > t16, jnp.full((), 32767, I16),
                                                                    jnp.full((), -32768, I16)))
            return 0

        lax.fori_loop(0, nc, low_body, 0)

        def w_cond(c):
            return jnp.logical_and(c[0] < 31, c[3] < 0.5)

        def w_body(c):
            thr, done = bit_step(c[0], c[1], c[2], count_low_ge)
            thr, done = bit_step(c[0] + 1, thr, done, count_low_ge)
            return c[0] + 2, thr, done, jnp.min(done)

        _, thr, done, settled = lax.while_loop(
            w_cond, w_body, (jnp.int32(HIGH_BITS), thr, done, jnp.min(done)))
        thr = jnp.where(small, INT_MIN, thr)

        def tie_index():
            n_ge = count_ge(thr)
            need = kf - count_ge(thr + 1)
            tie = jnp.logical_and(jnp.logical_not(small), n_ge > kf)

            def count_tie_below(cand):
                return count(lambda x, c: jnp.where(
                    x == thr, jnp.where(key_rows(c) < cand, 1.0, 0.0), 0.0))

            def jbit_body(i, jj):
                cand = jj + jnp.left_shift(jnp.int32(1), jbits - 1 - i)
                return jnp.where(count_tie_below(cand) < need, cand, jj)

            jj = lax.fori_loop(0, jbits, jbit_body, jnp.zeros((1, tq), I32))
            return jnp.where(tie, jj, seq)

        jmax = lax.cond(settled < 0.5, tie_index, lambda: jnp.full((1, tq), seq, I32))

        def mask_body(c, _):
            rows = pl.ds(pl.multiple_of(c * cw, cw), cw)
            x = key_ref[rows, :]
            kidx = key_rows(c)
            b = jnp.where(x > thr, 1.0, jnp.where(x == thr, jnp.where(kidx <= jmax, 1.0, 0.0), 0.0))
            mask_ref[rows, :] = jnp.where(kidx <= qidx, b, 0.0).astype(BF16)
            return 0

        lax.fori_loop(0, nc, mask_body, 0)

    _select()

    kn_all = jnp.max(kn_ref[...], axis=(0, 1), keepdims=True)[0]
    lane = lax.broadcasted_iota(I32, (1, LANES), 1)
    @pl.when(qb == 0)
    def _():
        qt_ref[...] = jnp.zeros(qt_ref.shape, BF16)

    for g in range(N_KV_HEADS):
        qt = q_ref[:, g * REP * HEAD_DIM:(g + 1) * REP * HEAD_DIM].astype(F32).T.astype(BF16)
        for r in range(REP):
            qt_ref[g, g * HEAD_DIM:(g + 1) * HEAD_DIM, r * tq:(r + 1) * tq] = qt[r * HEAD_DIM:(r + 1) * HEAD_DIM, :]
        qsq = qt.astype(F32) * qt.astype(F32)
        qn2 = jnp.concatenate([jnp.sum(qsq[r * HEAD_DIM:(r + 1) * HEAD_DIM, :], axis=0, keepdims=True)
                               for r in range(REP)], axis=1)
        kn2 = jnp.max(jnp.where(lane == g, kn_all, 0.0), axis=1, keepdims=True)
        shift_ref[g] = jnp.sqrt(qn2) * jnp.sqrt(kn2)
    acc_ref[...] = jnp.zeros(acc_ref.shape, F32)
    ones_rows = jnp.ones((ONES_ROWS, ca), BF16)

    n_items = N_KV_HEADS * nca

    def head_chunk(w):
        g = jnp.asarray(w, I32) // nca
        return g, w - g * nca

    def scores(w, s_ref):
        g, c = head_chunk(w)
        start = pl.multiple_of(c * ca, ca)
        s_ref[...] = jnp.dot(k_ref[pl.ds(start, ca), :], qt_ref[g], preferred_element_type=F32)

    def accumulate(g, c, scale, p_ref):
        vt = jnp.concatenate(
            [vt_ref[0, pl.ds(pl.multiple_of(g * HEAD_DIM, HEAD_DIM), HEAD_DIM),
                    pl.ds(pl.multiple_of(c * ca, ca), ca)], ones_rows], axis=0)
        acc_ref[g] = scale * acc_ref[g] + jnp.dot(vt, p_ref[...], preferred_element_type=F32)

    def mask_rows(c, i):
        return mask_ref[pl.ds(pl.multiple_of(c * ca, ca) + i * vs, vs), :]

    def softmax_chunk(w, s_ref, p_ref):
        g, c = head_chunk(w)
        shift = shift_ref[g]
        for i in range(ca // vs):
            p = jnp.exp2(s_ref[i * vs:(i + 1) * vs, :] - shift).astype(BF16)
            p_ref[i * vs:(i + 1) * vs, :] = p * jnp.tile(mask_rows(c, i), (1, REP))
        accumulate(g, c, 1.0, p_ref)

    scores(0, s0_ref)

    def att_triple(w3, _):
        w = 3 * w3
        scores(w + 1, s1_ref)
        softmax_chunk(w, s0_ref, p0_ref)
        scores(w + 2, s2_ref)
        softmax_chunk(w + 1, s1_ref, p1_ref)
        scores(jnp.minimum(w + 3, n_items - 1), s0_ref)
        softmax_chunk(w + 2, s2_ref, p2_ref)
        return 0

    lax.fori_loop(0, n_items // 3, att_triple, 0)
    rem = n_items % 3
    base = n_items - rem

    @pl.when(rem >= 1)
    def _():
        scores(jnp.minimum(base + 1, n_items - 1), s1_ref)
        softmax_chunk(base, s0_ref, p0_ref)

    @pl.when(rem == 2)
    def _():
        softmax_chunk(base + 1, s1_ref, p1_ref)

    def finish_head(g, _):
        @pl.when(jnp.min(acc_ref[g][HEAD_DIM:HEAD_DIM + 1, :]) < DENOM_FLOOR)
        def _():
            acc_ref[g] = jnp.zeros(acc_ref.shape[1:], F32)

            def exact_chunk(c, m_prev):
                scores(g * nca + c, s0_ref)
                m_new = m_prev
                for i in range(ca // vs):
                    b = jnp.where(jnp.tile(mask_rows(c, i), (1, REP)) > 0, s0_ref[i * vs:(i + 1) * vs, :], NEG_BIG)
                    s1_ref[i * vs:(i + 1) * vs, :] = b
                    m_new = jnp.maximum(m_new, jnp.max(b, axis=0, keepdims=True))
                for i in range(ca // vs):
                    p0_ref[i * vs:(i + 1) * vs, :] = jnp.exp2(s1_ref[i * vs:(i + 1) * vs, :] - m_new).astype(BF16)
                accumulate(g, c, jnp.exp2(m_prev - m_new), p0_ref)
                return m_new

            lax.fori_loop(0, nca, exact_chunk, jnp.full((1, REP * tq), NEG_BIG, F32))

        acc = acc_ref[g]
        out_t = acc[:HEAD_DIM] / acc[HEAD_DIM:HEAD_DIM + 1]
        out_t = jnp.concatenate([out_t[:, r * tq:(r + 1) * tq] for r in range(REP)], axis=0)
        o_ref[:, pl.ds(pl.multiple_of(g * REP * HEAD_DIM, REP * HEAD_DIM), REP * HEAD_DIM)] = (
            out_t.T.astype(o_ref.dtype))
        return 0

    lax.fori_loop(0, N_KV_HEADS, finish_head, 0)


def _dsa(qi_r, w_s, ki_r, q_r, k_r, v_t, k_n, bsz, seq):
    t = q_r.shape[0]
    tq = 256
    assert seq % (2 * tq) == 0
    ksel = min(TOPK_MAX, seq // 4)
    assert ksel <= tq
    nqb = seq // tq
    rowblk = lambda w: pl.BlockSpec((tq, w), lambda b, i: (b * nqb + i, 0))
    return pl.pallas_call(
        functools.partial(_dsa_kernel, tq=tq, seq=seq, ksel=ksel),
        grid=(bsz, nqb),
        in_specs=[rowblk(N_IDX_HEADS * IDX_DIM), rowblk(LANES),
                  pl.BlockSpec((seq, LANES), lambda b, i: (b, 0)),
                  rowblk(ATT_W),
                  pl.BlockSpec((seq, KV_W), lambda b, i: (b, 0)),
                  pl.BlockSpec((1, KV_W, seq), lambda b, i: (b, 0, 0)),
                  pl.BlockSpec((k_n.shape[0] // bsz, SUBLANES, LANES), lambda b, i: (b, 0, 0))],
        out_specs=rowblk(ATT_W),
        out_shape=jax.ShapeDtypeStruct((t, ATT_W), BF16),
        scratch_shapes=[pltpu.VMEM((seq, tq), I32),
                        pltpu.VMEM((seq, tq), I16),
                        pltpu.VMEM((seq, tq), BF16),
                        pltpu.VMEM((IDX_DIM, N_IDX_HEADS * tq), BF16),
                        pltpu.VMEM((tq, N_IDX_HEADS * tq), F32),
                        pltpu.VMEM((tq, N_IDX_HEADS * tq), F32),
                        pltpu.VMEM((N_KV_HEADS, KV_W, REP * tq), BF16),
                        pltpu.VMEM((2 * tq, REP * tq), F32),
                        pltpu.VMEM((2 * tq, REP * tq), F32),
                        pltpu.VMEM((2 * tq, REP * tq), F32),
                        pltpu.VMEM((2 * tq, REP * tq), BF16),
                        pltpu.VMEM((2 * tq, REP * tq), BF16),
                        pltpu.VMEM((2 * tq, REP * tq), BF16),
                        pltpu.VMEM((N_KV_HEADS, HEAD_DIM + ONES_ROWS, REP * tq), F32),
                        pltpu.VMEM((N_KV_HEADS, 1, REP * tq), F32)],
        compiler_params=_cparams(("parallel", "arbitrary")),
        name="dsa",
    )(qi_r, w_s, ki_r, q_r, k_r, v_t, k_n)


def _causal_conv3(cur, halo, w):
    tm = cur.shape[0]
    ext = jnp.concatenate([halo, cur], axis=0)
    return (w[2:3, :] * cur + w[1:2, :] * ext[SUBLANES - 1:SUBLANES - 1 + tm]
            + w[0:1, :] * ext[SUBLANES - 2:SUBLANES - 2 + tm])


def _merge_kernel(x_ref, ga_ref, gb_ref, gc_ref, scx_ref, scb_ref, scc_ref, hx_ref, hc_ref,
                  yb_ref, yc_ref, cw_ref, wa_ref, wb_ref, wc_ref, wo_ref, o_ref, *, tm, seq):
    first = (pl.program_id(0) * tm) % seq == 0
    cx = scc_ref[...].astype(F32) * scx_ref[...].astype(F32)
    halo = hc_ref[...].astype(F32) * hx_ref[...].astype(F32)
    halo = jnp.where(first, 0.0, halo)
    ya = scb_ref[...].astype(F32) * _causal_conv3(cx, halo, cw_ref[...])
    dot = lambda a, w: jnp.dot(a, w[...], preferred_element_type=F32)
    m = jax.nn.sigmoid(ga_ref[...].astype(F32)) * dot(ya.astype(BF16), wa_ref)
    m = m + jax.nn.sigmoid(gb_ref[...].astype(F32)) * dot(yb_ref[...], wb_ref)
    m = m + jax.nn.sigmoid(gc_ref[...].astype(F32)) * dot(yc_ref[...], wc_ref)
    o_ref[...] = x_ref[...] + dot(m.astype(BF16), wo_ref)


def _merge(x2, p2, y_b, y_c, conv_w, w_a, w_b, w_c, w_o, seq):
    t, d = x2.shape
    tm = _pick(seq, (256, 128))
    hb = tm // SUBLANES

    def col(width, off):
        return pl.BlockSpec((tm, width), lambda i, o=off // width: (i, o))

    def halo(off):
        return pl.BlockSpec((SUBLANES, SC_W), lambda i, o=off // SC_W: (jnp.maximum(i * hb - 1, 0), o))

    def const(shape):
        return pl.BlockSpec(shape, lambda i: (0, 0), pipeline_mode=pl.Buffered(1))

    row = lambda w: pl.BlockSpec((tm, w), lambda i: (i, 0))
    return pl.pallas_call(
        functools.partial(_merge_kernel, tm=tm, seq=seq),
        grid=(t // tm,),
        in_specs=[row(d), col(d, 0), col(d, d), col(d, 2 * d),
                  col(SC_W, OFF_SCX), col(SC_W, OFF_SCB), col(SC_W, OFF_SCC),
                  halo(OFF_SCX), halo(OFF_SCC), row(SSM_W), row(ATT_W),
                  const((SC_CONV, SC_W)), const((SC_W, d)), const((SSM_W, d)),
                  const((ATT_W, d)), const((d, d))],
        out_specs=row(d),
        out_shape=jax.ShapeDtypeStruct((t, d), F32),
        compiler_params=_cparams(("parallel",)),
        name="merge",
    )(x2, p2, p2, p2, p2, p2, p2, p2, p2, y_b, y_c, conv_w, w_a, w_b, w_c, w_o)


def _ffn_kernel(x_ref, g_ref, wg_ref, wu_ref, cg_ref, cu_ref, wd_ref, og_ref, o_ref,
                h_ref, halo_ref, *, tm, seq, out_norm):
    j = pl.program_id(1)

    @pl.when(j == 0)
    def _():
        x = x_ref[...]
        ms = jnp.mean(x * x, axis=-1, keepdims=True)
        h_ref[...] = (x * lax.rsqrt(ms + NORM_EPS) * g_ref[...]).astype(BF16)
        o_ref[...] = jnp.zeros_like(o_ref)

    first = (pl.program_id(0) * tm) % seq == 0
    h = h_ref[...]
    gt = jnp.dot(h, wg_ref[...], preferred_element_type=F32)
    ut = jnp.dot(h, wu_ref[...], preferred_element_type=F32)
    gh = jnp.where(first, 0.0, halo_ref[j, 0])
    uh = jnp.where(first, 0.0, halo_ref[j, 1])
    halo_ref[j, 0] = gt[tm - SUBLANES:]
    halo_ref[j, 1] = ut[tm - SUBLANES:]
    gc = _causal_conv3(gt, gh, cg_ref[...])
    uc = _causal_conv3(ut, uh, cu_ref[...])
    act = (jax.nn.silu(gc) * uc).astype(BF16)
    o_ref[...] += jnp.dot(act, wd_ref[...], preferred_element_type=F32)

    @pl.when(j == pl.num_programs(1) - 1)
    def _():
        y = x_ref[...] + o_ref[...]
        if out_norm:
            ms = jnp.mean(y * y, axis=-1, keepdims=True)
            y = y * lax.rsqrt(ms + NORM_EPS) * og_ref[...]
        o_ref[...] = y


def _ffn(x2, gain, w_up, conv_w, w_down, out_gain, seq, out_norm):
    t, d = x2.shape
    dff = w_down.shape[0]
    tm = _pick(seq, (512, 256, 128))
    tf = _pick(dff, (512, 256, 128))
    nf = dff // tf
    return pl.pallas_call(
        functools.partial(_ffn_kernel, tm=tm, seq=seq, out_norm=out_norm),
        grid=(t // tm, nf),
        in_specs=[pl.BlockSpec((tm, d), lambda i, j: (i, 0)),
                  pl.BlockSpec((1, d), lambda i, j: (0, 0)),
                  pl.BlockSpec((d, tf), lambda i, j: (0, j)),
                  pl.BlockSpec((d, tf), lambda i, j: (0, j + nf)),
                  pl.BlockSpec((FFN_CONV, tf), lambda i, j: (0, j)),
                  pl.BlockSpec((FFN_CONV, tf), lambda i, j: (0, j + nf)),
                  pl.BlockSpec((tf, d), lambda i, j: (j, 0)),
                  pl.BlockSpec((1, d), lambda i, j: (0, 0))],
        out_specs=pl.BlockSpec((tm, d), lambda i, j: (i, 0)),
        out_shape=jax.ShapeDtypeStruct((t, d), F32),
        scratch_shapes=[pltpu.VMEM((tm, d), BF16), pltpu.VMEM((nf, 2, SUBLANES, tf), F32)],
        compiler_params=_cparams(("arbitrary", "arbitrary")),
        name="ffn",
    )(x2, gain, w_up, w_up, conv_w, conv_w, w_down, out_gain)


def _prep_w_in(w):
    d = w.shape[0]
    pad = jnp.zeros((d, NP_COLS - w.shape[1]), BF16)
    return jnp.concatenate([w[:, _ORIG_GATE_OFF:].astype(BF16), w[:, :_ORIG_GATE_OFF].astype(BF16), pad], axis=1)


def _prep_ssm(a_re, a_im, log_dt, b_re, b_im, c_re, c_im):
    ar, ai = a_re.astype(F32), a_im.astype(F32)
    dt = jnp.exp(log_dt.astype(F32))[:, None]
    mag = jnp.exp(dt * ar)
    abr, abi = mag * jnp.cos(dt * ai), mag * jnp.sin(dt * ai)
    den = ar * ar + ai * ai
    cr = ((abr - 1.0) * ar + abi * ai) / den
    ci = (abi * ar - (abr - 1.0) * ai) / den
    br, bi = b_re.astype(F32), b_im.astype(F32)
    bbr = cr[..., None] * br - ci[..., None] * bi
    bbi = cr[..., None] * bi + ci[..., None] * br
    eye = jnp.eye(SSM_GROUPS, dtype=F32)

    def bdiag_in(m):
        return (jnp.transpose(m, (0, 2, 1))[:, :, None, :] * eye[:, None, :, None]).reshape(
            SSM_W, SSM_NS)

    def bdiag_out(m):
        return (jnp.transpose(m, (0, 2, 1))[:, :, None, :] * eye[:, None, :, None]).reshape(
            SSM_NS, SSM_W)

    bbd = jnp.concatenate([bdiag_in(bbr), bdiag_in(bbi)], axis=1).astype(BF16)
    cbd = jnp.concatenate([bdiag_out(c_re.astype(F32)), bdiag_out(-c_im.astype(F32))], axis=0).astype(BF16)
    a_rows = jnp.stack([abr.reshape(-1), abi.reshape(-1)], axis=0)
    return bbd, cbd, a_rows


def kernel(x, positions, norm_mix, w_in, sc_conv, ssm_a_re, ssm_a_im, ssm_log_dt, ssm_b_re, ssm_b_im,
           ssm_c_re, ssm_c_im, ssm_d, ssm_glu, w_branch_a, w_branch_b, w_branch_c, w_out, norm_ffn,
           w_up, ffn_conv, w_down, norm_final):
    bsz, seq, d = x.shape
    depth = w_in.shape[0]
    t = bsz * seq
    x2 = x.reshape(t, d).astype(F32)
    tabs = _rope_tables(positions.astype(I32))
    for l in range(depth):
        p2 = _inproj(x2, norm_mix[l].reshape(1, d).astype(F32), _prep_w_in(w_in[l]))
        q_r, k_r, v_t, qi_r, ki_r, w_s, k_n = _rope(p2, tabs, bsz, seq)
        bbd, cbd, a_rows = _prep_ssm(ssm_a_re[l], ssm_a_im[l], ssm_log_dt[l], ssm_b_re[l], ssm_b_im[l],
                                     ssm_c_re[l], ssm_c_im[l])
        y_b = _ssm(p2.reshape(bsz, seq, NP_COLS), bbd, cbd, a_rows,
                   ssm_d[l].reshape(1, SSM_W).astype(F32), ssm_glu[l].astype(BF16))
        y_c = _dsa(qi_r, w_s, ki_r, q_r, k_r, v_t, k_n, bsz, seq)
        x2 = _merge(x2, p2, y_b.reshape(t, SSM_W), y_c, sc_conv[l].astype(F32),
                    w_branch_a[l].astype(BF16), w_branch_b[l].astype(BF16), w_branch_c[l].astype(BF16),
                    w_out[l].astype(BF16), seq)
        x2 = _ffn(x2, norm_ffn[l].reshape(1, d).astype(F32), w_up[l].astype(BF16),
                  ffn_conv[l].astype(F32), w_down[l].astype(BF16),
                  norm_final.reshape(1, d).astype(F32), seq, out_norm=(l == depth - 1))
    return x2.reshape(bsz, seq, d).astype(x.dtype)
```

```python
import functools
import math

import jax
import jax.numpy as jnp
from jax import lax
from jax.experimental import pallas as pl
from jax.experimental.pallas import tpu as pltpu

F32 = jnp.float32
BF16 = jnp.bfloat16
I32 = jnp.int32
I16 = jnp.int16

SC_W = 512
SC_CONV = 3
SSM_W = 512
SSM_GROUP = 16
SSM_GROUPS = SSM_W // SSM_GROUP
SSM_STATE = 64
SSM_NS = SSM_GROUPS * SSM_STATE
N_Q_HEADS = 16
N_KV_HEADS = 4
HEAD_DIM = 64
ATT_W = N_Q_HEADS * HEAD_DIM
KV_W = N_KV_HEADS * HEAD_DIM
REP = N_Q_HEADS // N_KV_HEADS
ROT_DIM = HEAD_DIM // 4
ROT_HALF = ROT_DIM // 2
ROPE_THETA = 500000.0
N_IDX_HEADS = 8
IDX_DIM = 64
TOPK_MAX = 256
D_FF = 5632
FFN_CONV = 3
NORM_EPS = 1e-6

LANES = 128
SUBLANES = 8
ONES_ROWS = 16
MXU_TILE = 256
VMEM_LIMIT = 56 * 1024 * 1024
NEG_BIG = -1e30
LOG2E = math.log2(math.e)
DENOM_FLOOR = 2.0 ** -64
INT_MIN = -(2 ** 31)
HIGH_BITS = 15

_ORIG_GATE_OFF = 3 * SC_W + SSM_W + ATT_W + 2 * KV_W + N_IDX_HEADS * IDX_DIM + IDX_DIM + N_IDX_HEADS
OFF_SCX = 3 * 2048
OFF_SCB = OFF_SCX + SC_W
OFF_SCC = OFF_SCB + SC_W
OFF_SSM = OFF_SCC + SC_W
OFF_Q = OFF_SSM + SSM_W
OFF_K = OFF_Q + ATT_W
OFF_V = OFF_K + KV_W
OFF_QI = OFF_V + KV_W
OFF_KI = OFF_QI + N_IDX_HEADS * IDX_DIM
INPROJ_TN = 7 * 256
NP_COLS = -(-(OFF_KI + LANES) // INPROJ_TN) * INPROJ_TN


def _cparams(sem, vmem=VMEM_LIMIT):
    return pltpu.CompilerParams(dimension_semantics=sem, vmem_limit_bytes=vmem)


def _pick(n, prefs):
    for p in prefs:
        if n % p == 0:
            return p
    return n


def _inproj_kernel(x_ref, g_ref, w_ref, o_ref, h_ref):
    @pl.when(pl.program_id(1) == 0)
    def _():
        x = x_ref[...]
        ms = jnp.mean(x * x, axis=-1, keepdims=True)
        h_ref[...] = (x * lax.rsqrt(ms + NORM_EPS) * g_ref[...]).astype(BF16)

    o_ref[...] = jnp.dot(h_ref[...], w_ref[...], preferred_element_type=F32).astype(o_ref.dtype)


def _inproj(x2, gain, w):
    t, d = x2.shape
    n = w.shape[1]
    tm = _pick(t, (1024, 512, 256, 128))
    tn = _pick(n, (INPROJ_TN, 384, 128))
    return pl.pallas_call(
        _inproj_kernel,
        grid=(t // tm, n // tn),
        in_specs=[pl.BlockSpec((tm, d), lambda i, j: (i, 0)),
                  pl.BlockSpec((1, d), lambda i, j: (0, 0)),
                  pl.BlockSpec((d, tn), lambda i, j: (0, j))],
        out_specs=pl.BlockSpec((tm, tn), lambda i, j: (i, j)),
        out_shape=jax.ShapeDtypeStruct((t, n), BF16),
        scratch_shapes=[pltpu.VMEM((tm, d), BF16)],
        compiler_params=_cparams(("parallel", "arbitrary")),
        name="inproj",
    )(x2, gain, w)


def _rope_table_kernel(pos_ref, invf_ref, a_ref, bm_ref, bp_ref):
    ang = pos_ref[...].astype(F32) * invf_ref[...]
    c = jnp.cos(ang)
    s = jnp.sin(ang)
    r = lax.broadcasted_iota(I32, ang.shape, 1) % HEAD_DIM
    lo = r < ROT_HALF
    hi = jnp.logical_and(r >= ROT_HALF, r < ROT_DIM)
    a_ref[...] = jnp.where(r < ROT_DIM, c, 1.0)
    bm_ref[...] = jnp.where(lo, -s, 0.0)
    bp_ref[...] = jnp.where(hi, s, 0.0)


def _rope_tables(positions):
    t = positions.size
    posb = jnp.broadcast_to(positions.reshape(t, 1), (t, LANES))
    inv_freq = ROPE_THETA ** (-jnp.arange(0, ROT_DIM, 2, dtype=F32) / ROT_DIM)
    lane = jnp.arange(LANES) % ROT_HALF
    invf = inv_freq[lane].reshape(1, LANES)
    tm = _pick(t, (1024, 512, 256, 128))
    spec = pl.BlockSpec((tm, LANES), lambda i: (i, 0))
    return pl.pallas_call(
        _rope_table_kernel,
        grid=(t // tm,),
        in_specs=[spec, pl.BlockSpec((1, LANES), lambda i: (0, 0))],
        out_specs=[spec, spec, spec],
        out_shape=[jax.ShapeDtypeStruct((t, LANES), F32)] * 3,
        compiler_params=_cparams(("parallel",)),
        name="rope_tables",
    )(posb, invf)


def _rope_kernel(q_ref, k_ref, v_ref, qi_ref, kiw_ref, a_ref, bm_ref, bp_ref,
                 qo_ref, ko_ref, vto_ref, qio_ref, kio_ref, wo_ref, kn_ref):
    a = a_ref[...]
    bm = bm_ref[...]
    bp = bp_ref[...]

    def rope(x):
        w = x.shape[1]
        reps = w // LANES
        xf = x.astype(F32)
        up = pltpu.roll(xf, w - ROT_HALF, 1)
        dn = pltpu.roll(xf, ROT_HALF, 1)
        return (xf * jnp.tile(a, (1, reps)) + up * jnp.tile(bm, (1, reps))
                + dn * jnp.tile(bp, (1, reps)))

    qo_ref[...] = (rope(q_ref[...]) * (HEAD_DIM ** -0.5 * LOG2E)).astype(BF16)
    kr = rope(k_ref[...]).astype(BF16)
    ko_ref[...] = kr
    ksq = kr.astype(F32) * kr.astype(F32)
    lane = lax.broadcasted_iota(I32, (1, LANES), 1)
    kn = jnp.zeros((1, LANES), F32)
    for g in range(N_KV_HEADS):
        n2 = jnp.sum(ksq[:, g * HEAD_DIM:(g + 1) * HEAD_DIM], axis=1, keepdims=True)
        kn = jnp.where(lane == g, jnp.max(n2, axis=0, keepdims=True), kn)
    kn_ref[0] = jnp.broadcast_to(kn, (SUBLANES, LANES))
    vto_ref[0] = v_ref[...].astype(F32).T.astype(BF16)
    qio_ref[...] = (rope(qi_ref[...]) * (IDX_DIM ** -0.5)).astype(BF16)
    kiw = kiw_ref[...]
    kio_ref[...] = rope(kiw).astype(BF16)
    wo_ref[...] = kiw.astype(F32) * (N_IDX_HEADS ** -0.5)


def _rope(p2, tabs, bsz, seq):
    t = p2.shape[0]
    tm = _pick(seq, (512, 256, 128))
    nsb = seq // tm
    a, bm, bp = tabs

    def col(width, off):
        return pl.BlockSpec((tm, width), lambda i, o=off // width: (i, o))

    tab = pl.BlockSpec((tm, LANES), lambda i: (i, 0))
    row = lambda w: pl.BlockSpec((tm, w), lambda i: (i, 0))
    return pl.pallas_call(
        _rope_kernel,
        grid=(t // tm,),
        in_specs=[col(ATT_W, OFF_Q), col(KV_W, OFF_K), col(KV_W, OFF_V),
                  col(N_IDX_HEADS * IDX_DIM, OFF_QI), col(LANES, OFF_KI), tab, tab, tab],
        out_specs=[row(ATT_W), row(KV_W),
                   pl.BlockSpec((1, KV_W, tm), lambda i: (i // nsb, 0, i % nsb)),
                   row(N_IDX_HEADS * IDX_DIM), row(LANES), row(LANES),
                   pl.BlockSpec((1, SUBLANES, LANES), lambda i: (i, 0, 0))],
        out_shape=[jax.ShapeDtypeStruct((t, ATT_W), BF16),
                   jax.ShapeDtypeStruct((t, KV_W), BF16),
                   jax.ShapeDtypeStruct((bsz, KV_W, seq), BF16),
                   jax.ShapeDtypeStruct((t, N_IDX_HEADS * IDX_DIM), BF16),
                   jax.ShapeDtypeStruct((t, LANES), BF16),
                   jax.ShapeDtypeStruct((t, LANES), F32),
                   jax.ShapeDtypeStruct((t // tm, SUBLANES, LANES), F32)],
        compiler_params=_cparams(("parallel",)),
        name="rope",
    )(p2, p2, p2, p2, p2, a, bm, bp)


def _ssm_kernel(u_ref, bbd_ref, cbd_ref, a_ref, d_ref, wg_ref, o_ref, bu_ref, st_ref, *x_refs, bsz, chunk):
    @pl.when(pl.program_id(0) == 0)
    def _():
        st_ref[...] = jnp.zeros_like(st_ref)

    u = u_ref[...].reshape(bsz * chunk, SSM_W)
    gpt = MXU_TILE // SSM_STATE
    kin = gpt * SSM_GROUP
    for part in range(2):
        for n in range(SSM_NS // MXU_TILE):
            cols = slice(part * SSM_NS + n * MXU_TILE, part * SSM_NS + (n + 1) * MXU_TILE)
            bu_ref[:, cols] = jnp.dot(u[:, n * kin:(n + 1) * kin], bbd_ref[n * kin:(n + 1) * kin, cols],
                                      preferred_element_type=F32)
    ar = a_ref[0:1, :]
    ai = a_ref[1:2, :]

    def step(t, carry):
        out = []
        for b in range(bsz):
            xr, xi = carry[b]
            row = pl.ds(b * chunk + t, 1)
            nr = ar * xr - ai * xi + bu_ref[row, 0:SSM_NS]
            ni = ar * xi + ai * xr + bu_ref[row, SSM_NS:2 * SSM_NS]
            x_refs[b][pl.ds(t, 1), 0:SSM_NS] = nr
            x_refs[b][pl.ds(t, 1), SSM_NS:2 * SSM_NS] = ni
            out.append((nr, ni))
        return tuple(out)

    init = tuple((st_ref[2 * b:2 * b + 1, :], st_ref[2 * b + 1:2 * b + 2, :]) for b in range(bsz))
    last = lax.fori_loop(0, chunk, step, init, unroll=8)
    for b in range(bsz):
        st_ref[2 * b:2 * b + 1, :] = last[b][0]
        st_ref[2 * b + 1:2 * b + 2, :] = last[b][1]

    spt = (MXU_TILE // SSM_GROUP) * SSM_STATE
    ys = []
    for m in range(SSM_W // MXU_TILE):
        out = slice(m * MXU_TILE, (m + 1) * MXU_TILE)
        acc = None
        for part in range(2):
            rows = slice(part * SSM_NS + m * spt, part * SSM_NS + (m + 1) * spt)
            xs = jnp.concatenate([x_refs[b][:, rows].astype(BF16) for b in range(bsz)], axis=0)
            d = jnp.dot(xs, cbd_ref[rows, out], preferred_element_type=F32)
            acc = d if acc is None else acc + d
        ys.append(acc)
    y = jnp.concatenate(ys, axis=1) + d_ref[...] * u.astype(F32)
    z = jax.nn.gelu(y)
    gate = jnp.dot(z.astype(BF16), wg_ref[...], preferred_element_type=F32)
    o_ref[...] = (z * jax.nn.sigmoid(gate)).astype(o_ref.dtype).reshape(bsz, chunk, SSM_W)


def _ssm(p3, bbd, cbd, a_bar, d_skip, w_glu):
    bsz, seq, _ = p3.shape
    chunk = _pick(seq, (256, 128))
    const = lambda shape: pl.BlockSpec(shape, lambda c: (0,) * len(shape))
    return pl.pallas_call(
        functools.partial(_ssm_kernel, bsz=bsz, chunk=chunk),
        grid=(seq // chunk,),
        in_specs=[pl.BlockSpec((bsz, chunk, SSM_W), lambda c: (0, c, OFF_SSM // SSM_W)),
                  const((SSM_W, 2 * SSM_NS)), const((2 * SSM_NS, SSM_W)),
                  const((2, SSM_NS)), const((1, SSM_W)), const((SSM_W, SSM_W))],
        out_specs=pl.BlockSpec((bsz, chunk, SSM_W), lambda c: (0, c, 0)),
        out_shape=jax.ShapeDtypeStruct((bsz, seq, SSM_W), BF16),
        scratch_shapes=([pltpu.VMEM((bsz * chunk, 2 * SSM_NS), F32), pltpu.VMEM((2 * bsz, SSM_NS), F32)]
                        + [pltpu.VMEM((chunk, 2 * SSM_NS), F32)] * bsz),
        compiler_params=_cparams(("arbitrary",)),
        name="ssm",
    )(p3, bbd, cbd, a_bar, d_skip, w_glu)


def _dsa_kernel(qi_ref, ws_ref, ki_ref, q_ref, k_ref, vt_ref, kn_ref, o_ref,
                key_ref, hkey_ref, mask_ref, qit_ref, d0_ref, d1_ref, qt_ref, s0_ref, s1_ref, s2_ref,
                p0_ref, p1_ref, p2_ref, acc_ref, shift_ref,
                *, tq, seq, ksel):
    qb = pl.program_id(1)
    t0 = qb * tq
    cw = tq
    nc = qb + 1
    ca = 2 * tq
    nca = (qb + 2) // 2
    ks = 128
    vs = 64
    kf = float(ksel)
    jbits = (seq - 1).bit_length()
    qidx = t0 + lax.broadcasted_iota(I32, (1, tq), 1)

    def _select():
        qit = qi_ref[...].astype(F32).T
        qit_ref[...] = jnp.concatenate(
            [qit[h * IDX_DIM:(h + 1) * IDX_DIM, :] for h in range(N_IDX_HEADS)], axis=1).astype(BF16)
        wt = ws_ref[...].T
        wrow = jnp.concatenate([wt[IDX_DIM + h:IDX_DIM + h + 1, :] for h in range(N_IDX_HEADS)], axis=1)

        def idx_dots(c, d_ref):
            start = pl.multiple_of(c * cw, cw)
            kib = ki_ref[pl.ds(start, cw), :][:, :IDX_DIM]
            d_ref[...] = jnp.dot(kib, qit_ref[...], preferred_element_type=F32)

        def idx_scores(c, d_ref):
            for i in range(cw // ks):
                start = pl.multiple_of(c * cw, cw) + i * ks
                r = jnp.maximum(d_ref[i * ks:(i + 1) * ks, :], 0.0) * wrow
                isc = r[:, 0:tq]
                for h in range(1, N_IDX_HEADS):
                    isc = isc + r[:, h * tq:(h + 1) * tq]
                kidx = start + lax.broadcasted_iota(I32, (ks, tq), 0)
                isc = jnp.where(isc == 0.0, 0.0, isc)
                isc = jnp.where(kidx <= qidx, isc, -jnp.inf)
                bits = pltpu.bitcast(isc, I32)
                key = bits ^ ((bits >> 31) & 0x7FFFFFFF)
                key_ref[pl.ds(start, ks), :] = key
                hkey_ref[pl.ds(start, ks), :] = (key >> 16).astype(I16)

        idx_dots(0, d0_ref)

        def idx_pair(c2, _):
            c = 2 * c2
            idx_dots(c + 1, d1_ref)
            idx_scores(c, d0_ref)
            idx_dots(jnp.minimum(c + 2, nc - 1), d0_ref)
            idx_scores(c + 1, d1_ref)
            return 0

        lax.fori_loop(0, nc // 2, idx_pair, 0)

        @pl.when(nc % 2 == 1)
        def _():
            idx_scores(nc - 1, d0_ref)

        @pl.when(qb % 2 == 0)
        def _():
            mask_ref[pl.ds(pl.multiple_of((qb + 1) * cw, cw), cw), :] = jnp.zeros((cw, tq), BF16)
            hkey_ref[pl.ds(pl.multiple_of((qb + 1) * cw, cw), cw), :] = jnp.full((cw, tq), -32768, I16)

        small = qidx < ksel

        def key_rows(c):
            return pl.multiple_of(c * cw, cw) + lax.broadcasted_iota(I32, (cw, tq), 0)

        def count(pred):
            def body(c, cnt):
                blk = key_ref[pl.ds(pl.multiple_of(c * cw, cw), cw), :]
                hit = pred(blk, c)
                return cnt + jnp.sum(hit.reshape(cw // 32, 32, tq), axis=0)
            cnt = lax.fori_loop(0, nc, body, jnp.zeros((32, tq), F32))
            return jnp.sum(cnt, axis=0, keepdims=True)

        def count_ge(cand):
            return count(lambda x, c: jnp.where(x >= cand, 1.0, 0.0))

        def count_half_ge(c16):
            def body(c2, cnt):
                blk = hkey_ref[pl.ds(pl.multiple_of(c2 * ca, ca), ca), :]
                hit = jnp.where(blk >= c16, jnp.ones((), I16), jnp.zeros((), I16))
                for r in range(0, ca, 32):
                    cnt = cnt + hit[r:r + 32, :]
                return cnt
            cnt = lax.fori_loop(0, nca, body, jnp.zeros((32, tq), I16))
            return jnp.sum(cnt.astype(F32), axis=0, keepdims=True)

        def count_high_ge(cand):
            return count_half_ge((cand >> 16).astype(I16))

        def count_low_ge(cand):
            return count_half_ge(((cand & 0xFFFF) - 32768).astype(I16))

        c0 = count_high_ge(jnp.zeros((1, tq), I32))
        thr = jnp.where(c0 >= kf, 0, INT_MIN).astype(I32)
        done = jnp.where(jnp.logical_or(small, c0 == kf), 1.0, 0.0)

        def bit_step(i, thr, done, counter):
            cand = thr + jnp.left_shift(jnp.int32(1), 30 - i)
            cnt = counter(cand)
            return (jnp.where(cnt >= kf, cand, thr),
                    jnp.maximum(done, jnp.where(cnt == kf, 1.0, 0.0)))

        thr, done = lax.fori_loop(0, HIGH_BITS, lambda i, c: bit_step(i, *c, count_high_ge), (thr, done))

        t16 = (thr >> 16).astype(I16)

        def low_body(c, _):
            rows = pl.ds(pl.multiple_of(c * cw, cw), cw)
            h = hkey_ref[rows, :]
            low = ((key_ref[rows, :] & 0xFFFF) - 32768).astype(I16)
            hkey_ref[rows, :] = jnp.where(h == t16, low, jnp.where(h > t16, jnp.full((), 32767, I16),
                                                                    jnp.full((), -32768, I16)))
            return 0

        lax.fori_loop(0, nc, low_body, 0)

        def w_cond(c):
            return jnp.logical_and(c[0] < 31, c[3] < 0.5)

        def w_body(c):
            thr, done = bit_step(c[0], c[1], c[2], count_low_ge)
            thr, done = bit_step(c[0] + 1, thr, done, count_low_ge)
            return c[0] + 2, thr, done, jnp.min(done)

        _, thr, done, settled = lax.while_loop(
            w_cond, w_body, (jnp.int32(HIGH_BITS), thr, done, jnp.min(done)))
        thr = jnp.where(small, INT_MIN, thr)

        def tie_index():
            n_ge = count_ge(thr)
            need = kf - count_ge(thr + 1)
            tie = jnp.logical_and(jnp.logical_not(small), n_ge > kf)

            def count_tie_below(cand):
                return count(lambda x, c: jnp.where(
                    x == thr, jnp.where(key_rows(c) < cand, 1.0, 0.0), 0.0))

            def jbit_body(i, jj):
                cand = jj + jnp.left_shift(jnp.int32(1), jbits - 1 - i)
                return jnp.where(count_tie_below(cand) < need, cand, jj)

            jj = lax.fori_loop(0, jbits, jbit_body, jnp.zeros((1, tq), I32))
            return jnp.where(tie, jj, seq)

        jmax = lax.cond(settled < 0.5, tie_index, lambda: jnp.full((1, tq), seq, I32))

        def mask_body(c, _):
            rows = pl.ds(pl.multiple_of(c * cw, cw), cw)
            x = key_ref[rows, :]
            kidx = key_rows(c)
            b = jnp.where(x > thr, 1.0, jnp.where(x == thr, jnp.where(kidx <= jmax, 1.0, 0.0), 0.0))
            mask_ref[rows, :] = jnp.where(kidx <= qidx, b, 0.0).astype(BF16)
            return 0

        lax.fori_loop(0, nc, mask_body, 0)

    _select()

    kn_all = jnp.max(kn_ref[...], axis=(0, 1), keepdims=True)[0]
    lane = lax.broadcasted_iota(I32, (1, LANES), 1)
    @pl.when(qb == 0)
    def _():
        qt_ref[...] = jnp.zeros(qt_ref.shape, BF16)

    for g in range(N_KV_HEADS):
        qt = q_ref[:, g * REP * HEAD_DIM:(g + 1) * REP * HEAD_DIM].astype(F32).T.astype(BF16)
        for r in range(REP):
            qt_ref[g, g * HEAD_DIM:(g + 1) * HEAD_DIM, r * tq:(r + 1) * tq] = qt[r * HEAD_DIM:(r + 1) * HEAD_DIM, :]
        qsq = qt.astype(F32) * qt.astype(F32)
        qn2 = jnp.concatenate([jnp.sum(qsq[r * HEAD_DIM:(r + 1) * HEAD_DIM, :], axis=0, keepdims=True)
                               for r in range(REP)], axis=1)
        kn2 = jnp.max(jnp.where(lane == g, kn_all, 0.0), axis=1, keepdims=True)
        shift_ref[g] = jnp.sqrt(qn2) * jnp.sqrt(kn2)
    acc_ref[...] = jnp.zeros(acc_ref.shape, F32)
    ones_rows = jnp.ones((ONES_ROWS, ca), BF16)

    n_items = N_KV_HEADS * nca

    def head_chunk(w):
        g = jnp.asarray(w, I32) // nca
        return g, w - g * nca

    def scores(w, s_ref):
        g, c = head_chunk(w)
        start = pl.multiple_of(c * ca, ca)
        s_ref[...] = jnp.dot(k_ref[pl.ds(start, ca), :], qt_ref[g], preferred_element_type=F32)

    def accumulate(g, c, scale, p_ref):
        vt = jnp.concatenate(
            [vt_ref[0, pl.ds(pl.multiple_of(g * HEAD_DIM, HEAD_DIM), HEAD_DIM),
                    pl.ds(pl.multiple_of(c * ca, ca), ca)], ones_rows], axis=0)
        acc_ref[g] = scale * acc_ref[g] + jnp.dot(vt, p_ref[...], preferred_element_type=F32)

    def mask_rows(c, i):
        return mask_ref[pl.ds(pl.multiple_of(c * ca, ca) + i * vs, vs), :]

    def softmax_chunk(w, s_ref, p_ref):
        g, c = head_chunk(w)
        shift = shift_ref[g]
        for i in range(ca // vs):
            p = jnp.exp2(s_ref[i * vs:(i + 1) * vs, :] - shift).astype(BF16)
            p_ref[i * vs:(i + 1) * vs, :] = p * jnp.tile(mask_rows(c, i), (1, REP))
        accumulate(g, c, 1.0, p_ref)

    scores(0, s0_ref)

    def att_triple(w3, _):
        w = 3 * w3
        scores(w + 1, s1_ref)
        softmax_chunk(w, s0_ref, p0_ref)
        scores(w + 2, s2_ref)
        softmax_chunk(w + 1, s1_ref, p1_ref)
        scores(jnp.minimum(w + 3, n_items - 1), s0_ref)
        softmax_chunk(w + 2, s2_ref, p2_ref)
        return 0

    lax.fori_loop(0, n_items // 3, att_triple, 0)
    rem = n_items % 3
    base = n_items - rem

    @pl.when(rem >= 1)
    def _():
        scores(jnp.minimum(base + 1, n_items - 1), s1_ref)
        softmax_chunk(base, s0_ref, p0_ref)

    @pl.when(rem == 2)
    def _():
        softmax_chunk(base + 1, s1_ref, p1_ref)

    def finish_head(g, _):
        @pl.when(jnp.min(acc_ref[g][HEAD_DIM:HEAD_DIM + 1, :]) < DENOM_FLOOR)
        def _():
            acc_ref[g] = jnp.zeros(acc_ref.shape[1:], F32)

            def exact_chunk(c, m_prev):
                scores(g * nca + c, s0_ref)
                m_new = m_prev
                for i in range(ca // vs):
                    b = jnp.where(jnp.tile(mask_rows(c, i), (1, REP)) > 0, s0_ref[i * vs:(i + 1) * vs, :], NEG_BIG)
                    s1_ref[i * vs:(i + 1) * vs, :] = b
                    m_new = jnp.maximum(m_new, jnp.max(b, axis=0, keepdims=True))
                for i in range(ca // vs):
                    p0_ref[i * vs:(i + 1) * vs, :] = jnp.exp2(s1_ref[i * vs:(i + 1) * vs, :] - m_new).astype(BF16)
                accumulate(g, c, jnp.exp2(m_prev - m_new), p0_ref)
                return m_new

            lax.fori_loop(0, nca, exact_chunk, jnp.full((1, REP * tq), NEG_BIG, F32))

        acc = acc_ref[g]
        out_t = acc[:HEAD_DIM] / acc[HEAD_DIM:HEAD_DIM + 1]
        out_t = jnp.concatenate([out_t[:, r * tq:(r + 1) * tq] for r in range(REP)], axis=0)
        o_ref[:, pl.ds(pl.multiple_of(g * REP * HEAD_DIM, REP * HEAD_DIM), REP * HEAD_DIM)] = (
            out_t.T.astype(o_ref.dtype))
        return 0

    lax.fori_loop(0, N_KV_HEADS, finish_head, 0)


def _dsa(qi_r, w_s, ki_r, q_r, k_r, v_t, k_n, bsz, seq):
    t = q_r.shape[0]
    tq = 256
    assert seq % (2 * tq) == 0
    ksel = min(TOPK_MAX, seq // 4)
    assert ksel <= tq
    nqb = seq // tq
    rowblk = lambda w: pl.BlockSpec((tq, w), lambda b, i: (b * nqb + i, 0))
    return pl.pallas_call(
        functools.partial(_dsa_kernel, tq=tq, seq=seq, ksel=ksel),
        grid=(bsz, nqb),
        in_specs=[rowblk(N_IDX_HEADS * IDX_DIM), rowblk(LANES),
                  pl.BlockSpec((seq, LANES), lambda b, i: (b, 0)),
                  rowblk(ATT_W),
                  pl.BlockSpec((seq, KV_W), lambda b, i: (b, 0)),
                  pl.BlockSpec((1, KV_W, seq), lambda b, i: (b, 0, 0)),
                  pl.BlockSpec((k_n.shape[0] // bsz, SUBLANES, LANES), lambda b, i: (b, 0, 0))],
        out_specs=rowblk(ATT_W),
        out_shape=jax.ShapeDtypeStruct((t, ATT_W), BF16),
        scratch_shapes=[pltpu.VMEM((seq, tq), I32),
                        pltpu.VMEM((seq, tq), I16),
                        pltpu.VMEM((seq, tq), BF16),
                        pltpu.VMEM((IDX_DIM, N_IDX_HEADS * tq), BF16),
                        pltpu.VMEM((tq, N_IDX_HEADS * tq), F32),
                        pltpu.VMEM((tq, N_IDX_HEADS * tq), F32),
                        pltpu.VMEM((N_KV_HEADS, KV_W, REP * tq), BF16),
                        pltpu.VMEM((2 * tq, REP * tq), F32),
                        pltpu.VMEM((2 * tq, REP * tq), F32),
                        pltpu.VMEM((2 * tq, REP * tq), F32),
                        pltpu.VMEM((2 * tq, REP * tq), BF16),
                        pltpu.VMEM((2 * tq, REP * tq), BF16),
                        pltpu.VMEM((2 * tq, REP * tq), BF16),
                        pltpu.VMEM((N_KV_HEADS, HEAD_DIM + ONES_ROWS, REP * tq), F32),
                        pltpu.VMEM((N_KV_HEADS, 1, REP * tq), F32)],
        compiler_params=_cparams(("parallel", "arbitrary")),
        name="dsa",
    )(qi_r, w_s, ki_r, q_r, k_r, v_t, k_n)


def _causal_conv3(cur, halo, w):
    tm = cur.shape[0]
    ext = jnp.concatenate([halo, cur], axis=0)
    return (w[2:3, :] * cur + w[1:2, :] * ext[SUBLANES - 1:SUBLANES - 1 + tm]
            + w[0:1, :] * ext[SUBLANES - 2:SUBLANES - 2 + tm])


def _merge_kernel(x_ref, ga_ref, gb_ref, gc_ref, scx_ref, scb_ref, scc_ref, hx_ref, hc_ref,
                  yb_ref, yc_ref, cw_ref, wa_ref, wb_ref, wc_ref, wo_ref, o_ref, *, tm, seq):
    first = (pl.program_id(0) * tm) % seq == 0
    cx = scc_ref[...].astype(F32) * scx_ref[...].astype(F32)
    halo = hc_ref[...].astype(F32) * hx_ref[...].astype(F32)
    halo = jnp.where(first, 0.0, halo)
    ya = scb_ref[...].astype(F32) * _causal_conv3(cx, halo, cw_ref[...])
    dot = lambda a, w: jnp.dot(a, w[...], preferred_element_type=F32)
    m = jax.nn.sigmoid(ga_ref[...].astype(F32)) * dot(ya.astype(BF16), wa_ref)
    m = m + jax.nn.sigmoid(gb_ref[...].astype(F32)) * dot(yb_ref[...], wb_ref)
    m = m + jax.nn.sigmoid(gc_ref[...].astype(F32)) * dot(yc_ref[...], wc_ref)
    o_ref[...] = x_ref[...] + dot(m.astype(BF16), wo_ref)


def _merge(x2, p2, y_b, y_c, conv_w, w_a, w_b, w_c, w_o, seq):
    t, d = x2.shape
    tm = _pick(seq, (256, 128))
    hb = tm // SUBLANES

    def col(width, off):
        return pl.BlockSpec((tm, width), lambda i, o=off // width: (i, o))

    def halo(off):
        return pl.BlockSpec((SUBLANES, SC_W), lambda i, o=off // SC_W: (jnp.maximum(i * hb - 1, 0), o))

    def const(shape):
        return pl.BlockSpec(shape, lambda i: (0, 0), pipeline_mode=pl.Buffered(1))

    row = lambda w: pl.BlockSpec((tm, w), lambda i: (i, 0))
    return pl.pallas_call(
        functools.partial(_merge_kernel, tm=tm, seq=seq),
        grid=(t // tm,),
        in_specs=[row(d), col(d, 0), col(d, d), col(d, 2 * d),
                  col(SC_W, OFF_SCX), col(SC_W, OFF_SCB), col(SC_W, OFF_SCC),
                  halo(OFF_SCX), halo(OFF_SCC), row(SSM_W), row(ATT_W),
                  const((SC_CONV, SC_W)), const((SC_W, d)), const((SSM_W, d)),
                  const((ATT_W, d)), const((d, d))],
        out_specs=row(d),
        out_shape=jax.ShapeDtypeStruct((t, d), F32),
        compiler_params=_cparams(("parallel",)),
        name="merge",
    )(x2, p2, p2, p2, p2, p2, p2, p2, p2, y_b, y_c, conv_w, w_a, w_b, w_c, w_o)


def _ffn_kernel(x_ref, g_ref, wg_ref, wu_ref, cg_ref, cu_ref, wd_ref, og_ref, o_ref,
                h_ref, halo_ref, *, tm, seq, out_norm):
    j = pl.program_id(1)

    @pl.when(j == 0)
    def _():
        x = x_ref[...]
        ms = jnp.mean(x * x, axis=-1, keepdims=True)
        h_ref[...] = (x * lax.rsqrt(ms + NORM_EPS) * g_ref[...]).astype(BF16)
        o_ref[...] = jnp.zeros_like(o_ref)

    first = (pl.program_id(0) * tm) % seq == 0
    h = h_ref[...]
    gt = jnp.dot(h, wg_ref[...], preferred_element_type=F32)
    ut = jnp.dot(h, wu_ref[...], preferred_element_type=F32)
    gh = jnp.where(first, 0.0, halo_ref[j, 0])
    uh = jnp.where(first, 0.0, halo_ref[j, 1])
    halo_ref[j, 0] = gt[tm - SUBLANES:]
    halo_ref[j, 1] = ut[tm - SUBLANES:]
    gc = _causal_conv3(gt, gh, cg_ref[...])
    uc = _causal_conv3(ut, uh, cu_ref[...])
    act = (jax.nn.silu(gc) * uc).astype(BF16)
    o_ref[...] += jnp.dot(act, wd_ref[...], preferred_element_type=F32)

    @pl.when(j == pl.num_programs(1) - 1)
    def _():
        y = x_ref[...] + o_ref[...]
        if out_norm:
            ms = jnp.mean(y * y, axis=-1, keepdims=True)
            y = y * lax.rsqrt(ms + NORM_EPS) * og_ref[...]
        o_ref[...] = y


def _ffn(x2, gain, w_up, conv_w, w_down, out_gain, seq, out_norm):
    t, d = x2.shape
    dff = w_down.shape[0]
    tm = _pick(seq, (512, 256, 128))
    tf = _pick(dff, (512, 256, 128))
    nf = dff // tf
    return pl.pallas_call(
        functools.partial(_ffn_kernel, tm=tm, seq=seq, out_norm=out_norm),
        grid=(t // tm, nf),
        in_specs=[pl.BlockSpec((tm, d), lambda i, j: (i, 0)),
                  pl.BlockSpec((1, d), lambda i, j: (0, 0)),
                  pl.BlockSpec((d, tf), lambda i, j: (0, j)),
                  pl.BlockSpec((d, tf), lambda i, j: (0, j + nf)),
                  pl.BlockSpec((FFN_CONV, tf), lambda i, j: (0, j)),
                  pl.BlockSpec((FFN_CONV, tf), lambda i, j: (0, j + nf)),
                  pl.BlockSpec((tf, d), lambda i, j: (j, 0)),
                  pl.BlockSpec((1, d), lambda i, j: (0, 0))],
        out_specs=pl.BlockSpec((tm, d), lambda i, j: (i, 0)),
        out_shape=jax.ShapeDtypeStruct((t, d), F32),
        scratch_shapes=[pltpu.VMEM((tm, d), BF16), pltpu.VMEM((nf, 2, SUBLANES, tf), F32)],
        compiler_params=_cparams(("arbitrary", "arbitrary")),
        name="ffn",
    )(x2, gain, w_up, w_up, conv_w, conv_w, w_down, out_gain)


def _prep_w_in(w):
    d = w.shape[0]
    pad = jnp.zeros((d, NP_COLS - w.shape[1]), BF16)
    return jnp.concatenate([w[:, _ORIG_GATE_OFF:].astype(BF16), w[:, :_ORIG_GATE_OFF].astype(BF16), pad], axis=1)


def _prep_ssm(a_re, a_im, log_dt, b_re, b_im, c_re, c_im):
    ar, ai = a_re.astype(F32), a_im.astype(F32)
    dt = jnp.exp(log_dt.astype(F32))[:, None]
    mag = jnp.exp(dt * ar)
    abr, abi = mag * jnp.cos(dt * ai), mag * jnp.sin(dt * ai)
    den = ar * ar + ai * ai
    cr = ((abr - 1.0) * ar + abi * ai) / den
    ci = (abi * ar - (abr - 1.0) * ai) / den
    br, bi = b_re.astype(F32), b_im.astype(F32)
    bbr = cr[..., None] * br - ci[..., None] * bi
    bbi = cr[..., None] * bi + ci[..., None] * br
    eye = jnp.eye(SSM_GROUPS, dtype=F32)

    def bdiag_in(m):
        return (jnp.transpose(m, (0, 2, 1))[:, :, None, :] * eye[:, None, :, None]).reshape(
            SSM_W, SSM_NS)

    def bdiag_out(m):
        return (jnp.transpose(m, (0, 2, 1))[:, :, None, :] * eye[:, None, :, None]).reshape(
            SSM_NS, SSM_W)

    bbd = jnp.concatenate([bdiag_in(bbr), bdiag_in(bbi)], axis=1).astype(BF16)
    cbd = jnp.concatenate([bdiag_out(c_re.astype(F32)), bdiag_out(-c_im.astype(F32))], axis=0).astype(BF16)
    a_rows = jnp.stack([abr.reshape(-1), abi.reshape(-1)], axis=0)
    return bbd, cbd, a_rows


def kernel(x, positions, norm_mix, w_in, sc_conv, ssm_a_re, ssm_a_im, ssm_log_dt, ssm_b_re, ssm_b_im,
           ssm_c_re, ssm_c_im, ssm_d, ssm_glu, w_branch_a, w_branch_b, w_branch_c, w_out, norm_ffn,
           w_up, ffn_conv, w_down, norm_final):
    bsz, seq, d = x.shape
    depth = w_in.shape[0]
    t = bsz * seq
    x2 = x.reshape(t, d).astype(F32)
    tabs = _rope_tables(positions.astype(I32))
    for l in range(depth):
        p2 = _inproj(x2, norm_mix[l].reshape(1, d).astype(F32), _prep_w_in(w_in[l]))
        q_r, k_r, v_t, qi_r, ki_r, w_s, k_n = _rope(p2, tabs, bsz, seq)
        bbd, cbd, a_rows = _prep_ssm(ssm_a_re[l], ssm_a_im[l], ssm_log_dt[l], ssm_b_re[l], ssm_b_im[l],
                                     ssm_c_re[l], ssm_c_im[l])
        y_b = _ssm(p2.reshape(bsz, seq, NP_COLS), bbd, cbd, a_rows,
                   ssm_d[l].reshape(1, SSM_W).astype(F32), ssm_glu[l].astype(BF16))
        y_c = _dsa(qi_r, w_s, ki_r, q_r, k_r, v_t, k_n, bsz, seq)
        x2 = _merge(x2, p2, y_b.reshape(t, SSM_W), y_c, sc_conv[l].astype(F32),
                    w_branch_a[l].astype(BF16), w_branch_b[l].astype(BF16), w_branch_c[l].astype(BF16),
                    w_out[l].astype(BF16), seq)
        x2 = _ffn(x2, norm_ffn[l].reshape(1, d).astype(F32), w_up[l].astype(BF16),
                  ffn_conv[l].astype(F32), w_down[l].astype(BF16),
                  norm_final.reshape(1, d).astype(F32), seq, out_norm=(l == depth - 1))
    return x2.reshape(bsz, seq, d).astype(x.dtype)
```
